```python
import math
import jax, jax.numpy as jnp
from jax import lax
import numpy as np

D_MODEL = 1024
BATCH = 4
SEQ = 8192
DEPTH = 1

HEAD_DIM = 64
RWKV_HEADS = 8
RWKV_W = RWKV_HEADS * HEAD_DIM
DECAY_LORA = 64
AAA_LORA = 64
GATE_LORA = 128
RWKV_COLS = 3 * RWKV_W + DECAY_LORA + AAA_LORA + GATE_LORA
RWKV_GN_EPS = 64e-5
N_Q_HEADS = 8
N_KV_HEADS = 2
GQA_GROUP = N_Q_HEADS // N_KV_HEADS
ATT_Q_W = N_Q_HEADS * HEAD_DIM
ATT_KV_W = N_KV_HEADS * HEAD_DIM
ATT_COLS = ATT_Q_W + 2 * ATT_KV_W
WINDOW = 128
ATT_BLOCK = 128
ATT_SCALE = 1.0 / math.sqrt(HEAD_DIM)
NUM_BUCKETS = 32
MAX_DISTANCE = 128
MIX_W = RWKV_W + ATT_Q_W
IN_COLS = RWKV_COLS + ATT_COLS
D_FF = ((8 * D_MODEL // 3 + 255) // 256) * 256
N_MOD = 6
NORM_EPS = 1e-6
NEG_INF = -1e30

kernel_name = "hybrid_rwkv7_swa_sink_block"


def rmsnorm(x, g):
    x = x.astype(jnp.float32)
    return x * lax.rsqrt(jnp.mean(x * x, axis=-1, keepdims=True) + NORM_EPS) * g


def token_shift(t):
    return jnp.pad(t, ((0, 0), (1, 0), (0, 0)))[:, :-1]


def t5_bucket(n):
    max_exact = NUM_BUCKETS // 2
    is_small = n < max_exact
    n_f = jnp.maximum(n, 1).astype(jnp.float32)
    large = max_exact + (jnp.log(n_f / max_exact) / math.log(MAX_DISTANCE / max_exact)
                         * (NUM_BUCKETS - max_exact)).astype(jnp.int32)
    large = jnp.minimum(large, NUM_BUCKETS - 1)
    return jnp.where(is_small, n, large)


def rwkv7_group(p, rwkv_mu, w0, w_lora_up, a0, a_lora_up, g_lora_up, k_k, k_a, r_k, lnx_g, lnx_b):
    B, S, _ = p.shape
    p = p + (token_shift(p) - p) * rwkv_mu
    o = 0
    r = p[..., o:o + RWKV_W]; o += RWKV_W
    k = p[..., o:o + RWKV_W]; o += RWKV_W
    v = p[..., o:o + RWKV_W]; o += RWKV_W
    xw = p[..., o:o + DECAY_LORA]; o += DECAY_LORA
    xa = p[..., o:o + AAA_LORA]; o += AAA_LORA
    xg = p[..., o:o + GATE_LORA]

    w = -jax.nn.softplus(-(w0 + jnp.tanh(xw) @ w_lora_up)) - 0.5
    decay = jnp.exp(-jnp.exp(w))
    a = jax.nn.sigmoid(a0 + xa @ a_lora_up)
    g = jax.nn.sigmoid(xg) @ g_lora_up

    hs = (B, S, RWKV_HEADS, HEAD_DIM)
    kk = (k * k_k).reshape(hs)
    kk = kk / jnp.maximum(jnp.linalg.norm(kk, axis=-1, keepdims=True), 1e-12)
    k = k * (1.0 + (a - 1.0) * k_a)

    r_h, k_h, v_h = r.reshape(hs), k.reshape(hs), v.reshape(hs)
    a_h = a.reshape(hs)
    tm = lambda t: jnp.transpose(t, (1, 0, 2, 3)).astype(jnp.float32)
    xs = (tm(r_h), tm(decay.reshape(hs)), tm(k_h), tm(v_h), tm(-kk), tm(kk * a_h))

    def step(state, inp):
        r_t, w_t, k_t, v_t, a_t, b_t = inp
        sa = jnp.einsum('bhij,bhj->bhi', state, a_t)
        state = (state * w_t[:, :, None, :] + sa[..., None] * b_t[:, :, None, :]
                 + v_t[..., None] * k_t[:, :, None, :])
        return state, jnp.einsum('bhij,bhj->bhi', state, r_t)

    s0 = jnp.zeros((B, RWKV_HEADS, HEAD_DIM, HEAD_DIM), jnp.float32)
    _, y = lax.scan(step, s0, xs)
    y = jnp.transpose(y, (1, 0, 2, 3))

    mu = jnp.mean(y, axis=-1, keepdims=True)
    var = jnp.mean(jnp.square(y - mu), axis=-1, keepdims=True)
    y = ((y - mu) * lax.rsqrt(var + RWKV_GN_EPS)).reshape(B, S, RWKV_W) * lnx_g + lnx_b
    bonus = jnp.sum(r_h * k_h * r_k, axis=-1, keepdims=True) * v_h
    return (y + bonus.reshape(B, S, RWKV_W)) * g


def swa_group(p, q_norm_g, k_norm_g, sinks, rel_bias):
    B, S, _ = p.shape
    nb = S // ATT_BLOCK
    q = p[..., :ATT_Q_W].reshape(B, S, N_KV_HEADS, GQA_GROUP, HEAD_DIM)
    k = p[..., ATT_Q_W:ATT_Q_W + ATT_KV_W].reshape(B, S, N_KV_HEADS, HEAD_DIM)
    v = p[..., ATT_Q_W + ATT_KV_W:].reshape(B, S, N_KV_HEADS, HEAD_DIM)
    q = rmsnorm(q, q_norm_g).reshape(B, nb, ATT_BLOCK, N_KV_HEADS, GQA_GROUP, HEAD_DIM)
    k = rmsnorm(k, k_norm_g)

    def windows(t):
        t = jnp.pad(t, ((0, 0), (ATT_BLOCK, 0), (0, 0), (0, 0)))
        t = t.reshape(B, nb + 1, ATT_BLOCK, N_KV_HEADS, HEAD_DIM)
        return jnp.concatenate([t[:, :-1], t[:, 1:]], axis=2)

    kw, vw = windows(k), windows(v.astype(jnp.float32))
    logits = jnp.einsum('bnqhgd,bnkhd->bnhgqk', q, kw) * ATT_SCALE

    qi = jnp.arange(ATT_BLOCK)[:, None]
    kj = jnp.arange(2 * ATT_BLOCK)[None, :]
    dist = qi + ATT_BLOCK - kj
    bias = rel_bias[t5_bucket(jnp.maximum(dist, 0))]
    bias = jnp.transpose(bias, (2, 0, 1)).reshape(N_KV_HEADS, GQA_GROUP, ATT_BLOCK, 2 * ATT_BLOCK)
    key_pos = jnp.arange(nb)[:, None, None] * ATT_BLOCK - ATT_BLOCK + kj[None]
    valid = ((dist >= 0) & (dist < WINDOW))[None] & (key_pos >= 0)
    logits = jnp.where(valid[None, :, None, None], logits + bias.astype(jnp.float32), NEG_INF)

    sink = jnp.broadcast_to(sinks.reshape(N_KV_HEADS, GQA_GROUP, 1, 1).astype(jnp.float32),
                            logits.shape[:-1] + (1,))
    probs = jax.nn.softmax(jnp.concatenate([logits, sink], axis=-1), axis=-1)[..., :-1]
    out = jnp.einsum('bnhgqk,bnkhd->bnqhgd', probs, vw)
    return out.reshape(B, S, ATT_Q_W)


def setup_inputs(seed: int = 0) -> dict:
    key = jax.random.key(seed)
    ks = jax.random.split(key, 32)
    f = jnp.float32
    nrm = lambda i, shape, s: jax.random.normal(ks[i], shape, f) * s
    return {
        "x": nrm(0, (BATCH, SEQ, D_MODEL), 1.0),
        "c": nrm(1, (BATCH, D_MODEL), 1.0),
        "w_ada": nrm(2, (D_MODEL, N_MOD * D_MODEL), 0.5 * D_MODEL ** -0.5),
        "b_ada": nrm(3, (N_MOD * D_MODEL,), 0.01),
        "norm1_g": 1.0 + nrm(4, (D_MODEL,), 0.02),
        "w_in": nrm(5, (D_MODEL, IN_COLS), D_MODEL ** -0.5),
        "rwkv_mu": jax.random.uniform(ks[6], (RWKV_COLS,), f),
        "w0": nrm(7, (RWKV_W,), 0.5),
        "w_lora_up": nrm(8, (DECAY_LORA, RWKV_W), 0.5 * DECAY_LORA ** -0.5),
        "a0": nrm(9, (RWKV_W,), 0.1),
        "a_lora_up": nrm(10, (AAA_LORA, RWKV_W), 0.5 * AAA_LORA ** -0.5),
        "g_lora_up": nrm(11, (GATE_LORA, RWKV_W), GATE_LORA ** -0.5),
        "k_k": 0.85 + nrm(12, (RWKV_W,), 0.02),
        "k_a": 1.0 + nrm(13, (RWKV_W,), 0.02),
        "r_k": nrm(14, (RWKV_HEADS, HEAD_DIM), 0.1),
        "lnx_g": 1.0 + nrm(15, (RWKV_W,), 0.02),
        "lnx_b": nrm(16, (RWKV_W,), 0.01),
        "q_norm_g": 1.0 + nrm(17, (HEAD_DIM,), 0.02),
        "k_norm_g": 1.0 + nrm(18, (HEAD_DIM,), 0.02),
        "sinks": nrm(19, (N_Q_HEADS,), 0.5),
        "rel_bias": nrm(20, (NUM_BUCKETS, N_Q_HEADS), 0.5),
        "w_out": nrm(21, (MIX_W, D_MODEL), MIX_W ** -0.5),
        "norm2_g": 1.0 + nrm(22, (D_MODEL,), 0.02),
        "w_gate": nrm(23, (D_MODEL, D_FF), D_MODEL ** -0.5),
        "w_up": nrm(24, (D_MODEL, D_FF), D_MODEL ** -0.5),
        "w_down": nrm(25, (D_FF, D_MODEL), D_FF ** -0.5),
    }


def reference(x, c, w_ada, b_ada, norm1_g, w_in, rwkv_mu, w0, w_lora_up, a0, a_lora_up,
              g_lora_up, k_k, k_a, r_k, lnx_g, lnx_b, q_norm_g, k_norm_g, sinks, rel_bias,
              w_out, norm2_g, w_gate, w_up, w_down):
    out_dtype = x.dtype
    h_res = x.astype(jnp.float32)
    mod = jax.nn.silu(c.astype(jnp.float32)) @ w_ada + b_ada
    shift1, scale1, gate1, shift2, scale2, gate2 = [m[:, None, :] for m in jnp.split(mod, N_MOD, axis=-1)]

    for _ in range(DEPTH):
        h = rmsnorm(h_res, norm1_g) * (1.0 + scale1) + shift1
        proj = h @ w_in
        y_rwkv = rwkv7_group(proj[..., :RWKV_COLS], rwkv_mu, w0, w_lora_up, a0, a_lora_up,
                             g_lora_up, k_k, k_a, r_k, lnx_g, lnx_b)
        y_att = swa_group(proj[..., RWKV_COLS:], q_norm_g, k_norm_g, sinks, rel_bias)
        mix = jnp.concatenate([y_rwkv, y_att], axis=-1) @ w_out
        h_res = h_res + gate1 * mix
        h2 = rmsnorm(h_res, norm2_g) * (1.0 + scale2) + shift2
        ffn = (jax.nn.silu(h2 @ w_gate) * (h2 @ w_up)) @ w_down
        h_res = h_res + gate2 * ffn

    return h_res.astype(out_dtype)
```

```python
import functools
import math

import jax
import jax.numpy as jnp
from jax import lax
from jax.experimental import pallas as pl
from jax.experimental.pallas import tpu as pltpu

F32 = jnp.float32
BF16 = jnp.bfloat16

HEAD_DIM = 64
RWKV_HEADS = 8
RWKV_W = RWKV_HEADS * HEAD_DIM
DECAY_LORA = 64
AAA_LORA = 64
GATE_LORA = 128
RWKV_COLS = 3 * RWKV_W + DECAY_LORA + AAA_LORA + GATE_LORA
RWKV_GN_EPS = 64e-5
N_Q_HEADS = 8
N_KV_HEADS = 2
GQA_GROUP = N_Q_HEADS // N_KV_HEADS
ATT_Q_W = N_Q_HEADS * HEAD_DIM
ATT_KV_W = N_KV_HEADS * HEAD_DIM
ATT_BLOCK = 128
WINDOW = 128
ATT_SCALE = 1.0 / math.sqrt(HEAD_DIM)
NUM_BUCKETS = 32
MAX_DISTANCE = 128
N_MOD = 6
NORM_EPS = 1e-6
NEG_INF = -1e30

CHUNK = 64
IN_TILE = 256
RWKV_TILE = 128
FFN_TILE = 512
FFN_SPLIT = 4
VMEM_LIMIT = 56 * 1024 * 1024


def _dot(a, b):
    return jnp.dot(a.astype(BF16), b.astype(BF16), preferred_element_type=F32)


def _dot_nt(a, b):
    return lax.dot_general(a.astype(BF16), b.astype(BF16), (((1,), (1,)), ((), ())),
                           preferred_element_type=F32)


def _dot_tn(a, b):
    return lax.dot_general(a.astype(BF16), b.astype(BF16), (((0,), (0,)), ((), ())),
                           preferred_element_type=F32)


def _split(x):
    hi = x.astype(BF16)
    lo = (x - hi.astype(F32)).astype(BF16)
    return hi, lo


def _dot_rhs_exact(x, m):
    hi, lo = _split(x)
    return (jnp.dot(hi, m, preferred_element_type=F32) + jnp.dot(lo, m, preferred_element_type=F32))


def _dot_lhs_exact(m, x):
    hi, lo = _split(x)
    return (jnp.dot(m, hi, preferred_element_type=F32) + jnp.dot(m, lo, preferred_element_type=F32))


def _sigmoid(z):
    return 1.0 / (1.0 + jnp.exp(-z))


def _mod_kernel(c_ref, w_ref, b_ref, o_ref):
    c = c_ref[...]
    s = c * _sigmoid(c)
    s_hi, s_lo = _split(s)
    w = w_ref[...]
    w_hi, w_lo = _split(w)
    d = lambda x, y: jnp.dot(x, y, preferred_element_type=F32)
    o_ref[...] = d(s_hi, w_hi) + d(s_hi, w_lo) + d(s_lo, w_hi) + b_ref[...]


def _modulation(c, w_ada, b_ada):
    B, D = c.shape
    n = w_ada.shape[1]
    blk = D
    return pl.pallas_call(
        _mod_kernel,
        grid=(n // blk,),
        in_specs=[pl.BlockSpec((B, D), lambda j: (0, 0)),
                  pl.BlockSpec((D, blk), lambda j: (0, j)),
                  pl.BlockSpec((1, blk), lambda j: (0, j))],
        out_specs=pl.BlockSpec((B, blk), lambda j: (0, j)),
        out_shape=jax.ShapeDtypeStruct((B, n), F32),
        name="adaln_mod",
    )(c, w_ada, b_ada.reshape(1, n))


def _bias_kernel(rb_ref, o_ref):
    qi = lax.broadcasted_iota(jnp.int32, (ATT_BLOCK, 2 * ATT_BLOCK), 0)
    kj = lax.broadcasted_iota(jnp.int32, (ATT_BLOCK, 2 * ATT_BLOCK), 1)
    dist = qi + ATT_BLOCK - kj
    n = jnp.maximum(dist, 0)
    max_exact = NUM_BUCKETS // 2
    n_f = jnp.maximum(n, 1).astype(F32)
    large = max_exact + (jnp.log(n_f / max_exact) / math.log(MAX_DISTANCE / max_exact)
                         * (NUM_BUCKETS - max_exact)).astype(jnp.int32)
    large = jnp.minimum(large, NUM_BUCKETS - 1)
    bucket = jnp.where(n < max_exact, n, large)
    valid = (dist >= 0) & (dist < WINDOW)
    for h in range(N_Q_HEADS):
        acc = jnp.zeros((ATT_BLOCK, 2 * ATT_BLOCK), F32)
        for b in range(NUM_BUCKETS):
            acc = jnp.where(bucket == b, rb_ref[b, h], acc)
        o_ref[h] = jnp.where(valid, acc, NEG_INF)


def _bias_table(rel_bias):
    return pl.pallas_call(
        _bias_kernel,
        in_specs=[pl.BlockSpec(memory_space=pltpu.SMEM)],
        out_specs=pl.BlockSpec(memory_space=pltpu.VMEM),
        out_shape=jax.ShapeDtypeStruct((N_Q_HEADS, ATT_BLOCK, 2 * ATT_BLOCK), F32),
        name="rel_bias_table",
    )(rel_bias)


def _inproj_kernel(x_ref, mod_ref, g1_ref, win_ref, mu_ref, w0_ref, wlu_ref, a0_ref, alu_ref,
                   glu_ref, kk_ref, ka_ref, qg_ref, kg_ref, seg_ref,
                   r_out, lw_out, k_out, v_out, kn_out, b_out, g_out, q_out, ak_out, av_out,
                   carry_ref):
    @pl.when(pl.program_id(1) == 0)
    def _():
        carry_ref[...] = jnp.zeros_like(carry_ref)

    x = x_ref[0]
    tm = x.shape[0]
    ms = jnp.mean(x * x, axis=-1, keepdims=True)
    h = x * lax.rsqrt(ms + NORM_EPS) * g1_ref[...] * (1.0 + mod_ref[0, 1:2, :]) + mod_ref[0, 0:1, :]
    p = _dot(h, win_ref[...])

    pr = p[:, :RWKV_COLS]
    row = lax.broadcasted_iota(jnp.int32, pr.shape, 0)
    prev = jnp.where(row == 0, carry_ref[...], pltpu.roll(pr, 1, 0))
    carry_ref[...] = pr[tm - 1:tm, :]
    pm = pr + (prev - pr) * mu_ref[...]

    r = pm[:, 0:RWKV_W]
    k = pm[:, RWKV_W:2 * RWKV_W]
    v = pm[:, 2 * RWKV_W:3 * RWKV_W]
    o = 3 * RWKV_W
    xw = pm[:, o:o + DECAY_LORA]
    xa = pm[:, o + DECAY_LORA:o + DECAY_LORA + AAA_LORA]
    xg = pm[:, o + DECAY_LORA + AAA_LORA:RWKV_COLS]

    seg = seg_ref[...]
    z = w0_ref[...] + _dot(jnp.tanh(xw), wlu_ref[...])
    lw_out[0] = -math.exp(-0.5) * _sigmoid(z)
    a = _sigmoid(a0_ref[...] + _dot(xa, alu_ref[...]))
    g_out[0] = _dot(_sigmoid(xg), glu_ref[...])
    kk = k * kk_ref[...]
    nrm = jnp.sqrt(_dot_rhs_exact(kk * kk, seg))
    kk = kk / jnp.maximum(nrm, 1e-12)
    kn_out[0] = kk
    b_out[0] = kk * a
    k_out[0] = k * (1.0 + (a - 1.0) * ka_ref[...])
    r_out[0] = r
    v_out[0] = v

    q = p[:, RWKV_COLS:RWKV_COLS + ATT_Q_W]
    ak = p[:, RWKV_COLS + ATT_Q_W:RWKV_COLS + ATT_Q_W + ATT_KV_W]
    av = p[:, RWKV_COLS + ATT_Q_W + ATT_KV_W:]
    qms = _dot_rhs_exact(q * q, seg) * (1.0 / HEAD_DIM)
    q_out[0] = (q * lax.rsqrt(qms + NORM_EPS) * (qg_ref[...] * ATT_SCALE)).astype(BF16)
    kms = _dot_rhs_exact(ak * ak, seg[:ATT_KV_W, :ATT_KV_W]) * (1.0 / HEAD_DIM)
    ak_out[0] = (ak * lax.rsqrt(kms + NORM_EPS) * kg_ref[...]).astype(BF16)
    av_out[0] = av.astype(BF16)


def _in_projection(x, mod3, g1, w_in, mu, w0, wlu, a0, alu, glu, k_k, k_a, qg, kg, seg):
    B, S, D = x.shape
    tm = min(IN_TILE, S)
    const = lambda shape: pl.BlockSpec(shape, lambda b, j: (0,) * len(shape))
    row_spec = lambda w: pl.BlockSpec((1, tm, w), lambda b, j: (b, j, 0))
    f32_out = jax.ShapeDtypeStruct((B, S, RWKV_W), F32)
    return pl.pallas_call(
        _inproj_kernel,
        grid=(B, S // tm),
        in_specs=[row_spec(D),
                  pl.BlockSpec((1, N_MOD, D), lambda b, j: (b, 0, 0)),
                  const(g1.shape), const(w_in.shape), const(mu.shape), const(w0.shape),
                  const(wlu.shape), const(a0.shape), const(alu.shape), const(glu.shape),
                  const(k_k.shape), const(k_a.shape), const(qg.shape), const(kg.shape),
                  const(seg.shape)],
        out_specs=[row_spec(RWKV_W)] * 7 + [row_spec(ATT_Q_W), row_spec(ATT_KV_W), row_spec(ATT_KV_W)],
        out_shape=[f32_out] * 7 + [jax.ShapeDtypeStruct((B, S, ATT_Q_W), BF16),
                                   jax.ShapeDtypeStruct((B, S, ATT_KV_W), BF16),
                                   jax.ShapeDtypeStruct((B, S, ATT_KV_W), BF16)],
        scratch_shapes=[pltpu.VMEM((1, RWKV_COLS), F32)],
        compiler_params=pltpu.CompilerParams(dimension_semantics=("arbitrary", "arbitrary"),
                                             vmem_limit_bytes=VMEM_LIMIT),
        name="in_projection",
    )(x, mod3, g1, w_in, mu, w0, wlu, a0, alu, glu, k_k, k_a, qg, kg, seg)


def _rwkv_kernel(r_ref, lw_ref, k_ref, v_ref, kn_ref, b_ref, g_ref, rk_ref, lg_ref, lb_ref,
                 seg_ref, tri_ref, o_ref, state_ref, y_ref):
    @pl.when(pl.program_id(1) == 0)
    def _():
        state_ref[...] = jnp.zeros_like(state_ref)

    L = CHUNK
    ii = lax.broadcasted_iota(jnp.int32, (L, L), 0)
    jj = lax.broadcasted_iota(jnp.int32, (L, L), 1)
    strict = ii > jj
    incl = ii >= jj
    diag = ii == jj
    eye = jnp.where(diag, 1.0, 0.0).astype(F32)
    tri = tri_ref[...]
    seg = seg_ref[...]

    for c in range(r_ref.shape[1] // L):
        rows = pl.ds(c * L, L)
        r = r_ref[0, rows, :]
        lw = lw_ref[0, rows, :]
        k = k_ref[0, rows, :]
        v = v_ref[0, rows, :]
        kn = kn_ref[0, rows, :]
        bv = b_ref[0, rows, :]

        cum = _dot_lhs_exact(tri, lw)
        cl = cum[L - 1:L, :]
        e_neg = jnp.exp(-cum)
        a_t = -kn * jnp.exp(cum - lw)
        b_t = bv * e_neg
        k_t = k * e_neg
        r_t = r * jnp.exp(cum)
        e_hat = jnp.exp(cl - cum)
        b_h = bv * e_hat
        k_h = k * e_hat
        w_l = jnp.exp(cl)

        for h in range(RWKV_HEADS):
            hs = slice(h * HEAD_DIM, (h + 1) * HEAD_DIM)
            at, bt, kt, rt = a_t[:, hs], b_t[:, hs], k_t[:, hs], r_t[:, hs]
            bh, kh, vh = b_h[:, hs], k_h[:, hs], v[:, hs]
            a_ab = jnp.where(strict, _dot_nt(at, bt), 0.0)
            a_ak = jnp.where(strict, _dot_nt(at, kt), 0.0)
            a_rb = jnp.where(incl, _dot_nt(rt, bt), 0.0)
            a_rk = jnp.where(incl, _dot_nt(rt, kt), 0.0)
            t_inv = eye + a_ab
            a_pow = a_ab
            n = 1
            while 2 * n < L:
                a_pow = _dot(a_pow, a_pow)
                t_inv = t_inv + _dot(t_inv, a_pow)
                n *= 2
            p_m = _dot(t_inv, at)
            q_m = _dot(t_inv, _dot(a_ak, vh))
            g_m = rt + _dot(a_rb, p_m)
            y_i = _dot(a_rb, q_m) + _dot(a_rk, vh)
            m_t = jnp.where(diag, w_l[:, hs], 0.0) + _dot_tn(bh, p_m)
            n_t = _dot_tn(bh, q_m) + _dot_tn(kh, vh)
            st = state_ref[h]
            y_ref[:, hs] = _dot(g_m, st) + y_i
            state_ref[h] = _dot(m_t, st) + n_t

        y = y_ref[...]
        mu = _dot_rhs_exact(y, seg) * (1.0 / HEAD_DIM)
        yc = y - mu
        var = _dot_rhs_exact(yc * yc, seg) * (1.0 / HEAD_DIM)
        yn = yc * lax.rsqrt(var + RWKV_GN_EPS) * lg_ref[...] + lb_ref[...]
        bonus = _dot_rhs_exact(r * k * rk_ref[...], seg) * v
        o_ref[0, rows, :] = ((yn + bonus) * g_ref[0, rows, :]).astype(BF16)


def _rwkv(r, lw, k, v, kn, bv, g, r_k, lnx_g, lnx_b, seg, tri):
    B, S, W = r.shape
    tc = min(RWKV_TILE, S)
    const = lambda shape: pl.BlockSpec(shape, lambda b, j: (0,) * len(shape))
    row_spec = pl.BlockSpec((1, tc, W), lambda b, j: (b, j, 0))
    return pl.pallas_call(
        _rwkv_kernel,
        grid=(B, S // tc),
        in_specs=[row_spec] * 7 + [const(r_k.shape), const(lnx_g.shape), const(lnx_b.shape),
                                   const(seg.shape), const(tri.shape)],
        out_specs=row_spec,
        out_shape=jax.ShapeDtypeStruct((B, S, W), BF16),
        scratch_shapes=[pltpu.VMEM((RWKV_HEADS, HEAD_DIM, HEAD_DIM), F32),
                        pltpu.VMEM((CHUNK, W), F32)],
        compiler_params=pltpu.CompilerParams(dimension_semantics=("arbitrary", "arbitrary"),
                                             vmem_limit_bytes=VMEM_LIMIT),
        name="rwkv7_chunked",
    )(r, lw, k, v, kn, bv, g, r_k, lnx_g, lnx_b, seg, tri)


def _swa_kernel(q_ref, kc_ref, kp_ref, vc_ref, vp_ref, bias_ref, sink_ref, o_ref):
    first = pl.program_id(1) == 0
    col = lax.broadcasted_iota(jnp.int32, (ATT_BLOCK, 2 * ATT_BLOCK), 1)
    no_prev = jnp.logical_and(first, col < ATT_BLOCK)
    q = q_ref[0]
    for hk in range(N_KV_HEADS):
        ks = slice(hk * HEAD_DIM, (hk + 1) * HEAD_DIM)
        kw = jnp.concatenate([kp_ref[0, :, ks], kc_ref[0, :, ks]], axis=0)
        vw = jnp.concatenate([vp_ref[0, :, ks], vc_ref[0, :, ks]], axis=0)
        for gq in range(GQA_GROUP):
            h = hk * GQA_GROUP + gq
            qs = slice(h * HEAD_DIM, (h + 1) * HEAD_DIM)
            logits = lax.dot_general(q[:, qs], kw, (((1,), (1,)), ((), ())),
                                     preferred_element_type=F32)
            logits = jnp.where(no_prev, NEG_INF, logits + bias_ref[h])
            sink = sink_ref[h]
            m = jnp.maximum(jnp.max(logits, axis=-1, keepdims=True), sink)
            e = jnp.exp(logits - m)
            den = jnp.sum(e, axis=-1, keepdims=True) + jnp.exp(sink - m)
            out = jnp.dot(e.astype(BF16), vw, preferred_element_type=F32) / den
            o_ref[0, :, qs] = out.astype(BF16)


def _swa(q, ak, av, bias, sinks):
    B, S, _ = q.shape
    nb = S // ATT_BLOCK
    cur = lambda w: pl.BlockSpec((1, ATT_BLOCK, w), lambda b, i: (b, i, 0))
    prev = lambda w: pl.BlockSpec((1, ATT_BLOCK, w), lambda b, i: (b, jnp.maximum(i - 1, 0), 0))
    return pl.pallas_call(
        _swa_kernel,
        grid=(B, nb),
        in_specs=[cur(ATT_Q_W), cur(ATT_KV_W), prev(ATT_KV_W), cur(ATT_KV_W), prev(ATT_KV_W),
                  pl.BlockSpec(bias.shape, lambda b, i: (0, 0, 0)),
                  pl.BlockSpec(memory_space=pltpu.SMEM)],
        out_specs=cur(ATT_Q_W),
        out_shape=jax.ShapeDtypeStruct((B, S, ATT_Q_W), BF16),
        compiler_params=pltpu.CompilerParams(dimension_semantics=("arbitrary", "arbitrary")),
        name="swa_attention",
    )(q, ak, ak, av, av, bias, sinks)


def _ffn_kernel(x_ref, yr_ref, ya_ref, mod_ref, g2_ref, wo_ref, wg_ref, wu_ref, wd_ref, o_ref):
    x = x_ref[0]
    wo = wo_ref[...]
    mix = (jnp.dot(yr_ref[0], wo[:RWKV_W], preferred_element_type=F32)
           + jnp.dot(ya_ref[0], wo[RWKV_W:], preferred_element_type=F32))
    h_res = x + mod_ref[0, 2:3, :] * mix
    ms = jnp.mean(h_res * h_res, axis=-1, keepdims=True)
    h2 = (h_res * lax.rsqrt(ms + NORM_EPS) * g2_ref[...] * (1.0 + mod_ref[0, 4:5, :])
          + mod_ref[0, 3:4, :]).astype(BF16)
    d_ff = wg_ref.shape[1]
    blk = d_ff // FFN_SPLIT
    ffn = jnp.zeros_like(x)
    for s in range(FFN_SPLIT):
        cs = slice(s * blk, (s + 1) * blk)
        gt = jnp.dot(h2, wg_ref[:, cs], preferred_element_type=F32)
        up = jnp.dot(h2, wu_ref[:, cs], preferred_element_type=F32)
        act = (gt * _sigmoid(gt) * up).astype(BF16)
        ffn = ffn + jnp.dot(act, wd_ref[cs, :], preferred_element_type=F32)
    o_ref[0] = h_res + mod_ref[0, 5:6, :] * ffn


def _out_ffn(x, y_rwkv, y_att, mod3, g2, w_out, w_gate, w_up, w_down):
    B, S, D = x.shape
    tm = min(FFN_TILE, S)
    resident = lambda shape: pl.BlockSpec(shape, lambda b, j: (0,) * len(shape),
                                          pipeline_mode=pl.Buffered(1))
    row_spec = lambda w: pl.BlockSpec((1, tm, w), lambda b, j: (b, j, 0))
    return pl.pallas_call(
        _ffn_kernel,
        grid=(B, S // tm),
        in_specs=[row_spec(D), row_spec(RWKV_W), row_spec(ATT_Q_W),
                  pl.BlockSpec((1, N_MOD, D), lambda b, j: (b, 0, 0)),
                  resident(g2.shape), resident(w_out.shape), resident(w_gate.shape),
                  resident(w_up.shape), resident(w_down.shape)],
        out_specs=row_spec(D),
        out_shape=jax.ShapeDtypeStruct((B, S, D), F32),
        compiler_params=pltpu.CompilerParams(dimension_semantics=("arbitrary", "arbitrary"),
                                             vmem_limit_bytes=VMEM_LIMIT),
        name="out_proj_ffn",
    )(x, y_rwkv, y_att, mod3, g2, w_out, w_gate, w_up, w_down)


def kernel(x, c, w_ada, b_ada, norm1_g, w_in, rwkv_mu, w0, w_lora_up, a0, a_lora_up, g_lora_up, k_k, k_a, r_k, lnx_g, lnx_b, q_norm_g, k_norm_g, sinks, rel_bias, w_out, norm2_g, w_gate, w_up, w_down):
    B, S, D = x.shape
    row = lambda t: t.reshape(1, -1).astype(F32)
    lane = jnp.arange(RWKV_W) // HEAD_DIM
    seg = (lane[:, None] == lane[None, :]).astype(BF16)
    t_idx = jnp.arange(CHUNK)
    tri = (t_idx[:, None] >= t_idx[None, :]).astype(BF16)

    mod3 = _modulation(c.astype(F32), w_ada, b_ada).reshape(B, N_MOD, D)
    bias = _bias_table(rel_bias)

    r, lw, k, v, kn, bv, g, q, ak, av = _in_projection(
        x, mod3, row(norm1_g), w_in.astype(BF16), row(rwkv_mu), row(w0), w_lora_up.astype(BF16),
        row(a0), a_lora_up.astype(BF16), g_lora_up.astype(BF16), row(k_k), row(k_a),
        row(jnp.tile(q_norm_g, N_Q_HEADS)), row(jnp.tile(k_norm_g, N_KV_HEADS)), seg)

    y_rwkv = _rwkv(r, lw, k, v, kn, bv, g, row(r_k), row(lnx_g), row(lnx_b), seg, tri)
    y_att = _swa(q, ak, av, bias, sinks)

    out = _out_ffn(x, y_rwkv, y_att, mod3, row(norm2_g), w_out.astype(BF16),
                   w_gate.astype(BF16), w_up.astype(BF16), w_down.astype(BF16))
    return out.astype(x.dtype)
```

```python
import functools
import math

import jax
import jax.numpy as jnp
from jax import lax
from jax.experimental import pallas as pl
from jax.experimental.pallas import tpu as pltpu

F32 = jnp.float32
BF16 = jnp.bfloat16

HEAD_DIM = 64
RWKV_HEADS = 8
RWKV_W = RWKV_HEADS * HEAD_DIM
DECAY_LORA = 64
AAA_LORA = 64
GATE_LORA = 128
RWKV_COLS = 3 * RWKV_W + DECAY_LORA + AAA_LORA + GATE_LORA
RWKV_GN_EPS = 64e-5
N_Q_HEADS = 8
N_KV_HEADS = 2
GQA_GROUP = N_Q_HEADS // N_KV_HEADS
ATT_Q_W = N_Q_HEADS * HEAD_DIM
ATT_KV_W = N_KV_HEADS * HEAD_DIM
ATT_BLOCK = 128
WINDOW = 128
ATT_SCALE = 1.0 / math.sqrt(HEAD_DIM)
NUM_BUCKETS = 32
MAX_DISTANCE = 128
N_MOD = 6
NORM_EPS = 1e-6
NEG_INF = -1e30

CHUNK = 64
IN_TILE = 256
RWKV_TILE = 128
FFN_TILE = 512
FFN_SPLIT = 4
VMEM_LIMIT = 56 * 1024 * 1024


def _dot(a, b):
    return jnp.dot(a.astype(BF16), b.astype(BF16), preferred_element_type=F32)


def _dot_nt(a, b):
    return lax.dot_general(a.astype(BF16), b.astype(BF16), (((1,), (1,)), ((), ())),
                           preferred_element_type=F32)


def _dot_tn(a, b):
    return lax.dot_general(a.astype(BF16), b.astype(BF16), (((0,), (0,)), ((), ())),
                           preferred_element_type=F32)


def _split(x):
    hi = x.astype(BF16)
    lo = (x - hi.astype(F32)).astype(BF16)
    return hi, lo


def _dot_rhs_exact(x, m):
    hi, lo = _split(x)
    return (jnp.dot(hi, m, preferred_element_type=F32) + jnp.dot(lo, m, preferred_element_type=F32))


def _dot_lhs_exact(m, x):
    hi, lo = _split(x)
    return (jnp.dot(m, hi, preferred_element_type=F32) + jnp.dot(m, lo, preferred_element_type=F32))


def _sigmoid(z):
    return 1.0 / (1.0 + jnp.exp(-z))


def _mod_kernel(c_ref, w_ref, b_ref, o_ref):
    c = c_ref[...]
    s = c * _sigmoid(c)
    s_hi, s_lo = _split(s)
    w = w_ref[...]
    w_hi, w_lo = _split(w)
    d = lambda x, y: jnp.dot(x, y, preferred_element_type=F32)
    o_ref[...] = d(s_hi, w_hi) + d(s_hi, w_lo) + d(s_lo, w_hi) + b_ref[...]


def _modulation(c, w_ada, b_ada):
    B, D = c.shape
    n = w_ada.shape[1]
    blk = D
    return pl.pallas_call(
        _mod_kernel,
        grid=(n // blk,),
        in_specs=[pl.BlockSpec((B, D), lambda j: (0, 0)),
                  pl.BlockSpec((D, blk), lambda j: (0, j)),
                  pl.BlockSpec((1, blk), lambda j: (0, j))],
        out_specs=pl.BlockSpec((B, blk), lambda j: (0, j)),
        out_shape=jax.ShapeDtypeStruct((B, n), F32),
        name="adaln_mod",
    )(c, w_ada, b_ada.reshape(1, n))


def _bias_kernel(rb_ref, o_ref):
    qi = lax.broadcasted_iota(jnp.int32, (ATT_BLOCK, 2 * ATT_BLOCK), 0)
    kj = lax.broadcasted_iota(jnp.int32, (ATT_BLOCK, 2 * ATT_BLOCK), 1)
    dist = qi + ATT_BLOCK - kj
    n = jnp.maximum(dist, 0)
    max_exact = NUM_BUCKETS // 2
    n_f = jnp.maximum(n, 1).astype(F32)
    large = max_exact + (jnp.log(n_f / max_exact) / math.log(MAX_DISTANCE / max_exact)
                         * (NUM_BUCKETS - max_exact)).astype(jnp.int32)
    large = jnp.minimum(large, NUM_BUCKETS - 1)
    bucket = jnp.where(n < max_exact, n, large)
    valid = (dist >= 0) & (dist < WINDOW)
    for h in range(N_Q_HEADS):
        acc = jnp.zeros((ATT_BLOCK, 2 * ATT_BLOCK), F32)
        for b in range(NUM_BUCKETS):
            acc = jnp.where(bucket == b, rb_ref[b, h], acc)
        o_ref[h] = jnp.where(valid, acc, NEG_INF)


def _bias_table(rel_bias):
    return pl.pallas_call(
        _bias_kernel,
        in_specs=[pl.BlockSpec(memory_space=pltpu.SMEM)],
        out_specs=pl.BlockSpec(memory_space=pltpu.VMEM),
        out_shape=jax.ShapeDtypeStruct((N_Q_HEADS, ATT_BLOCK, 2 * ATT_BLOCK), F32),
        name="rel_bias_table",
    )(rel_bias)


def _inproj_kernel(x_ref, mod_ref, g1_ref, win_ref, mu_ref, w0_ref, wlu_ref, a0_ref, alu_ref,
                   glu_ref, kk_ref, ka_ref, qg_ref, kg_ref, seg_ref,
                   r_out, lw_out, k_out, v_out, kn_out, b_out, g_out, q_out, ak_out, av_out,
                   carry_ref):
    @pl.when(pl.program_id(1) == 0)
    def _():
        carry_ref[...] = jnp.zeros_like(carry_ref)

    x = x_ref[0]
    tm = x.shape[0]
    ms = jnp.mean(x * x, axis=-1, keepdims=True)
    h = x * lax.rsqrt(ms + NORM_EPS) * g1_ref[...] * (1.0 + mod_ref[0, 1:2, :]) + mod_ref[0, 0:1, :]
    p = _dot(h, win_ref[...])

    pr = p[:, :RWKV_COLS]
    row = lax.broadcasted_iota(jnp.int32, pr.shape, 0)
    prev = jnp.where(row == 0, carry_ref[...], pltpu.roll(pr, 1, 0))
    carry_ref[...] = pr[tm - 1:tm, :]
    pm = pr + (prev - pr) * mu_ref[...]

    r = pm[:, 0:RWKV_W]
    k = pm[:, RWKV_W:2 * RWKV_W]
    v = pm[:, 2 * RWKV_W:3 * RWKV_W]
    o = 3 * RWKV_W
    xw = pm[:, o:o + DECAY_LORA]
    xa = pm[:, o + DECAY_LORA:o + DECAY_LORA + AAA_LORA]
    xg = pm[:, o + DECAY_LORA + AAA_LORA:RWKV_COLS]

    seg = seg_ref[...]
    z = w0_ref[...] + _dot(jnp.tanh(xw), wlu_ref[...])
    lw_out[0] = -math.exp(-0.5) * _sigmoid(z)
    a = _sigmoid(a0_ref[...] + _dot(xa, alu_ref[...]))
    g_out[0] = _dot(_sigmoid(xg), glu_ref[...])
    kk = k * kk_ref[...]
    nrm = jnp.sqrt(_dot_rhs_exact(kk * kk, seg))
    kk = kk / jnp.maximum(nrm, 1e-12)
    kn_out[0] = kk
    b_out[0] = kk * a
    k_out[0] = k * (1.0 + (a - 1.0) * ka_ref[...])
    r_out[0] = r
    v_out[0] = v

    q = p[:, RWKV_COLS:RWKV_COLS + ATT_Q_W]
    ak = p[:, RWKV_COLS + ATT_Q_W:RWKV_COLS + ATT_Q_W + ATT_KV_W]
    av = p[:, RWKV_COLS + ATT_Q_W + ATT_KV_W:]
    qms = _dot_rhs_exact(q * q, seg) * (1.0 / HEAD_DIM)
    q_out[0] = (q * lax.rsqrt(qms + NORM_EPS) * (qg_ref[...] * ATT_SCALE)).astype(BF16)
    kms = _dot_rhs_exact(ak * ak, seg[:ATT_KV_W, :ATT_KV_W]) * (1.0 / HEAD_DIM)
    ak_out[0] = (ak * lax.rsqrt(kms + NORM_EPS) * kg_ref[...]).astype(BF16)
    av_out[0] = av.astype(BF16)


def _in_projection(x, mod3, g1, w_in, mu, w0, wlu, a0, alu, glu, k_k, k_a, qg, kg, seg):
    B, S, D = x.shape
    tm = min(IN_TILE, S)
    const = lambda shape: pl.BlockSpec(shape, lambda b, j: (0,) * len(shape))
    row_spec = lambda w: pl.BlockSpec((1, tm, w), lambda b, j: (b, j, 0))
    f32_out = jax.ShapeDtypeStruct((B, S, RWKV_W), F32)
    return pl.pallas_call(
        _inproj_kernel,
        grid=(B, S // tm),
        in_specs=[row_spec(D),
                  pl.BlockSpec((1, N_MOD, D), lambda b, j: (b, 0, 0)),
                  const(g1.shape), const(w_in.shape), const(mu.shape), const(w0.shape),
                  const(wlu.shape), const(a0.shape), const(alu.shape), const(glu.shape),
                  const(k_k.shape), const(k_a.shape), const(qg.shape), const(kg.shape),
                  const(seg.shape)],
        out_specs=[row_spec(RWKV_W)] * 7 + [row_spec(ATT_Q_W), row_spec(ATT_KV_W), row_spec(ATT_KV_W)],
        out_shape=[f32_out] * 7 + [jax.ShapeDtypeStruct((B, S, ATT_Q_W), BF16),
                                   jax.ShapeDtypeStruct((B, S, ATT_KV_W), BF16),
                                   jax.ShapeDtypeStruct((B, S, ATT_KV_W), BF16)],
        scratch_shapes=[pltpu.VMEM((1, RWKV_COLS), F32)],
        compiler_params=pltpu.CompilerParams(dimension_semantics=("arbitrary", "arbitrary"),
                                             vmem_limit_bytes=VMEM_LIMIT),
        name="in_projection",
    )(x, mod3, g1, w_in, mu, w0, wlu, a0, alu, glu, k_k, k_a, qg, kg, seg)


def _rwkv_kernel(r_ref, lw_ref, k_ref, v_ref, kn_ref, b_ref, g_ref, rk_ref, lg_ref, lb_ref,
                 seg_ref, tri_ref, o_ref, state_ref):
    @pl.when(pl.program_id(1) == 0)
    def _():
        state_ref[...] = jnp.zeros_like(state_ref)

    L, D, P = CHUNK, HEAD_DIM, 2 * HEAD_DIM
    n_chunks = r_ref.shape[1] // L
    n_pairs = RWKV_W // P
    row = lax.broadcasted_iota(jnp.int32, (L, P), 0)
    lane = lax.broadcasted_iota(jnp.int32, (L, P), 1)
    low = lane < D
    s_idx = jnp.where(low, lane, lane - D)
    strict = row > s_idx
    incl = row >= s_idx
    zeros_lp = jnp.zeros((L, P), F32)
    own = (lambda t: jnp.where(low, t, 0.0), lambda t: jnp.where(low, 0.0, t))
    other = (own[1], own[0])
    diag = (lane == row, lane == row + D)
    tri = tri_ref[...]
    seg = seg_ref[...]

    items = [(c, j, par) for c in range(n_chunks) for j in range(n_pairs) for par in range(2)]
    chunk_in = {}
    for c in range(n_chunks):
        rows = pl.ds(c * L, L)
        r = r_ref[0, rows, :]
        lw = lw_ref[0, rows, :]
        k = k_ref[0, rows, :]
        v = v_ref[0, rows, :]
        kn = kn_ref[0, rows, :]
        bv = b_ref[0, rows, :]
        cum = _dot_lhs_exact(tri, lw)
        cl = cum[L - 1:L, :]
        e_neg = jnp.exp(-cum)
        e_hat = jnp.exp(cl - cum)
        chunk_in[c] = dict(r=r, k=k, v=v, a_t=-kn * jnp.exp(cum - lw), b_t=bv * e_neg, k_t=k * e_neg,
                           r_t=r * jnp.exp(cum), b_h=bv * e_hat, k_h=k * e_hat, w_l=jnp.exp(cl))

    top, bot, v_o, x = {}, {}, {}, {}
    for c in range(n_chunks):
        ci = chunk_in[c]
        for j in range(n_pairs):
            ps = slice(j * P, (j + 1) * P)
            atp, rtp = ci["a_t"][:, ps], ci["r_t"][:, ps]
            lhs = jnp.concatenate([own[0](atp), own[0](rtp), own[1](atp), own[1](rtp)], axis=0)
            s1 = _dot_nt(lhs, jnp.concatenate([ci["b_t"][:, ps], ci["k_t"][:, ps]], axis=0))
            v_swap = pltpu.roll(ci["v"][:, ps], D, 1)
            for par in range(2):
                it = (c, j, par)
                top[it] = jnp.where(strict, s1[2 * L * par:2 * L * par + L], 0.0)
                bot[it] = jnp.where(incl, s1[2 * L * par + L:2 * L * (par + 1)], 0.0)
                v_o[it] = other[par](v_swap)
                x[it] = own[par](atp)

    for it in items:
        x[it] = x[it] + _dot(top[it], jnp.concatenate([zeros_lp, v_o[it]], axis=0))

    a_pow = {it: top[it][:, :D].astype(BF16) for it in items}
    n = 1
    while n < L:
        last = 2 * n >= L
        for it in items:
            xb = x[it].astype(BF16)
            if last:
                x[it] = x[it] + jnp.dot(a_pow[it], xb, preferred_element_type=F32)
            else:
                res = jnp.dot(a_pow[it], jnp.concatenate([xb, a_pow[it]], axis=1),
                              preferred_element_type=F32)
                x[it] = x[it] + res[:, :P]
                a_pow[it] = res[:, P:].astype(BF16)
        n *= 2

    rhs2 = {it: jnp.concatenate([x[it], v_o[it]], axis=0).astype(BF16) for it in items}
    rb = {it: jnp.dot(bot[it].astype(BF16), rhs2[it], preferred_element_type=F32) for it in items}
    mn = {}
    for c in range(n_chunks):
        ci = chunk_in[c]
        for j in range(n_pairs):
            ps = slice(j * P, (j + 1) * P)
            res = _dot_tn(jnp.concatenate([ci["b_h"][:, ps], ci["k_h"][:, ps]], axis=0),
                          jnp.concatenate([rhs2[(c, j, 0)], rhs2[(c, j, 1)]], axis=1))
            for par in range(2):
                mn[(c, j, par)] = res[D * par:D * (par + 1), P * par:P * (par + 1)]

    y = {}
    for c in range(n_chunks):
        ci = chunk_in[c]
        for j in range(n_pairs):
            ps = slice(j * P, (j + 1) * P)
            for par in range(2):
                it = (c, j, par)
                h = 2 * j + par
                g_p = rb[it] + ci["r_t"][:, ps]
                m_p = mn[it] + jnp.where(diag[par], ci["w_l"][:, ps], 0.0)
                z = state_ref[h]
                rhs = jnp.concatenate([z, zeros_lp] if par == 0 else [zeros_lp, z], axis=0)
                res = _dot(jnp.concatenate([g_p, m_p], axis=0), rhs)
                y[it] = res[:L] + rb[it]
                state_ref[h] = other[par](res[L:] + mn[it])

    for c in range(n_chunks):
        ci = chunk_in[c]
        rows = pl.ds(c * L, L)
        yc = jnp.concatenate(
            [pltpu.roll(jnp.where(low, y[(c, j, 1)], y[(c, j, 0)]), D, 1) for j in range(n_pairs)], axis=1)
        mu = _dot_rhs_exact(yc, seg) * (1.0 / HEAD_DIM)
        yc = yc - mu
        var = _dot_rhs_exact(yc * yc, seg) * (1.0 / HEAD_DIM)
        yn = yc * lax.rsqrt(var + RWKV_GN_EPS) * lg_ref[...] + lb_ref[...]
        bonus = _dot_rhs_exact(ci["r"] * ci["k"] * rk_ref[...], seg) * ci["v"]
        o_ref[0, rows, :] = ((yn + bonus) * g_ref[0, rows, :]).astype(BF16)


def _rwkv(r, lw, k, v, kn, bv, g, r_k, lnx_g, lnx_b, seg, tri):
    B, S, W = r.shape
    tc = min(RWKV_TILE, S)
    const = lambda shape: pl.BlockSpec(shape, lambda b, j: (0,) * len(shape))
    row_spec = pl.BlockSpec((1, tc, W), lambda b, j: (b, j, 0))
    return pl.pallas_call(
        _rwkv_kernel,
        grid=(B, S // tc),
        in_specs=[row_spec] * 7 + [const(r_k.shape), const(lnx_g.shape), const(lnx_b.shape),
                                   const(seg.shape), const(tri.shape)],
        out_specs=row_spec,
        out_shape=jax.ShapeDtypeStruct((B, S, W), BF16),
        scratch_shapes=[pltpu.VMEM((RWKV_HEADS, CHUNK, 2 * HEAD_DIM), F32)],
        compiler_params=pltpu.CompilerParams(dimension_semantics=("arbitrary", "arbitrary"),
                                             vmem_limit_bytes=VMEM_LIMIT),
        name="rwkv7_chunked",
    )(r, lw, k, v, kn, bv, g, r_k, lnx_g, lnx_b, seg, tri)


def _swa_kernel(q_ref, kc_ref, kp_ref, vc_ref, vp_ref, bias_ref, sink_ref, o_ref):
    first = pl.program_id(1) == 0
    col = lax.broadcasted_iota(jnp.int32, (ATT_BLOCK, 2 * ATT_BLOCK), 1)
    no_prev = jnp.logical_and(first, col < ATT_BLOCK)
    q = q_ref[0]
    for hk in range(N_KV_HEADS):
        ks = slice(hk * HEAD_DIM, (hk + 1) * HEAD_DIM)
        kw = jnp.concatenate([kp_ref[0, :, ks], kc_ref[0, :, ks]], axis=0)
        vw = jnp.concatenate([vp_ref[0, :, ks], vc_ref[0, :, ks]], axis=0)
        for gq in range(GQA_GROUP):
            h = hk * GQA_GROUP + gq
            qs = slice(h * HEAD_DIM, (h + 1) * HEAD_DIM)
            logits = lax.dot_general(q[:, qs], kw, (((1,), (1,)), ((), ())),
                                     preferred_element_type=F32)
            logits = jnp.where(no_prev, NEG_INF, logits + bias_ref[h])
            sink = sink_ref[h]
            m = jnp.maximum(jnp.max(logits, axis=-1, keepdims=True), sink)
            e = jnp.exp(logits - m)
            den = jnp.sum(e, axis=-1, keepdims=True) + jnp.exp(sink - m)
            out = jnp.dot(e.astype(BF16), vw, preferred_element_type=F32) / den
            o_ref[0, :, qs] = out.astype(BF16)


def _swa(q, ak, av, bias, sinks):
    B, S, _ = q.shape
    nb = S // ATT_BLOCK
    cur = lambda w: pl.BlockSpec((1, ATT_BLOCK, w), lambda b, i: (b, i, 0))
    prev = lambda w: pl.BlockSpec((1, ATT_BLOCK, w), lambda b, i: (b, jnp.maximum(i - 1, 0), 0))
    return pl.pallas_call(
        _swa_kernel,
        grid=(B, nb),
        in_specs=[cur(ATT_Q_W), cur(ATT_KV_W), prev(ATT_KV_W), cur(ATT_KV_W), prev(ATT_KV_W),
                  pl.BlockSpec(bias.shape, lambda b, i: (0, 0, 0)),
                  pl.BlockSpec(memory_space=pltpu.SMEM)],
        out_specs=cur(ATT_Q_W),
        out_shape=jax.ShapeDtypeStruct((B, S, ATT_Q_W), BF16),
        compiler_params=pltpu.CompilerParams(dimension_semantics=("arbitrary", "arbitrary")),
        name="swa_attention",
    )(q, ak, ak, av, av, bias, sinks)


def _ffn_kernel(x_ref, yr_ref, ya_ref, mod_ref, g2_ref, wo_ref, wg_ref, wu_ref, wd_ref, o_ref):
    x = x_ref[0]
    wo = wo_ref[...]
    mix = (jnp.dot(yr_ref[0], wo[:RWKV_W], preferred_element_type=F32)
           + jnp.dot(ya_ref[0], wo[RWKV_W:], preferred_element_type=F32))
    h_res = x + mod_ref[0, 2:3, :] * mix
    ms = jnp.mean(h_res * h_res, axis=-1, keepdims=True)
    h2 = (h_res * lax.rsqrt(ms + NORM_EPS) * g2_ref[...] * (1.0 + mod_ref[0, 4:5, :])
          + mod_ref[0, 3:4, :]).astype(BF16)
    d_ff = wg_ref.shape[1]
    blk = d_ff // FFN_SPLIT
    ffn = jnp.zeros_like(x)
    for s in range(FFN_SPLIT):
        cs = slice(s * blk, (s + 1) * blk)
        gt = jnp.dot(h2, wg_ref[:, cs], preferred_element_type=F32)
        up = jnp.dot(h2, wu_ref[:, cs], preferred_element_type=F32)
        act = (gt * _sigmoid(gt) * up).astype(BF16)
        ffn = ffn + jnp.dot(act, wd_ref[cs, :], preferred_element_type=F32)
    o_ref[0] = h_res + mod_ref[0, 5:6, :] * ffn


def _out_ffn(x, y_rwkv, y_att, mod3, g2, w_out, w_gate, w_up, w_down):
    B, S, D = x.shape
    tm = min(FFN_TILE, S)
    resident = lambda shape: pl.BlockSpec(shape, lambda b, j: (0,) * len(shape),
                                          pipeline_mode=pl.Buffered(1))
    row_spec = lambda w: pl.BlockSpec((1, tm, w), lambda b, j: (b, j, 0))
    return pl.pallas_call(
        _ffn_kernel,
        grid=(B, S // tm),
        in_specs=[row_spec(D), row_spec(RWKV_W), row_spec(ATT_Q_W),
                  pl.BlockSpec((1, N_MOD, D), lambda b, j: (b, 0, 0)),
                  resident(g2.shape), resident(w_out.shape), resident(w_gate.shape),
                  resident(w_up.shape), resident(w_down.shape)],
        out_specs=row_spec(D),
        out_shape=jax.ShapeDtypeStruct((B, S, D), F32),
        compiler_params=pltpu.CompilerParams(dimension_semantics=("arbitrary", "arbitrary"),
                                             vmem_limit_bytes=VMEM_LIMIT),
        name="out_proj_ffn",
    )(x, y_rwkv, y_att, mod3, g2, w_out, w_gate, w_up, w_down)


def kernel(x, c, w_ada, b_ada, norm1_g, w_in, rwkv_mu, w0, w_lora_up, a0, a_lora_up, g_lora_up, k_k, k_a, r_k, lnx_g, lnx_b, q_norm_g, k_norm_g, sinks, rel_bias, w_out, norm2_g, w_gate, w_up, w_down):
    B, S, D = x.shape
    row = lambda t: t.reshape(1, -1).astype(F32)
    lane = jnp.arange(RWKV_W) // HEAD_DIM
    seg = (lane[:, None] == lane[None, :]).astype(BF16)
    t_idx = jnp.arange(CHUNK)
    tri = (t_idx[:, None] >= t_idx[None, :]).astype(BF16)

    mod3 = _modulation(c.astype(F32), w_ada, b_ada).reshape(B, N_MOD, D)
    bias = _bias_table(rel_bias)

    r, lw, k, v, kn, bv, g, q, ak, av = _in_projection(
        x, mod3, row(norm1_g), w_in.astype(BF16), row(rwkv_mu), row(w0), w_lora_up.astype(BF16),
        row(a0), a_lora_up.astype(BF16), g_lora_up.astype(BF16), row(k_k), row(k_a),
        row(jnp.tile(q_norm_g, N_Q_HEADS)), row(jnp.tile(k_norm_g, N_KV_HEADS)), seg)

    y_rwkv = _rwkv(r, lw, k, v, kn, bv, g, row(r_k), row(lnx_g), row(lnx_b), seg, tri)
    y_att = _swa(q, ak, av, bias, sinks)

    out = _out_ffn(x, y_rwkv, y_att, mod3, row(norm2_g), w_out.astype(BF16),
                   w_gate.astype(BF16), w_up.astype(BF16), w_down.astype(BF16))
    return out.astype(x.dtype)
```

```python
import functools
import math

import jax
import jax.numpy as jnp
from jax import lax
from jax.experimental import pallas as pl
from jax.experimental.pallas import tpu as pltpu

F32 = jnp.float32
BF16 = jnp.bfloat16

HEAD_DIM = 64
RWKV_HEADS = 8
RWKV_W = RWKV_HEADS * HEAD_DIM
DECAY_LORA = 64
AAA_LORA = 64
GATE_LORA = 128
RWKV_COLS = 3 * RWKV_W + DECAY_LORA + AAA_LORA + GATE_LORA
RWKV_GN_EPS = 64e-5
N_Q_HEADS = 8
N_KV_HEADS = 2
GQA_GROUP = N_Q_HEADS // N_KV_HEADS
ATT_Q_W = N_Q_HEADS * HEAD_DIM
ATT_KV_W = N_KV_HEADS * HEAD_DIM
ATT_BLOCK = 128
WINDOW = 128
ATT_SCALE = 1.0 / math.sqrt(HEAD_DIM)
NUM_BUCKETS = 32
MAX_DISTANCE = 128
N_MOD = 6
NORM_EPS = 1e-6
NEG_INF = -1e30

CHUNK = 64
IN_TILE = 256
RWKV_TILE = 128
FFN_TILE = 512
MXU_TILE = 256
FFN_BLOCK = 4 * MXU_TILE
VMEM_LIMIT = 56 * 1024 * 1024


def _dot(a, b):
    return jnp.dot(a.astype(BF16), b.astype(BF16), preferred_element_type=F32)


def _dot_nt(a, b):
    return lax.dot_general(a.astype(BF16), b.astype(BF16), (((1,), (1,)), ((), ())),
                           preferred_element_type=F32)


def _dot_tn(a, b):
    return lax.dot_general(a.astype(BF16), b.astype(BF16), (((0,), (0,)), ((), ())),
                           preferred_element_type=F32)


def _split(x):
    hi = x.astype(BF16)
    lo = (x - hi.astype(F32)).astype(BF16)
    return hi, lo


def _seg_sum(x, seg):
    m, w = x.shape
    xb = x.astype(BF16)
    blk = seg.shape[0]
    if w <= blk:
        return jnp.dot(xb, seg[:w, :w], preferred_element_type=F32)
    n = w // blk
    stacked = jnp.concatenate([xb[:, i * blk:(i + 1) * blk] for i in range(n)], axis=0)
    res = jnp.dot(stacked, seg, preferred_element_type=F32)
    return jnp.concatenate([res[i * m:(i + 1) * m] for i in range(n)], axis=1)


def _dot_lhs_exact(m, x):
    hi, lo = _split(x)
    return (jnp.dot(m, hi, preferred_element_type=F32) + jnp.dot(m, lo, preferred_element_type=F32))


def _sigmoid(z):
    return 1.0 / (1.0 + jnp.exp(-z))


def _mod_kernel(c_ref, w_ref, b_ref, o_ref):
    c = c_ref[...]
    s = c * _sigmoid(c)
    s_hi, s_lo = _split(s)
    w = w_ref[...]
    w_hi, w_lo = _split(w)
    d = lambda x, y: jnp.dot(x, y, preferred_element_type=F32)
    o_ref[...] = d(s_hi, w_hi) + d(s_hi, w_lo) + d(s_lo, w_hi) + b_ref[...]


def _modulation(c, w_ada, b_ada):
    B, D = c.shape
    n = w_ada.shape[1]
    blk = D
    return pl.pallas_call(
        _mod_kernel,
        grid=(n // blk,),
        in_specs=[pl.BlockSpec((B, D), lambda j: (0, 0)),
                  pl.BlockSpec((D, blk), lambda j: (0, j)),
                  pl.BlockSpec((1, blk), lambda j: (0, j))],
        out_specs=pl.BlockSpec((B, blk), lambda j: (0, j)),
        out_shape=jax.ShapeDtypeStruct((B, n), F32),
        name="adaln_mod",
    )(c, w_ada, b_ada.reshape(1, n))


def _bias_kernel(rb_ref, o_ref):
    qi = lax.broadcasted_iota(jnp.int32, (ATT_BLOCK, 2 * ATT_BLOCK), 0)
    kj = lax.broadcasted_iota(jnp.int32, (ATT_BLOCK, 2 * ATT_BLOCK), 1)
    dist = qi + ATT_BLOCK - kj
    n = jnp.maximum(dist, 0)
    max_exact = NUM_BUCKETS // 2
    n_f = jnp.maximum(n, 1).astype(F32)
    large = max_exact + (jnp.log(n_f / max_exact) / math.log(MAX_DISTANCE / max_exact)
                         * (NUM_BUCKETS - max_exact)).astype(jnp.int32)
    large = jnp.minimum(large, NUM_BUCKETS - 1)
    bucket = jnp.where(n < max_exact, n, large)
    valid = (dist >= 0) & (dist < WINDOW)
    for h in range(N_Q_HEADS):
        acc = jnp.zeros((ATT_BLOCK, 2 * ATT_BLOCK), F32)
        for b in range(NUM_BUCKETS):
            acc = jnp.where(bucket == b, rb_ref[b, h], acc)
        o_ref[h] = jnp.where(valid, acc, NEG_INF)


def _bias_table(rel_bias):
    return pl.pallas_call(
        _bias_kernel,
        in_specs=[pl.BlockSpec(memory_space=pltpu.SMEM)],
        out_specs=pl.BlockSpec(memory_space=pltpu.VMEM),
        out_shape=jax.ShapeDtypeStruct((N_Q_HEADS, ATT_BLOCK, 2 * ATT_BLOCK), F32),
        name="rel_bias_table",
    )(rel_bias)


def _inproj_kernel(x_ref, mod_ref, g1_ref, win_ref, mu_ref, w0_ref, wlu_ref, a0_ref, alu_ref,
                   glu_ref, kk_ref, ka_ref, qg_ref, kg_ref, seg_ref,
                   r_out, lw_out, k_out, v_out, kn_out, b_out, g_out, q_out, ak_out, av_out,
                   carry_ref):
    @pl.when(pl.program_id(1) == 0)
    def _():
        carry_ref[...] = jnp.zeros_like(carry_ref)

    x = x_ref[0]
    tm = x.shape[0]
    ms = jnp.mean(x * x, axis=-1, keepdims=True)
    h = x * lax.rsqrt(ms + NORM_EPS) * g1_ref[...] * (1.0 + mod_ref[0, 1:2, :]) + mod_ref[0, 0:1, :]
    p = _dot(h, win_ref[...])

    pr = p[:, :RWKV_COLS]
    row = lax.broadcasted_iota(jnp.int32, pr.shape, 0)
    prev = jnp.where(row == 0, carry_ref[...], pltpu.roll(pr, 1, 0))
    carry_ref[...] = pr[tm - 1:tm, :]
    pm = pr + (prev - pr) * mu_ref[...]

    r = pm[:, 0:RWKV_W]
    k = pm[:, RWKV_W:2 * RWKV_W]
    v = pm[:, 2 * RWKV_W:3 * RWKV_W]
    o = 3 * RWKV_W
    xw = pm[:, o:o + DECAY_LORA]
    xa = pm[:, o + DECAY_LORA:o + DECAY_LORA + AAA_LORA]
    xg = pm[:, o + DECAY_LORA + AAA_LORA:RWKV_COLS]

    seg = seg_ref[...]
    z = w0_ref[...] + _dot(jnp.tanh(xw), wlu_ref[...])
    lw_out[0] = -math.exp(-0.5) * _sigmoid(z)
    a = _sigmoid(a0_ref[...] + _dot(xa, alu_ref[...]))
    g_out[0] = _dot(_sigmoid(xg), glu_ref[...])
    kk = k * kk_ref[...]
    nrm = jnp.sqrt(_seg_sum(kk * kk, seg))
    kk = kk / jnp.maximum(nrm, 1e-12)
    kn_out[0] = kk
    b_out[0] = kk * a
    k_out[0] = k * (1.0 + (a - 1.0) * ka_ref[...])
    r_out[0] = r
    v_out[0] = v

    q = p[:, RWKV_COLS:RWKV_COLS + ATT_Q_W]
    ak = p[:, RWKV_COLS + ATT_Q_W:RWKV_COLS + ATT_Q_W + ATT_KV_W]
    av = p[:, RWKV_COLS + ATT_Q_W + ATT_KV_W:]
    qms = _seg_sum(q * q, seg) * (1.0 / HEAD_DIM)
    q_out[0] = (q * lax.rsqrt(qms + NORM_EPS) * (qg_ref[...] * ATT_SCALE)).astype(BF16)
    kms = _seg_sum(ak * ak, seg) * (1.0 / HEAD_DIM)
    ak_out[0] = (ak * lax.rsqrt(kms + NORM_EPS) * kg_ref[...]).astype(BF16)
    av_out[0] = av.astype(BF16)


def _in_projection(x, mod3, g1, w_in, mu, w0, wlu, a0, alu, glu, k_k, k_a, qg, kg, seg):
    B, S, D = x.shape
    tm = min(IN_TILE, S)
    const = lambda shape: pl.BlockSpec(shape, lambda b, j: (0,) * len(shape))
    row_spec = lambda w: pl.BlockSpec((1, tm, w), lambda b, j: (b, j, 0))
    f32_out = jax.ShapeDtypeStruct((B, S, RWKV_W), F32)
    return pl.pallas_call(
        _inproj_kernel,
        grid=(B, S // tm),
        in_specs=[row_spec(D),
                  pl.BlockSpec((1, N_MOD, D), lambda b, j: (b, 0, 0)),
                  const(g1.shape), const(w_in.shape), const(mu.shape), const(w0.shape),
                  const(wlu.shape), const(a0.shape), const(alu.shape), const(glu.shape),
                  const(k_k.shape), const(k_a.shape), const(qg.shape), const(kg.shape),
                  const(seg.shape)],
        out_specs=[row_spec(RWKV_W)] * 7 + [row_spec(ATT_Q_W), row_spec(ATT_KV_W), row_spec(ATT_KV_W)],
        out_shape=[f32_out] * 7 + [jax.ShapeDtypeStruct((B, S, ATT_Q_W), BF16),
                                   jax.ShapeDtypeStruct((B, S, ATT_KV_W), BF16),
                                   jax.ShapeDtypeStruct((B, S, ATT_KV_W), BF16)],
        scratch_shapes=[pltpu.VMEM((1, RWKV_COLS), F32)],
        compiler_params=pltpu.CompilerParams(dimension_semantics=("arbitrary", "arbitrary"),
                                             vmem_limit_bytes=VMEM_LIMIT),
        name="in_projection",
    )(x, mod3, g1, w_in, mu, w0, wlu, a0, alu, glu, k_k, k_a, qg, kg, seg)


def _rwkv_kernel(r_ref, lw_ref, k_ref, v_ref, kn_ref, b_ref, g_ref, rk_ref, lg_ref, lb_ref,
                 seg_ref, tri_ref, o_ref, state_ref):
    @pl.when(pl.program_id(1) == 0)
    def _():
        state_ref[...] = jnp.zeros_like(state_ref)

    L, D, P = CHUNK, HEAD_DIM, 2 * HEAD_DIM
    n_chunks = r_ref.shape[1] // L
    n_pairs = RWKV_W // P
    row = lax.broadcasted_iota(jnp.int32, (L, P), 0)
    lane = lax.broadcasted_iota(jnp.int32, (L, P), 1)
    low = lane < D
    s_idx = jnp.where(low, lane, lane - D)
    strict = row > s_idx
    incl = row >= s_idx
    zeros_lp = jnp.zeros((L, P), F32)
    own = (lambda t: jnp.where(low, t, 0.0), lambda t: jnp.where(low, 0.0, t))
    other = (own[1], own[0])
    diag = (lane == row, lane == row + D)
    tri = tri_ref[...]
    seg = seg_ref[...]

    items = [(c, j, par) for c in range(n_chunks) for j in range(n_pairs) for par in range(2)]
    chunk_in = {}
    for c in range(n_chunks):
        rows = pl.ds(c * L, L)
        r = r_ref[0, rows, :]
        lw = lw_ref[0, rows, :]
        k = k_ref[0, rows, :]
        v = v_ref[0, rows, :]
        kn = kn_ref[0, rows, :]
        bv = b_ref[0, rows, :]
        cum = _dot_lhs_exact(tri, lw)
        cl = cum[L - 1:L, :]
        e_neg = jnp.exp(-cum)
        e_hat = jnp.exp(cl - cum)
        chunk_in[c] = dict(r=r, k=k, v=v, a_t=-kn * jnp.exp(cum - lw), b_t=bv * e_neg, k_t=k * e_neg,
                           r_t=r * jnp.exp(cum), b_h=bv * e_hat, k_h=k * e_hat, w_l=jnp.exp(cl))

    top, bot, v_o, x = {}, {}, {}, {}
    for c in range(n_chunks):
        ci = chunk_in[c]
        for j in range(n_pairs):
            ps = slice(j * P, (j + 1) * P)
            atp, rtp = ci["a_t"][:, ps], ci["r_t"][:, ps]
            lhs = jnp.concatenate([own[0](atp), own[0](rtp), own[1](atp), own[1](rtp)], axis=0)
            s1 = _dot_nt(lhs, jnp.concatenate([ci["b_t"][:, ps], ci["k_t"][:, ps]], axis=0))
            v_swap = pltpu.roll(ci["v"][:, ps], D, 1)
            for par in range(2):
                it = (c, j, par)
                top[it] = jnp.where(strict, s1[2 * L * par:2 * L * par + L], 0.0)
                bot[it] = jnp.where(incl, s1[2 * L * par + L:2 * L * (par + 1)], 0.0)
                v_o[it] = other[par](v_swap)
                x[it] = own[par](atp)

    for it in items:
        x[it] = x[it] + _dot(top[it], jnp.concatenate([zeros_lp, v_o[it]], axis=0))

    a_pow = {it: top[it][:, :D].astype(BF16) for it in items}
    n = 1
    while n < L:
        last = 2 * n >= L
        for it in items:
            xb = x[it].astype(BF16)
            if last:
                x[it] = x[it] + jnp.dot(a_pow[it], xb, preferred_element_type=F32)
            else:
                res = jnp.dot(a_pow[it], jnp.concatenate([xb, a_pow[it]], axis=1),
                              preferred_element_type=F32)
                x[it] = x[it] + res[:, :P]
                a_pow[it] = res[:, P:].astype(BF16)
        n *= 2

    rhs2 = {it: jnp.concatenate([x[it], v_o[it]], axis=0).astype(BF16) for it in items}
    rb = {it: jnp.dot(bot[it].astype(BF16), rhs2[it], preferred_element_type=F32) for it in items}
    mn = {}
    for c in range(n_chunks):
        ci = chunk_in[c]
        for j in range(n_pairs):
            ps = slice(j * P, (j + 1) * P)
            res = _dot_tn(jnp.concatenate([ci["b_h"][:, ps], ci["k_h"][:, ps]], axis=0),
                          jnp.concatenate([rhs2[(c, j, 0)], rhs2[(c, j, 1)]], axis=1))
            for par in range(2):
                mn[(c, j, par)] = res[D * par:D * (par + 1), P * par:P * (par + 1)]

    y = {}
    for c in range(n_chunks):
        ci = chunk_in[c]
        for j in range(n_pairs):
            ps = slice(j * P, (j + 1) * P)
            for par in range(2):
                it = (c, j, par)
                h = 2 * j + par
                g_p = rb[it] + ci["r_t"][:, ps]
                m_p = mn[it] + jnp.where(diag[par], ci["w_l"][:, ps], 0.0)
                z = state_ref[h]
                rhs = jnp.concatenate([z, zeros_lp] if par == 0 else [zeros_lp, z], axis=0)
                res = _dot(jnp.concatenate([g_p, m_p], axis=0), rhs)
                y[it] = res[:L] + rb[it]
                state_ref[h] = other[par](res[L:] + mn[it])

    for c in range(n_chunks):
        ci = chunk_in[c]
        rows = pl.ds(c * L, L)
        yc = jnp.concatenate(
            [pltpu.roll(jnp.where(low, y[(c, j, 1)], y[(c, j, 0)]), D, 1) for j in range(n_pairs)], axis=1)
        mu = _seg_sum(yc, seg) * (1.0 / HEAD_DIM)
        yc = yc - mu
        var = _seg_sum(yc * yc, seg) * (1.0 / HEAD_DIM)
        yn = yc * lax.rsqrt(var + RWKV_GN_EPS) * lg_ref[...] + lb_ref[...]
        bonus = _seg_sum(ci["r"] * ci["k"] * rk_ref[...], seg) * ci["v"]
        o_ref[0, rows, :] = ((yn + bonus) * g_ref[0, rows, :]).astype(BF16)


def _rwkv(r, lw, k, v, kn, bv, g, r_k, lnx_g, lnx_b, seg, tri):
    B, S, W = r.shape
    tc = min(RWKV_TILE, S)
    const = lambda shape: pl.BlockSpec(shape, lambda b, j: (0,) * len(shape))
    row_spec = pl.BlockSpec((1, tc, W), lambda b, j: (b, j, 0))
    return pl.pallas_call(
        _rwkv_kernel,
        grid=(B, S // tc),
        in_specs=[row_spec] * 7 + [const(r_k.shape), const(lnx_g.shape), const(lnx_b.shape),
                                   const(seg.shape), const(tri.shape)],
        out_specs=row_spec,
        out_shape=jax.ShapeDtypeStruct((B, S, W), BF16),
        scratch_shapes=[pltpu.VMEM((RWKV_HEADS, CHUNK, 2 * HEAD_DIM), F32)],
        compiler_params=pltpu.CompilerParams(dimension_semantics=("arbitrary", "arbitrary"),
                                             vmem_limit_bytes=VMEM_LIMIT),
        name="rwkv7_chunked",
    )(r, lw, k, v, kn, bv, g, r_k, lnx_g, lnx_b, seg, tri)


def _swa_kernel(q_ref, kc_ref, kp_ref, vc_ref, vp_ref, bias_ref, sink_ref, o_ref):
    first = pl.program_id(1) == 0
    rows = GQA_GROUP * ATT_BLOCK
    col = lax.broadcasted_iota(jnp.int32, (rows, 2 * ATT_BLOCK), 1)
    no_prev = jnp.logical_and(first, col < ATT_BLOCK)
    row = lax.broadcasted_iota(jnp.int32, (rows, 1), 0)
    q = q_ref[0]
    heads = range(N_KV_HEADS)
    logits, vw, sink = [], [], []
    for hk in heads:
        ks = slice(hk * HEAD_DIM, (hk + 1) * HEAD_DIM)
        kw = jnp.concatenate([kp_ref[0, :, ks], kc_ref[0, :, ks]], axis=0)
        vw.append(jnp.concatenate([vp_ref[0, :, ks], vc_ref[0, :, ks]], axis=0))
        qg = jnp.concatenate([q[:, (hk * GQA_GROUP + g) * HEAD_DIM:(hk * GQA_GROUP + g + 1) * HEAD_DIM]
                              for g in range(GQA_GROUP)], axis=0)
        lg = lax.dot_general(qg, kw, (((1,), (1,)), ((), ())), preferred_element_type=F32)
        logits.append(jnp.where(no_prev, NEG_INF, lg + bias_ref[hk]))
        s = jnp.full((rows, 1), sink_ref[hk * GQA_GROUP], F32)
        for g in range(1, GQA_GROUP):
            s = jnp.where(row >= g * ATT_BLOCK, sink_ref[hk * GQA_GROUP + g], s)
        sink.append(s)
    m = [jnp.maximum(jnp.max(logits[hk], axis=-1, keepdims=True), sink[hk]) for hk in heads]
    e = [jnp.exp(logits[hk] - m[hk]) for hk in heads]
    den = [jnp.sum(e[hk], axis=-1, keepdims=True) + jnp.exp(sink[hk] - m[hk]) for hk in heads]
    for hk in heads:
        out = jnp.dot(e[hk].astype(BF16), vw[hk], preferred_element_type=F32) / den[hk]
        for g in range(GQA_GROUP):
            h = hk * GQA_GROUP + g
            o_ref[0, :, h * HEAD_DIM:(h + 1) * HEAD_DIM] = out[g * ATT_BLOCK:(g + 1) * ATT_BLOCK].astype(BF16)


def _swa(q, ak, av, bias, sinks):
    B, S, _ = q.shape
    nb = S // ATT_BLOCK
    bias = bias.reshape(N_KV_HEADS, GQA_GROUP * ATT_BLOCK, 2 * ATT_BLOCK)
    cur = lambda w: pl.BlockSpec((1, ATT_BLOCK, w), lambda b, i: (b, i, 0))
    prev = lambda w: pl.BlockSpec((1, ATT_BLOCK, w), lambda b, i: (b, jnp.maximum(i - 1, 0), 0))
    return pl.pallas_call(
        _swa_kernel,
        grid=(B, nb),
        in_specs=[cur(ATT_Q_W), cur(ATT_KV_W), prev(ATT_KV_W), cur(ATT_KV_W), prev(ATT_KV_W),
                  pl.BlockSpec(bias.shape, lambda b, i: (0, 0, 0)),
                  pl.BlockSpec(memory_space=pltpu.SMEM)],
        out_specs=cur(ATT_Q_W),
        out_shape=jax.ShapeDtypeStruct((B, S, ATT_Q_W), BF16),
        compiler_params=pltpu.CompilerParams(dimension_semantics=("arbitrary", "arbitrary")),
        name="swa_attention",
    )(q, ak, ak, av, av, bias, sinks)


def _ffn_kernel(x_ref, yr_ref, ya_ref, mod_ref, g2_ref, wo_ref, wg_ref, wu_ref, wd_ref, o_ref):
    x = x_ref[0]
    wo = wo_ref[...]
    mix = (jnp.dot(yr_ref[0], wo[:RWKV_W], preferred_element_type=F32)
           + jnp.dot(ya_ref[0], wo[RWKV_W:], preferred_element_type=F32))
    h_res = x + mod_ref[0, 2:3, :] * mix
    ms = jnp.mean(h_res * h_res, axis=-1, keepdims=True)
    h2 = (h_res * lax.rsqrt(ms + NORM_EPS) * g2_ref[...] * (1.0 + mod_ref[0, 4:5, :])
          + mod_ref[0, 3:4, :]).astype(BF16)
    d_ff = wg_ref.shape[1]
    ffn = jnp.zeros_like(x)
    for lo in range(0, d_ff, FFN_BLOCK):
        cs = slice(lo, min(lo + FFN_BLOCK, d_ff))
        gt = jnp.dot(h2, wg_ref[:, cs], preferred_element_type=F32)
        up = jnp.dot(h2, wu_ref[:, cs], preferred_element_type=F32)
        act = (gt * _sigmoid(gt) * up).astype(BF16)
        ffn = ffn + jnp.dot(act, wd_ref[cs, :], preferred_element_type=F32)
    o_ref[0] = h_res + mod_ref[0, 5:6, :] * ffn


def _out_ffn(x, y_rwkv, y_att, mod3, g2, w_out, w_gate, w_up, w_down):
    B, S, D = x.shape
    tm = min(FFN_TILE, S)
    resident = lambda shape: pl.BlockSpec(shape, lambda b, j: (0,) * len(shape),
                                          pipeline_mode=pl.Buffered(1))
    row_spec = lambda w: pl.BlockSpec((1, tm, w), lambda b, j: (b, j, 0))
    return pl.pallas_call(
        _ffn_kernel,
        grid=(B, S // tm),
        in_specs=[row_spec(D), row_spec(RWKV_W), row_spec(ATT_Q_W),
                  pl.BlockSpec((1, N_MOD, D), lambda b, j: (b, 0, 0)),
                  resident(g2.shape), resident(w_out.shape), resident(w_gate.shape),
                  resident(w_up.shape), resident(w_down.shape)],
        out_specs=row_spec(D),
        out_shape=jax.ShapeDtypeStruct((B, S, D), F32),
        compiler_params=pltpu.CompilerParams(dimension_semantics=("arbitrary", "arbitrary"),
                                             vmem_limit_bytes=VMEM_LIMIT),
        name="out_proj_ffn",
    )(x, y_rwkv, y_att, mod3, g2, w_out, w_gate, w_up, w_down)


def kernel(x, c, w_ada, b_ada, norm1_g, w_in, rwkv_mu, w0, w_lora_up, a0, a_lora_up, g_lora_up, k_k, k_a, r_k, lnx_g, lnx_b, q_norm_g, k_norm_g, sinks, rel_bias, w_out, norm2_g, w_gate, w_up, w_down):
    B, S, D = x.shape
    row = lambda t: t.reshape(1, -1).astype(F32)
    lane = jnp.arange(MXU_TILE) // HEAD_DIM
    seg = (lane[:, None] == lane[None, :]).astype(BF16)
    t_idx = jnp.arange(CHUNK)
    tri = (t_idx[:, None] >= t_idx[None, :]).astype(BF16)

    mod3 = _modulation(c.astype(F32), w_ada, b_ada).reshape(B, N_MOD, D)
    bias = _bias_table(rel_bias)

    r, lw, k, v, kn, bv, g, q, ak, av = _in_projection(
        x, mod3, row(norm1_g), w_in.astype(BF16), row(rwkv_mu), row(w0), w_lora_up.astype(BF16),
        row(a0), a_lora_up.astype(BF16), g_lora_up.astype(BF16), row(k_k), row(k_a),
        row(jnp.tile(q_norm_g, N_Q_HEADS)), row(jnp.tile(k_norm_g, N_KV_HEADS)), seg)

    y_rwkv = _rwkv(r, lw, k, v, kn, bv, g, row(r_k), row(lnx_g), row(lnx_b), seg, tri)
    y_att = _swa(q, ak, av, bias, sinks)

    out = _out_ffn(x, y_rwkv, y_att, mod3, row(norm2_g), w_out.astype(BF16),
                   w_gate.astype(BF16), w_up.astype(BF16), w_down.astype(BF16))
    return out.astype(x.dtype)
```

```python
import functools
import math

import jax
import jax.numpy as jnp
from jax import lax
from jax.experimental import pallas as pl
from jax.experimental.pallas import tpu as pltpu

F32 = jnp.float32
BF16 = jnp.bfloat16

HEAD_DIM = 64
RWKV_HEADS = 8
RWKV_W = RWKV_HEADS * HEAD_DIM
DECAY_LORA = 64
AAA_LORA = 64
GATE_LORA = 128
RWKV_COLS = 3 * RWKV_W + DECAY_LORA + AAA_LORA + GATE_LORA
RWKV_GN_EPS = 64e-5
N_Q_HEADS = 8
N_KV_HEADS = 2
GQA_GROUP = N_Q_HEADS // N_KV_HEADS
ATT_Q_W = N_Q_HEADS * HEAD_DIM
ATT_KV_W = N_KV_HEADS * HEAD_DIM
ATT_BLOCK = 128
WINDOW = 128
ATT_SCALE = 1.0 / math.sqrt(HEAD_DIM)
NUM_BUCKETS = 32
MAX_DISTANCE = 128
N_MOD = 6
NORM_EPS = 1e-6
NEG_INF = -1e30

CHUNK = 64
IN_TILE = 512
IN_SUB = 256
RWKV_TILE = 256
FFN_TILE = 512
MXU_TILE = 256
FFN_BLOCK = 4 * MXU_TILE
VMEM_LIMIT = 56 * 1024 * 1024


def _dot(a, b):
    return jnp.dot(a.astype(BF16), b.astype(BF16), preferred_element_type=F32)


def _dot_nt(a, b):
    return lax.dot_general(a.astype(BF16), b.astype(BF16), (((1,), (1,)), ((), ())),
                           preferred_element_type=F32)


def _dot_tn(a, b):
    return lax.dot_general(a.astype(BF16), b.astype(BF16), (((0,), (0,)), ((), ())),
                           preferred_element_type=F32)


def _split(x):
    hi = x.astype(BF16)
    lo = (x - hi.astype(F32)).astype(BF16)
    return hi, lo


def _seg_sum(x, seg):
    m, w = x.shape
    xb = x.astype(BF16)
    blk = seg.shape[0]
    if w <= blk:
        return jnp.dot(xb, seg[:w, :w], preferred_element_type=F32)
    n = w // blk
    stacked = jnp.concatenate([xb[:, i * blk:(i + 1) * blk] for i in range(n)], axis=0)
    res = jnp.dot(stacked, seg, preferred_element_type=F32)
    return jnp.concatenate([res[i * m:(i + 1) * m] for i in range(n)], axis=1)


def _dot_lhs_exact(m, x):
    hi, lo = _split(x)
    return (jnp.dot(m, hi, preferred_element_type=F32) + jnp.dot(m, lo, preferred_element_type=F32))


def _sigmoid(z):
    return 1.0 / (1.0 + jnp.exp(-z))


def _mod_kernel(c_ref, w_ref, b_ref, o_ref):
    c = c_ref[...]
    s = c * _sigmoid(c)
    s_hi, s_lo = _split(s)
    w = w_ref[...]
    w_hi, w_lo = _split(w)
    d = lambda x, y: jnp.dot(x, y, preferred_element_type=F32)
    o_ref[...] = d(s_hi, w_hi) + d(s_hi, w_lo) + d(s_lo, w_hi) + b_ref[...]


def _modulation(c, w_ada, b_ada):
    B, D = c.shape
    n = w_ada.shape[1]
    blk = D
    return pl.pallas_call(
        _mod_kernel,
        grid=(n // blk,),
        in_specs=[pl.BlockSpec((B, D), lambda j: (0, 0)),
                  pl.BlockSpec((D, blk), lambda j: (0, j)),
                  pl.BlockSpec((1, blk), lambda j: (0, j))],
        out_specs=pl.BlockSpec((B, blk), lambda j: (0, j)),
        out_shape=jax.ShapeDtypeStruct((B, n), F32),
        name="adaln_mod",
    )(c, w_ada, b_ada.reshape(1, n))


def _bias_kernel(rb_ref, o_ref):
    qi = lax.broadcasted_iota(jnp.int32, (ATT_BLOCK, 2 * ATT_BLOCK), 0)
    kj = lax.broadcasted_iota(jnp.int32, (ATT_BLOCK, 2 * ATT_BLOCK), 1)
    dist = qi + ATT_BLOCK - kj
    n = jnp.maximum(dist, 0)
    max_exact = NUM_BUCKETS // 2
    n_f = jnp.maximum(n, 1).astype(F32)
    large = max_exact + (jnp.log(n_f / max_exact) / math.log(MAX_DISTANCE / max_exact)
                         * (NUM_BUCKETS - max_exact)).astype(jnp.int32)
    large = jnp.minimum(large, NUM_BUCKETS - 1)
    bucket = jnp.where(n < max_exact, n, large)
    valid = (dist >= 0) & (dist < WINDOW)
    for h in range(N_Q_HEADS):
        acc = jnp.zeros((ATT_BLOCK, 2 * ATT_BLOCK), F32)
        for b in range(NUM_BUCKETS):
            acc = jnp.where(bucket == b, rb_ref[b, h], acc)
        o_ref[h] = jnp.where(valid, acc, NEG_INF)


def _bias_table(rel_bias):
    return pl.pallas_call(
        _bias_kernel,
        in_specs=[pl.BlockSpec(memory_space=pltpu.SMEM)],
        out_specs=pl.BlockSpec(memory_space=pltpu.VMEM),
        out_shape=jax.ShapeDtypeStruct((N_Q_HEADS, ATT_BLOCK, 2 * ATT_BLOCK), F32),
        name="rel_bias_table",
    )(rel_bias)


def _inproj_kernel(x_ref, mod_ref, g1_ref, win_ref, mu_ref, w0_ref, wlu_ref, a0_ref, alu_ref,
                   glu_ref, kk_ref, ka_ref, rk_ref, qg_ref, kg_ref, seg_ref, tri_ref,
                   at_out, bt_out, kt_out, rt_out, bh_out, kh_out, v_out, wl_out, bonus_out, g_out,
                   q_out, ak_out, av_out, carry_ref):
    @pl.when(pl.program_id(1) == 0)
    def _():
        carry_ref[...] = jnp.zeros_like(carry_ref)

    tm = tri_ref.shape[0]
    refs = (mu_ref, w0_ref, wlu_ref, a0_ref, alu_ref, glu_ref, kk_ref, ka_ref, rk_ref, qg_ref, kg_ref,
            seg_ref, tri_ref)
    outs = (at_out, bt_out, kt_out, rt_out, bh_out, kh_out, v_out, wl_out, bonus_out, g_out, q_out,
            ak_out, av_out)
    tail = iter(())
    for s in range(x_ref.shape[1] // tm):
        rs = pl.ds(s * tm, tm)
        x = x_ref[0, rs, :]
        ms = jnp.mean(x * x, axis=-1, keepdims=True)
        h = (x * lax.rsqrt(ms + NORM_EPS) * g1_ref[...] * (1.0 + mod_ref[0, 1:2, :])
             + mod_ref[0, 0:1, :]).astype(BF16)
        pieces = []
        for lo in range(0, win_ref.shape[1], MXU_TILE):
            pieces.append(jnp.dot(h, win_ref[:, lo:lo + MXU_TILE], preferred_element_type=F32))
            next(tail, None)
        for _ in tail:
            pass
        tail = _inproj_tail(jnp.concatenate(pieces, axis=1), rs, s * (tm // CHUNK), refs, outs, carry_ref)
    for _ in tail:
        pass


def _inproj_tail(p, rs, c0, refs, outs, carry_ref):
    (mu_ref, w0_ref, wlu_ref, a0_ref, alu_ref, glu_ref, kk_ref, ka_ref, rk_ref, qg_ref, kg_ref,
     seg_ref, tri_ref) = refs
    (at_out, bt_out, kt_out, rt_out, bh_out, kh_out, v_out, wl_out, bonus_out, g_out, q_out,
     ak_out, av_out) = outs
    tm = p.shape[0]

    pr = p[:, :RWKV_COLS]
    row = lax.broadcasted_iota(jnp.int32, pr.shape, 0)
    prev = jnp.where(row == 0, carry_ref[...], pltpu.roll(pr, 1, 0))
    carry_ref[...] = pr[tm - 1:tm, :]
    pm = pr + (prev - pr) * mu_ref[...]
    yield

    r = pm[:, 0:RWKV_W]
    k = pm[:, RWKV_W:2 * RWKV_W]
    v = pm[:, 2 * RWKV_W:3 * RWKV_W]
    o = 3 * RWKV_W
    xw = pm[:, o:o + DECAY_LORA]
    xa = pm[:, o + DECAY_LORA:o + DECAY_LORA + AAA_LORA]
    xg = pm[:, o + DECAY_LORA + AAA_LORA:RWKV_COLS]

    seg = seg_ref[...]
    z = w0_ref[...] + _dot(jnp.tanh(xw), wlu_ref[...])
    lw = -math.exp(-0.5) * _sigmoid(z)
    yield
    a = _sigmoid(a0_ref[...] + _dot(xa, alu_ref[...]))
    g_out[0, rs, :] = _dot(_sigmoid(xg), glu_ref[...])
    yield
    kk = k * kk_ref[...]
    nrm = jnp.sqrt(_seg_sum(kk * kk, seg))
    kk = kk / jnp.maximum(nrm, 1e-12)
    bv = kk * a
    k = k * (1.0 + (a - 1.0) * ka_ref[...])
    yield
    v_out[0, rs, :] = v.astype(BF16)
    bonus_out[0, rs, :] = _seg_sum(r * k * rk_ref[...], seg) * v
    yield

    cum = _dot_lhs_exact(tri_ref[...], lw)
    n_chunks = tm // CHUNK
    ends = [cum[(c + 1) * CHUNK - 1:(c + 1) * CHUNK, :] for c in range(n_chunks)]
    cl = jnp.concatenate([jnp.broadcast_to(e, (CHUNK, RWKV_W)) for e in ends], axis=0)
    for c in range(n_chunks):
        wl_out[0, c0 + c] = jnp.exp(ends[c])
    yield
    e_neg = jnp.exp(-cum)
    bt_out[0, rs, :] = (bv * e_neg).astype(BF16)
    kt_out[0, rs, :] = (k * e_neg).astype(BF16)
    yield
    e_hat = jnp.exp(cl - cum)
    bh_out[0, rs, :] = (bv * e_hat).astype(BF16)
    kh_out[0, rs, :] = (k * e_hat).astype(BF16)
    yield
    at_out[0, rs, :] = (-kk * jnp.exp(cum - lw)).astype(BF16)
    rt_out[0, rs, :] = (r * jnp.exp(cum)).astype(BF16)
    yield

    q = p[:, RWKV_COLS:RWKV_COLS + ATT_Q_W]
    ak = p[:, RWKV_COLS + ATT_Q_W:RWKV_COLS + ATT_Q_W + ATT_KV_W]
    av = p[:, RWKV_COLS + ATT_Q_W + ATT_KV_W:]
    qms = _seg_sum(q * q, seg) * (1.0 / HEAD_DIM)
    q_out[0, rs, :] = (q * lax.rsqrt(qms + NORM_EPS) * (qg_ref[...] * ATT_SCALE)).astype(BF16)
    yield
    kms =_seg_sum(ak * ak, seg) * (1.0 / HEAD_DIM)
    ak_out[0, rs, :] = (ak * lax.rsqrt(kms + NORM_EPS) * kg_ref[...]).astype(BF16)
    av_out[0, rs, :] = av.astype(BF16)


def _in_projection(x, mod3, g1, w_in, mu, w0, wlu, a0, alu, glu, k_k, k_a, r_k, qg, kg, seg, tri):
    B, S, D = x.shape
    tm = min(IN_TILE, S)
    const = lambda shape: pl.BlockSpec(shape, lambda b, j: (0,) * len(shape))
    row_spec = lambda w: pl.BlockSpec((1, tm, w), lambda b, j: (b, j, 0))
    rows = lambda w, dt: jax.ShapeDtypeStruct((B, S, w), dt)
    return pl.pallas_call(
        _inproj_kernel,
        grid=(B, S // tm),
        in_specs=[row_spec(D),
                  pl.BlockSpec((1, N_MOD, D), lambda b, j: (b, 0, 0)),
                  const(g1.shape), const(w_in.shape), const(mu.shape), const(w0.shape),
                  const(wlu.shape), const(a0.shape), const(alu.shape), const(glu.shape),
                  const(k_k.shape), const(k_a.shape), const(r_k.shape), const(qg.shape),
                  const(kg.shape), const(seg.shape), const(tri.shape)],
        out_specs=([row_spec(RWKV_W)] * 7
                   + [pl.BlockSpec((1, tm // CHUNK, 1, RWKV_W), lambda b, j: (b, j, 0, 0))]
                   + [row_spec(RWKV_W)] * 2
                   + [row_spec(ATT_Q_W), row_spec(ATT_KV_W), row_spec(ATT_KV_W)]),
        out_shape=([rows(RWKV_W, BF16)] * 7
                   + [jax.ShapeDtypeStruct((B, S // CHUNK, 1, RWKV_W), F32)]
                   + [rows(RWKV_W, F32)] * 2
                   + [rows(ATT_Q_W, BF16), rows(ATT_KV_W, BF16), rows(ATT_KV_W, BF16)]),
        scratch_shapes=[pltpu.VMEM((1, RWKV_COLS), F32)],
        compiler_params=pltpu.CompilerParams(dimension_semantics=("arbitrary", "arbitrary"),
                                             vmem_limit_bytes=VMEM_LIMIT),
        name="in_projection",
    )(x, mod3, g1, w_in, mu, w0, wlu, a0, alu, glu, k_k, k_a, r_k, qg, kg, seg, tri)


def _rwkv_kernel(at_ref, bt_ref, kt_ref, rt_ref, bh_ref, kh_ref, v_ref, wl_ref, bonus_ref, g_ref,
                 lg_ref, lb_ref, seg_ref, o_ref, state_ref):
    @pl.when(pl.program_id(1) == 0)
    def _():
        state_ref[...] = jnp.zeros_like(state_ref)

    L, D, P = CHUNK, HEAD_DIM, 2 * HEAD_DIM
    n_chunks = at_ref.shape[1] // L
    n_pairs = RWKV_W // P
    row = lax.broadcasted_iota(jnp.int32, (L, P), 0)
    lane = lax.broadcasted_iota(jnp.int32, (L, P), 1)
    low = lane < D
    s_idx = jnp.where(low, lane, lane - D)
    strict = row > s_idx
    incl = row >= s_idx
    zeros_lp = jnp.zeros((L, P), F32)
    own = (lambda t: jnp.where(low, t, jnp.zeros_like(t)), lambda t: jnp.where(low, jnp.zeros_like(t), t))
    other = (own[1], own[0])
    diag = (lane == row, lane == row + D)
    seg = seg_ref[...]

    items = [(c, j, par) for c in range(n_chunks) for j in range(n_pairs) for par in range(2)]
    chunk_in = {}
    for c in range(n_chunks):
        rows = pl.ds(c * L, L)
        chunk_in[c] = dict(a_t=at_ref[0, rows, :], b_t=bt_ref[0, rows, :], k_t=kt_ref[0, rows, :],
                           r_t=rt_ref[0, rows, :], b_h=bh_ref[0, rows, :], k_h=kh_ref[0, rows, :],
                           v=v_ref[0, rows, :], w_l=wl_ref[0, c])

    top, bot, v_o, x = {}, {}, {}, {}
    for c in range(n_chunks):
        ci = chunk_in[c]
        for j in range(n_pairs):
            ps = slice(j * P, (j + 1) * P)
            atp, rtp = ci["a_t"][:, ps], ci["r_t"][:, ps]
            lhs = jnp.concatenate([own[0](atp), own[0](rtp), own[1](atp), own[1](rtp)], axis=0)
            s1 = _dot_nt(lhs, jnp.concatenate([ci["b_t"][:, ps], ci["k_t"][:, ps]], axis=0))
            v_swap = pltpu.roll(ci["v"][:, ps].astype(F32), D, 1)
            for par in range(2):
                it = (c, j, par)
                top[it] = jnp.where(strict, s1[2 * L * par:2 * L * par + L], 0.0)
                bot[it] = jnp.where(incl, s1[2 * L * par + L:2 * L * (par + 1)], 0.0)
                v_o[it] = other[par](v_swap)
                x[it] = own[par](atp).astype(F32)

    for it in items:
        x[it] = x[it] + _dot(top[it], jnp.concatenate([zeros_lp, v_o[it]], axis=0))

    a_pow = {it: top[it][:, :D].astype(BF16) for it in items}
    n = 1
    while n < L:
        last = 2 * n >= L
        for it in items:
            xb = x[it].astype(BF16)
            if last:
                x[it] = x[it] + jnp.dot(a_pow[it], xb, preferred_element_type=F32)
            else:
                res = jnp.dot(a_pow[it], jnp.concatenate([xb, a_pow[it]], axis=1),
                              preferred_element_type=F32)
                x[it] = x[it] + res[:, :P]
                a_pow[it] = res[:, P:].astype(BF16)
        n *= 2

    rhs2 = {it: jnp.concatenate([x[it], v_o[it]], axis=0).astype(BF16) for it in items}
    rb = {it: jnp.dot(bot[it].astype(BF16), rhs2[it], preferred_element_type=F32) for it in items}
    mn = {}
    for c in range(n_chunks):
        ci = chunk_in[c]
        for j in range(n_pairs):
            ps = slice(j * P, (j + 1) * P)
            res = _dot_tn(jnp.concatenate([ci["b_h"][:, ps], ci["k_h"][:, ps]], axis=0),
                          jnp.concatenate([rhs2[(c, j, 0)], rhs2[(c, j, 1)]], axis=1))
            for par in range(2):
                mn[(c, j, par)] = res[D * par:D * (par + 1), P * par:P * (par + 1)]

    y = {}
    for c in range(n_chunks):
        ci = chunk_in[c]
        for j in range(n_pairs):
            ps = slice(j * P, (j + 1) * P)
            for par in range(2):
                it = (c, j, par)
                h = 2 * j + par
                g_p = rb[it] + ci["r_t"][:, ps]
                m_p = mn[it] + jnp.where(diag[par], ci["w_l"][:, ps], 0.0)
                z = state_ref[h]
                rhs = jnp.concatenate([z, zeros_lp] if par == 0 else [zeros_lp, z], axis=0)
                res = _dot(jnp.concatenate([g_p, m_p], axis=0), rhs)
                y[it] = res[:L] + rb[it]
                state_ref[h] = other[par](res[L:] + mn[it])

    for c in range(n_chunks):
        rows = pl.ds(c * L, L)
        yc = jnp.concatenate(
            [pltpu.roll(jnp.where(low, y[(c, j, 1)], y[(c, j, 0)]), D, 1) for j in range(n_pairs)], axis=1)
        mu = _seg_sum(yc, seg) * (1.0 / HEAD_DIM)
        yc = yc - mu
        var = _seg_sum(yc * yc, seg) * (1.0 / HEAD_DIM)
        yn = yc * lax.rsqrt(var + RWKV_GN_EPS) * lg_ref[...] + lb_ref[...]
        o_ref[0, rows, :] = ((yn + bonus_ref[0, rows, :]) * g_ref[0, rows, :]).astype(BF16)


def _rwkv(at, bt, kt, rt, bh, kh, v, wl, bonus, g, lnx_g, lnx_b, seg):
    B, S, W = at.shape
    tc = min(RWKV_TILE, S)
    const = lambda shape: pl.BlockSpec(shape, lambda b, j: (0,) * len(shape))
    row_spec = pl.BlockSpec((1, tc, W), lambda b, j: (b, j, 0))
    return pl.pallas_call(
        _rwkv_kernel,
        grid=(B, S // tc),
        in_specs=([row_spec] * 7 + [pl.BlockSpec((1, tc // CHUNK, 1, W), lambda b, j: (b, j, 0, 0))]
                  + [row_spec] * 2 + [const(lnx_g.shape), const(lnx_b.shape), const(seg.shape)]),
        out_specs=row_spec,
        out_shape=jax.ShapeDtypeStruct((B, S, W), BF16),
        scratch_shapes=[pltpu.VMEM((RWKV_HEADS, CHUNK, 2 * HEAD_DIM), F32)],
        compiler_params=pltpu.CompilerParams(dimension_semantics=("arbitrary", "arbitrary"),
                                             vmem_limit_bytes=VMEM_LIMIT),
        name="rwkv7_chunked",
    )(at, bt, kt, rt, bh, kh, v, wl, bonus, g, lnx_g, lnx_b, seg)


def _swa_kernel(q_ref, kc_ref, kp_ref, vc_ref, vp_ref, bias_ref, sink_ref, o_ref):
    first = pl.program_id(1) == 0
    rows = GQA_GROUP * ATT_BLOCK
    col = lax.broadcasted_iota(jnp.int32, (rows, 2 * ATT_BLOCK), 1)
    no_prev = jnp.logical_and(first, col < ATT_BLOCK)
    row = lax.broadcasted_iota(jnp.int32, (rows, 1), 0)
    q = q_ref[0]
    heads = range(N_KV_HEADS)
    logits, vw, sink = [], [], []
    for hk in heads:
        ks = slice(hk * HEAD_DIM, (hk + 1) * HEAD_DIM)
        kw = jnp.concatenate([kp_ref[0, :, ks], kc_ref[0, :, ks]], axis=0)
        vw.append(jnp.concatenate([vp_ref[0, :, ks], vc_ref[0, :, ks]], axis=0))
        qg = jnp.concatenate([q[:, (hk * GQA_GROUP + g) * HEAD_DIM:(hk * GQA_GROUP + g + 1) * HEAD_DIM]
                              for g in range(GQA_GROUP)], axis=0)
        lg = lax.dot_general(qg, kw, (((1,), (1,)), ((), ())), preferred_element_type=F32)
        logits.append(jnp.where(no_prev, NEG_INF, lg + bias_ref[hk]))
        s = jnp.full((rows, 1), sink_ref[hk * GQA_GROUP], F32)
        for g in range(1, GQA_GROUP):
            s = jnp.where(row >= g * ATT_BLOCK, sink_ref[hk * GQA_GROUP + g], s)
        sink.append(s)
    m = [jnp.maximum(jnp.max(logits[hk], axis=-1, keepdims=True), sink[hk]) for hk in heads]
    e = [jnp.exp(logits[hk] - m[hk]) for hk in heads]
    den = [jnp.sum(e[hk], axis=-1, keepdims=True) + jnp.exp(sink[hk] - m[hk]) for hk in heads]
    for hk in heads:
        out = jnp.dot(e[hk].astype(BF16), vw[hk], preferred_element_type=F32) / den[hk]
        for g in range(GQA_GROUP):
            h = hk * GQA_GROUP + g
            o_ref[0, :, h * HEAD_DIM:(h + 1) * HEAD_DIM] = out[g * ATT_BLOCK:(g + 1) * ATT_BLOCK].astype(BF16)


def _swa(q, ak, av, bias, sinks):
    B, S, _ = q.shape
    nb = S // ATT_BLOCK
    bias = bias.reshape(N_KV_HEADS, GQA_GROUP * ATT_BLOCK, 2 * ATT_BLOCK)
    cur = lambda w: pl.BlockSpec((1, ATT_BLOCK, w), lambda b, i: (b, i, 0))
    prev = lambda w: pl.BlockSpec((1, ATT_BLOCK, w), lambda b, i: (b, jnp.maximum(i - 1, 0), 0))
    return pl.pallas_call(
        _swa_kernel,
        grid=(B, nb),
        in_specs=[cur(ATT_Q_W), cur(ATT_KV_W), prev(ATT_KV_W), cur(ATT_KV_W), prev(ATT_KV_W),
                  pl.BlockSpec(bias.shape, lambda b, i: (0, 0, 0)),
                  pl.BlockSpec(memory_space=pltpu.SMEM)],
        out_specs=cur(ATT_Q_W),
        out_shape=jax.ShapeDtypeStruct((B, S, ATT_Q_W), BF16),
        compiler_params=pltpu.CompilerParams(dimension_semantics=("arbitrary", "arbitrary")),
        name="swa_attention",
    )(q, ak, ak, av, av, bias, sinks)


def _ffn_kernel(x_ref, yr_ref, ya_ref, mod_ref, g2_ref, wo_ref, wg_ref, wu_ref, wd_ref, o_ref):
    x = x_ref[0]
    wo = wo_ref[...]
    mix = (jnp.dot(yr_ref[0], wo[:RWKV_W], preferred_element_type=F32)
           + jnp.dot(ya_ref[0], wo[RWKV_W:], preferred_element_type=F32))
    h_res = x + mod_ref[0, 2:3, :] * mix
    ms = jnp.mean(h_res * h_res, axis=-1, keepdims=True)
    h2 = (h_res * lax.rsqrt(ms + NORM_EPS) * g2_ref[...] * (1.0 + mod_ref[0, 4:5, :])
          + mod_ref[0, 3:4, :]).astype(BF16)
    d_ff = wg_ref.shape[1]
    ffn = jnp.zeros_like(x)
    for lo in range(0, d_ff, FFN_BLOCK):
        cs = slice(lo, min(lo + FFN_BLOCK, d_ff))
        gt = jnp.dot(h2, wg_ref[:, cs], preferred_element_type=F32)
        up = jnp.dot(h2, wu_ref[:, cs], preferred_element_type=F32)
        act = (gt * _sigmoid(gt) * up).astype(BF16)
        ffn = ffn + jnp.dot(act, wd_ref[cs, :], preferred_element_type=F32)
    o_ref[0] = h_res + mod_ref[0, 5:6, :] * ffn


def _out_ffn(x, y_rwkv, y_att, mod3, g2, w_out, w_gate, w_up, w_down):
    B, S, D = x.shape
    tm = min(FFN_TILE, S)
    resident = lambda shape: pl.BlockSpec(shape, lambda b, j: (0,) * len(shape),
                                          pipeline_mode=pl.Buffered(1))
    row_spec = lambda w: pl.BlockSpec((1, tm, w), lambda b, j: (b, j, 0))
    return pl.pallas_call(
        _ffn_kernel,
        grid=(B, S // tm),
        in_specs=[row_spec(D), row_spec(RWKV_W), row_spec(ATT_Q_W),
                  pl.BlockSpec((1, N_MOD, D), lambda b, j: (b, 0, 0)),
                  resident(g2.shape), resident(w_out.shape), resident(w_gate.shape),
                  resident(w_up.shape), resident(w_down.shape)],
        out_specs=row_spec(D),
        out_shape=jax.ShapeDtypeStruct((B, S, D), F32),
        compiler_params=pltpu.CompilerParams(dimension_semantics=("arbitrary", "arbitrary"),
                                             vmem_limit_bytes=VMEM_LIMIT),
        name="out_proj_ffn",
    )(x, y_rwkv, y_att, mod3, g2, w_out, w_gate, w_up, w_down)


def kernel(x, c, w_ada, b_ada, norm1_g, w_in, rwkv_mu, w0, w_lora_up, a0, a_lora_up, g_lora_up, k_k, k_a, r_k, lnx_g, lnx_b, q_norm_g, k_norm_g, sinks, rel_bias, w_out, norm2_g, w_gate, w_up, w_down):
    B, S, D = x.shape
    row = lambda t: t.reshape(1, -1).astype(F32)
    lane = jnp.arange(MXU_TILE) // HEAD_DIM
    seg = (lane[:, None] == lane[None, :]).astype(BF16)
    t_idx = jnp.arange(min(IN_SUB, S))
    tri = ((t_idx[:, None] >= t_idx[None, :])
           & (t_idx[:, None] // CHUNK == t_idx[None, :] // CHUNK)).astype(BF16)

    mod3 = _modulation(c.astype(F32), w_ada, b_ada).reshape(B, N_MOD, D)
    bias = _bias_table(rel_bias)

    at, bt, kt, rt, bh, kh, v, wl, bonus, g, q, ak, av = _in_projection(
        x, mod3, row(norm1_g), w_in.astype(BF16), row(rwkv_mu), row(w0), w_lora_up.astype(BF16),
        row(a0), a_lora_up.astype(BF16), g_lora_up.astype(BF16), row(k_k), row(k_a), row(r_k),
        row(jnp.tile(q_norm_g, N_Q_HEADS)), row(jnp.tile(k_norm_g, N_KV_HEADS)), seg, tri)

    y_rwkv = _rwkv(at, bt, kt, rt, bh, kh, v, wl, bonus, g, row(lnx_g), row(lnx_b), seg)
    y_att = _swa(q, ak, av, bias, sinks)

    out = _out_ffn(x, y_rwkv, y_att, mod3, row(norm2_g), w_out.astype(BF16),
                   w_gate.astype(BF16), w_up.astype(BF16), w_down.astype(BF16))
    return out.astype(x.dtype)
```

```python
import functools
import math

import jax
import jax.numpy as jnp
from jax import lax
from jax.experimental import pallas as pl
from jax.experimental.pallas import tpu as pltpu

F32 = jnp.float32
BF16 = jnp.bfloat16

HEAD_DIM = 64
RWKV_HEADS = 8
RWKV_W = RWKV_HEADS * HEAD_DIM
DECAY_LORA = 64
AAA_LORA = 64
GATE_LORA = 128
RWKV_COLS = 3 * RWKV_W + DECAY_LORA + AAA_LORA + GATE_LORA
RWKV_GN_EPS = 64e-5
N_Q_HEADS = 8
N_KV_HEADS = 2
GQA_GROUP = N_Q_HEADS // N_KV_HEADS
ATT_Q_W = N_Q_HEADS * HEAD_DIM
ATT_KV_W = N_KV_HEADS * HEAD_DIM
ATT_BLOCK = 128
WINDOW = 128
ATT_SCALE = 1.0 / math.sqrt(HEAD_DIM)
NUM_BUCKETS = 32
MAX_DISTANCE = 128
N_MOD = 6
NORM_EPS = 1e-6
NEG_INF = -1e30

CHUNK = 64
IN_TILE = 512
IN_SUB = 256
RWKV_TILE = 512
SWA_TILE = 512
FFN_TILE = 512
MXU_TILE = 256
FFN_BLOCK = 4 * MXU_TILE
VMEM_LIMIT = 56 * 1024 * 1024


def _dot(a, b):
    return jnp.dot(a.astype(BF16), b.astype(BF16), preferred_element_type=F32)


def _dot_nt(a, b):
    return lax.dot_general(a.astype(BF16), b.astype(BF16), (((1,), (1,)), ((), ())),
                           preferred_element_type=F32)


def _dot_tn(a, b):
    return lax.dot_general(a.astype(BF16), b.astype(BF16), (((0,), (0,)), ((), ())),
                           preferred_element_type=F32)


def _split(x):
    hi = x.astype(BF16)
    lo = (x - hi.astype(F32)).astype(BF16)
    return hi, lo


def _seg_sum(x, seg):
    m, w = x.shape
    xb = x.astype(BF16)
    blk = seg.shape[0]
    if w <= blk:
        return jnp.dot(xb, seg[:w, :w], preferred_element_type=F32)
    n = w // blk
    stacked = jnp.concatenate([xb[:, i * blk:(i + 1) * blk] for i in range(n)], axis=0)
    res = jnp.dot(stacked, seg, preferred_element_type=F32)
    return jnp.concatenate([res[i * m:(i + 1) * m] for i in range(n)], axis=1)


def _dot_lhs_exact(m, x):
    hi, lo = _split(x)
    return (jnp.dot(m, hi, preferred_element_type=F32) + jnp.dot(m, lo, preferred_element_type=F32))


def _sigmoid(z):
    return 1.0 / (1.0 + jnp.exp(-z))


def _mod_kernel(c_ref, w_ref, b_ref, o_ref):
    c = c_ref[...]
    s = c * _sigmoid(c)
    s_hi, s_lo = _split(s)
    w = w_ref[...]
    w_hi, w_lo = _split(w)
    d = lambda x, y: jnp.dot(x, y, preferred_element_type=F32)
    o_ref[...] = d(s_hi, w_hi) + d(s_hi, w_lo) + d(s_lo, w_hi) + b_ref[...]


def _modulation(c, w_ada, b_ada):
    B, D = c.shape
    n = w_ada.shape[1]
    blk = D
    return pl.pallas_call(
        _mod_kernel,
        grid=(n // blk,),
        in_specs=[pl.BlockSpec((B, D), lambda j: (0, 0)),
                  pl.BlockSpec((D, blk), lambda j: (0, j)),
                  pl.BlockSpec((1, blk), lambda j: (0, j))],
        out_specs=pl.BlockSpec((B, blk), lambda j: (0, j)),
        out_shape=jax.ShapeDtypeStruct((B, n), F32),
        name="adaln_mod",
    )(c, w_ada, b_ada.reshape(1, n))


def _bias_kernel(rb_ref, o_ref):
    qi = lax.broadcasted_iota(jnp.int32, (ATT_BLOCK, ATT_BLOCK), 0)
    kj = lax.broadcasted_iota(jnp.int32, (ATT_BLOCK, ATT_BLOCK), 1)
    n = jnp.where(kj <= qi, qi - kj, qi + ATT_BLOCK - kj)
    max_exact = NUM_BUCKETS // 2
    n_f = jnp.maximum(n, 1).astype(F32)
    large = max_exact + (jnp.log(n_f / max_exact) / math.log(MAX_DISTANCE / max_exact)
                         * (NUM_BUCKETS - max_exact)).astype(jnp.int32)
    large = jnp.minimum(large, NUM_BUCKETS - 1)
    bucket = jnp.where(n < max_exact, n, large)
    for h in range(N_Q_HEADS):
        acc = jnp.zeros((ATT_BLOCK, ATT_BLOCK), F32)
        for b in range(NUM_BUCKETS):
            acc = jnp.where(bucket == b, rb_ref[b, h], acc)
        o_ref[h] = acc


def _bias_table(rel_bias):
    assert WINDOW == ATT_BLOCK
    return pl.pallas_call(
        _bias_kernel,
        in_specs=[pl.BlockSpec(memory_space=pltpu.SMEM)],
        out_specs=pl.BlockSpec(memory_space=pltpu.VMEM),
        out_shape=jax.ShapeDtypeStruct((N_Q_HEADS, ATT_BLOCK, ATT_BLOCK), F32),
        name="rel_bias_table",
    )(rel_bias)


def _inproj_kernel(x_ref, mod_ref, g1_ref, win_ref, mu_ref, w0_ref, wlu_ref, a0_ref, alu_ref,
                   glu_ref, kk_ref, ka_ref, rk_ref, qg_ref, kg_ref, seg_ref, tri_ref,
                   at_out, bt_out, kt_out, rt_out, bh_out, kh_out, v_out, wl_out, bonus_out, g_out,
                   q_out, ak_out, av_out, carry_ref):
    @pl.when(pl.program_id(1) == 0)
    def _():
        carry_ref[...] = jnp.zeros_like(carry_ref)

    tm = tri_ref.shape[0]
    refs = (mu_ref, w0_ref, wlu_ref, a0_ref, alu_ref, glu_ref, kk_ref, ka_ref, rk_ref, qg_ref, kg_ref,
            seg_ref, tri_ref)
    outs = (at_out, bt_out, kt_out, rt_out, bh_out, kh_out, v_out, wl_out, bonus_out, g_out, q_out,
            ak_out, av_out)
    tail = iter(())
    for s in range(x_ref.shape[1] // tm):
        rs = pl.ds(s * tm, tm)
        x = x_ref[0, rs, :]
        ms = jnp.mean(x * x, axis=-1, keepdims=True)
        h = (x * lax.rsqrt(ms + NORM_EPS) * g1_ref[...] * (1.0 + mod_ref[0, 1:2, :])
             + mod_ref[0, 0:1, :]).astype(BF16)
        pieces = []
        for lo in range(0, win_ref.shape[1], MXU_TILE):
            pieces.append(jnp.dot(h, win_ref[:, lo:lo + MXU_TILE], preferred_element_type=F32))
            next(tail, None)
        for _ in tail:
            pass
        tail = _inproj_tail(jnp.concatenate(pieces, axis=1), rs, s * (tm // CHUNK), refs, outs, carry_ref)
    for _ in tail:
        pass


def _inproj_tail(p, rs, c0, refs, outs, carry_ref):
    (mu_ref, w0_ref, wlu_ref, a0_ref, alu_ref, glu_ref, kk_ref, ka_ref, rk_ref, qg_ref, kg_ref,
     seg_ref, tri_ref) = refs
    (at_out, bt_out, kt_out, rt_out, bh_out, kh_out, v_out, wl_out, bonus_out, g_out, q_out,
     ak_out, av_out) = outs
    tm = p.shape[0]

    mm = lambda a, b: jnp.dot(a, b, preferred_element_type=F32)

    pr = p[:, :RWKV_COLS]
    row = lax.broadcasted_iota(jnp.int32, pr.shape, 0)
    prev = jnp.where(row == 0, carry_ref[...], pltpu.roll(pr, 1, 0))
    carry_ref[...] = pr[tm - 1:tm, :]
    pm = pr + (prev - pr) * mu_ref[...]
    r = pm[:, 0:RWKV_W]
    k = pm[:, RWKV_W:2 * RWKV_W]
    v = pm[:, 2 * RWKV_W:3 * RWKV_W]
    o = 3 * RWKV_W
    tanh_w = jnp.tanh(pm[:, o:o + DECAY_LORA]).astype(BF16)
    xa = pm[:, o + DECAY_LORA:o + DECAY_LORA + AAA_LORA].astype(BF16)
    sig_g = _sigmoid(pm[:, o + DECAY_LORA + AAA_LORA:RWKV_COLS]).astype(BF16)
    yield
    q = p[:, RWKV_COLS:RWKV_COLS + ATT_Q_W]
    ak = p[:, RWKV_COLS + ATT_Q_W:RWKV_COLS + ATT_Q_W + ATT_KV_W]
    av = p[:, RWKV_COLS + ATT_Q_W + ATT_KV_W:]
    kk = k * kk_ref[...]
    kk_sq, q_sq, ak_sq = kk * kk, q * q, ak * ak
    v_out[0, rs, :] = v.astype(BF16)
    av_out[0, rs, :] = av.astype(BF16)
    yield

    seg = seg_ref[...]
    z_mm = mm(tanh_w, wlu_ref[...])
    a_mm = mm(xa, alu_ref[...])
    g_mm = mm(sig_g, glu_ref[...])
    kk_ss = _seg_sum(kk_sq, seg)
    q_ss = _seg_sum(q_sq, seg)
    ak_ss = _seg_sum(ak_sq, seg)
    yield

    lw = -math.exp(-0.5) * _sigmoid(w0_ref[...] + z_mm)
    lw_hi, lw_lo = _split(lw)
    a = _sigmoid(a0_ref[...] + a_mm)
    g_out[0, rs, :] = g_mm
    kk = kk / jnp.maximum(jnp.sqrt(kk_ss), 1e-12)
    bv = kk * a
    k = k * (1.0 + (a - 1.0) * ka_ref[...])
    rk_prod = r * k * rk_ref[...]
    yield
    q_out[0, rs, :] = (q * lax.rsqrt(q_ss * (1.0 / HEAD_DIM) + NORM_EPS) * (qg_ref[...] * ATT_SCALE)).astype(BF16)
    ak_out[0, rs, :] = (ak * lax.rsqrt(ak_ss * (1.0 / HEAD_DIM) + NORM_EPS) * kg_ref[...]).astype(BF16)
    yield

    tri = tri_ref[...]
    cum = mm(tri, lw_hi) + mm(tri, lw_lo)
    bonus_ss = _seg_sum(rk_prod, seg)
    yield

    n_chunks = tm // CHUNK
    ends = [cum[(c + 1) * CHUNK - 1:(c + 1) * CHUNK, :] for c in range(n_chunks)]
    cl = jnp.concatenate([jnp.broadcast_to(e, (CHUNK, RWKV_W)) for e in ends], axis=0)
    for c in range(n_chunks):
        wl_out[0, c0 + c] = jnp.exp(ends[c])
    bonus_out[0, rs, :] = bonus_ss * v
    yield
    e_neg = jnp.exp(-cum)
    bt_out[0, rs, :] = (bv * e_neg).astype(BF16)
    kt_out[0, rs, :] = (k * e_neg).astype(BF16)
    yield
    e_hat = jnp.exp(cl - cum)
    bh_out[0, rs, :] = (bv * e_hat).astype(BF16)
    kh_out[0, rs, :] = (k * e_hat).astype(BF16)
    yield
    at_out[0, rs, :] = (-kk * jnp.exp(cum - lw)).astype(BF16)
    rt_out[0, rs, :] = (r * jnp.exp(cum)).astype(BF16)


def _in_projection(x, mod3, g1, w_in, mu, w0, wlu, a0, alu, glu, k_k, k_a, r_k, qg, kg, seg, tri):
    B, S, D = x.shape
    tm = min(IN_TILE, S)
    const = lambda shape: pl.BlockSpec(shape, lambda b, j: (0,) * len(shape))
    row_spec = lambda w: pl.BlockSpec((1, tm, w), lambda b, j: (b, j, 0))
    rows = lambda w, dt: jax.ShapeDtypeStruct((B, S, w), dt)
    return pl.pallas_call(
        _inproj_kernel,
        grid=(B, S // tm),
        in_specs=[row_spec(D),
                  pl.BlockSpec((1, N_MOD, D), lambda b, j: (b, 0, 0)),
                  const(g1.shape), const(w_in.shape), const(mu.shape), const(w0.shape),
                  const(wlu.shape), const(a0.shape), const(alu.shape), const(glu.shape),
                  const(k_k.shape), const(k_a.shape), const(r_k.shape), const(qg.shape),
                  const(kg.shape), const(seg.shape), const(tri.shape)],
        out_specs=([row_spec(RWKV_W)] * 7
                   + [pl.BlockSpec((1, tm // CHUNK, 1, RWKV_W), lambda b, j: (b, j, 0, 0))]
                   + [row_spec(RWKV_W)] * 2
                   + [row_spec(ATT_Q_W), row_spec(ATT_KV_W), row_spec(ATT_KV_W)]),
        out_shape=([rows(RWKV_W, BF16)] * 7
                   + [jax.ShapeDtypeStruct((B, S // CHUNK, 1, RWKV_W), F32)]
                   + [rows(RWKV_W, F32)] * 2
                   + [rows(ATT_Q_W, BF16), rows(ATT_KV_W, BF16), rows(ATT_KV_W, BF16)]),
        scratch_shapes=[pltpu.VMEM((1, RWKV_COLS), F32)],
        compiler_params=pltpu.CompilerParams(dimension_semantics=("arbitrary", "arbitrary"),
                                             vmem_limit_bytes=VMEM_LIMIT),
        name="in_projection",
    )(x, mod3, g1, w_in, mu, w0, wlu, a0, alu, glu, k_k, k_a, r_k, qg, kg, seg, tri)


def _rwkv_kernel(at_ref, bt_ref, kt_ref, rt_ref, bh_ref, kh_ref, v_ref, wl_ref, bonus_ref, g_ref,
                 lg_ref, lb_ref, seg_ref, o_ref, state_ref):
    @pl.when(pl.program_id(1) == 0)
    def _():
        state_ref[...] = jnp.zeros_like(state_ref)

    L, D, P = CHUNK, HEAD_DIM, 2 * HEAD_DIM
    n_chunks = at_ref.shape[1] // L
    n_pairs = RWKV_W // P
    row = lax.broadcasted_iota(jnp.int32, (L, P), 0)
    lane = lax.broadcasted_iota(jnp.int32, (L, P), 1)
    low = lane < D
    s_idx = jnp.where(low, lane, lane - D)
    strict = row > s_idx
    incl = row >= s_idx
    zeros_lp = jnp.zeros((L, P), F32)
    own = (lambda t: jnp.where(low, t, jnp.zeros_like(t)), lambda t: jnp.where(low, jnp.zeros_like(t), t))
    other = (own[1], own[0])
    diag = (lane == row, lane == row + D)
    seg = seg_ref[...]

    items = [(c, j, par) for c in range(n_chunks) for j in range(n_pairs) for par in range(2)]
    chunk_in = {}
    for c in range(n_chunks):
        rows = pl.ds(c * L, L)
        chunk_in[c] = dict(a_t=at_ref[0, rows, :], b_t=bt_ref[0, rows, :], k_t=kt_ref[0, rows, :],
                           r_t=rt_ref[0, rows, :], b_h=bh_ref[0, rows, :], k_h=kh_ref[0, rows, :],
                           v=v_ref[0, rows, :], w_l=wl_ref[0, c])

    top, bot, v_o, x = {}, {}, {}, {}
    for c in range(n_chunks):
        ci = chunk_in[c]
        for j in range(n_pairs):
            ps = slice(j * P, (j + 1) * P)
            atp, rtp = ci["a_t"][:, ps], ci["r_t"][:, ps]
            lhs = jnp.concatenate([own[0](atp), own[0](rtp), own[1](atp), own[1](rtp)], axis=0)
            s1 = _dot_nt(lhs, jnp.concatenate([ci["b_t"][:, ps], ci["k_t"][:, ps]], axis=0))
            v_swap = pltpu.roll(ci["v"][:, ps].astype(F32), D, 1)
            for par in range(2):
                it = (c, j, par)
                top[it] = jnp.where(strict, s1[2 * L * par:2 * L * par + L], 0.0)
                bot[it] = jnp.where(incl, s1[2 * L * par + L:2 * L * (par + 1)], 0.0)
                v_o[it] = other[par](v_swap)
                x[it] = own[par](atp).astype(F32)

    for it in items:
        x[it] = x[it] + _dot(top[it], jnp.concatenate([zeros_lp, v_o[it]], axis=0))

    a_pow = {it: top[it][:, :D].astype(BF16) for it in items}
    n = 1
    while n < L:
        last = 2 * n >= L
        for it in items:
            xb = x[it].astype(BF16)
            if last:
                x[it] = x[it] + jnp.dot(a_pow[it], xb, preferred_element_type=F32)
            else:
                res = jnp.dot(a_pow[it], jnp.concatenate([xb, a_pow[it]], axis=1),
                              preferred_element_type=F32)
                x[it] = x[it] + res[:, :P]
                a_pow[it] = res[:, P:].astype(BF16)
        n *= 2

    rhs2 = {it: jnp.concatenate([x[it], v_o[it]], axis=0).astype(BF16) for it in items}
    rb = {it: jnp.dot(bot[it].astype(BF16), rhs2[it], preferred_element_type=F32) for it in items}
    mn = {}
    for c in range(n_chunks):
        ci = chunk_in[c]
        for j in range(n_pairs):
            ps = slice(j * P, (j + 1) * P)
            res = _dot_tn(jnp.concatenate([ci["b_h"][:, ps], ci["k_h"][:, ps]], axis=0),
                          jnp.concatenate([rhs2[(c, j, 0)], rhs2[(c, j, 1)]], axis=1))
            for par in range(2):
                mn[(c, j, par)] = res[D * par:D * (par + 1), P * par:P * (par + 1)]

    y = {}
    for c in range(n_chunks):
        ci = chunk_in[c]
        for j in range(n_pairs):
            ps = slice(j * P, (j + 1) * P)
            for par in range(2):
                it = (c, j, par)
                h = 2 * j + par
                g_p = rb[it] + ci["r_t"][:, ps]
                m_p = mn[it] + jnp.where(diag[par], ci["w_l"][:, ps], 0.0)
                z = state_ref[h]
                rhs = jnp.concatenate([z, zeros_lp] if par == 0 else [zeros_lp, z], axis=0)
                res = _dot(jnp.concatenate([g_p, m_p], axis=0), rhs)
                y[it] = res[:L] + rb[it]
                state_ref[h] = other[par](res[L:] + mn[it])

    for c in range(n_chunks):
        rows = pl.ds(c * L, L)
        yc = jnp.concatenate(
            [pltpu.roll(jnp.where(low, y[(c, j, 1)], y[(c, j, 0)]), D, 1) for j in range(n_pairs)], axis=1)
        mu = _seg_sum(yc, seg) * (1.0 / HEAD_DIM)
        yc = yc - mu
        var = _seg_sum(yc * yc, seg) * (1.0 / HEAD_DIM)
        yn = yc * lax.rsqrt(var + RWKV_GN_EPS) * lg_ref[...] + lb_ref[...]
        o_ref[0, rows, :] = ((yn + bonus_ref[0, rows, :]) * g_ref[0, rows, :]).astype(BF16)


def _rwkv(at, bt, kt, rt, bh, kh, v, wl, bonus, g, lnx_g, lnx_b, seg):
    B, S, W = at.shape
    tc = min(RWKV_TILE, S)
    const = lambda shape: pl.BlockSpec(shape, lambda b, j: (0,) * len(shape))
    row_spec = pl.BlockSpec((1, tc, W), lambda b, j: (b, j, 0))
    return pl.pallas_call(
        _rwkv_kernel,
        grid=(B, S // tc),
        in_specs=([row_spec] * 7 + [pl.BlockSpec((1, tc // CHUNK, 1, W), lambda b, j: (b, j, 0, 0))]
                  + [row_spec] * 2 + [const(lnx_g.shape), const(lnx_b.shape), const(seg.shape)]),
        out_specs=row_spec,
        out_shape=jax.ShapeDtypeStruct((B, S, W), BF16),
        scratch_shapes=[pltpu.VMEM((RWKV_HEADS, CHUNK, 2 * HEAD_DIM), F32)],
        compiler_params=pltpu.CompilerParams(dimension_semantics=("arbitrary", "arbitrary"),
                                             vmem_limit_bytes=VMEM_LIMIT),
        name="rwkv7_chunked",
    )(at, bt, kt, rt, bh, kh, v, wl, bonus, g, lnx_g, lnx_b, seg)


def _swa_kernel(q_ref, kc_ref, kp_ref, vc_ref, vp_ref, bias_ref, sink_ref, o_ref):
    first = pl.program_id(1) == 0
    rows = GQA_GROUP * ATT_BLOCK
    col = lax.broadcasted_iota(jnp.int32, (rows, ATT_BLOCK), 1)
    qpos = lax.broadcasted_iota(jnp.int32, (rows, ATT_BLOCK), 0) % ATT_BLOCK
    use_cur = col <= qpos
    no_key = jnp.logical_and(first, jnp.logical_not(use_cur))
    row = lax.broadcasted_iota(jnp.int32, (rows, 1), 0)
    sink = []
    for hk in range(N_KV_HEADS):
        s = jnp.full((rows, 1), sink_ref[hk * GQA_GROUP], F32)
        for g in range(1, GQA_GROUP):
            s = jnp.where(row >= g * ATT_BLOCK, sink_ref[hk * GQA_GROUP + g], s)
        sink.append(s)

    items = [(blk, hk) for blk in range(q_ref.shape[1] // ATT_BLOCK) for hk in range(N_KV_HEADS)]
    logits, vw = {}, {}
    for blk, hk in items:
        ks = slice(hk * HEAD_DIM, (hk + 1) * HEAD_DIM)
        rs = pl.ds(blk * ATT_BLOCK, ATT_BLOCK)
        if blk == 0:
            k_prev, v_prev = kp_ref[0, :, ks], vp_ref[0, :, ks]
        else:
            ps = pl.ds((blk - 1) * ATT_BLOCK, ATT_BLOCK)
            k_prev, v_prev = kc_ref[0, ps, ks], vc_ref[0, ps, ks]
        kw = jnp.concatenate([k_prev, kc_ref[0, rs, ks]], axis=0)
        vw[blk, hk] = jnp.concatenate([v_prev, vc_ref[0, rs, ks]], axis=0)
        q = q_ref[0, rs, :]
        qg = jnp.concatenate([q[:, (hk * GQA_GROUP + g) * HEAD_DIM:(hk * GQA_GROUP + g + 1) * HEAD_DIM]
                              for g in range(GQA_GROUP)], axis=0)
        lg = lax.dot_general(qg, kw, (((1,), (1,)), ((), ())), preferred_element_type=F32)
        lg = jnp.where(use_cur, lg[:, ATT_BLOCK:], lg[:, :ATT_BLOCK]) + bias_ref[hk]
        logits[blk, hk] = jnp.where(no_key, NEG_INF, lg) if blk == 0 else lg
    m = {it: jnp.maximum(jnp.max(logits[it], axis=-1, keepdims=True), sink[it[1]]) for it in items}
    e = {it: jnp.exp(logits[it] - m[it]) for it in items}
    den = {it: jnp.sum(e[it], axis=-1, keepdims=True) + jnp.exp(sink[it[1]] - m[it]) for it in items}
    for blk, hk in items:
        eb = e[blk, hk].astype(BF16)
        zero = jnp.zeros_like(eb)
        e2 = jnp.concatenate([jnp.where(use_cur, zero, eb), jnp.where(use_cur, eb, zero)], axis=1)
        out = jnp.dot(e2, vw[blk, hk], preferred_element_type=F32) / den[blk, hk]
        for g in range(GQA_GROUP):
            h = hk * GQA_GROUP + g
            o_ref[0, pl.ds(blk * ATT_BLOCK, ATT_BLOCK), h * HEAD_DIM:(h + 1) * HEAD_DIM] = (
                out[g * ATT_BLOCK:(g + 1) * ATT_BLOCK].astype(BF16))


def _swa(q, ak, av, bias, sinks):
    B, S, _ = q.shape
    tq = min(SWA_TILE, S)
    nb = tq // ATT_BLOCK
    bias = bias.reshape(N_KV_HEADS, GQA_GROUP * ATT_BLOCK, ATT_BLOCK)
    cur = lambda w: pl.BlockSpec((1, tq, w), lambda b, i: (b, i, 0))
    prev = lambda w: pl.BlockSpec((1, ATT_BLOCK, w), lambda b, i: (b, jnp.maximum(i * nb - 1, 0), 0))
    return pl.pallas_call(
        _swa_kernel,
        grid=(B, S // tq),
        in_specs=[cur(ATT_Q_W), cur(ATT_KV_W), prev(ATT_KV_W), cur(ATT_KV_W), prev(ATT_KV_W),
                  pl.BlockSpec(bias.shape, lambda b, i: (0, 0, 0)),
                  pl.BlockSpec(memory_space=pltpu.SMEM)],
        out_specs=cur(ATT_Q_W),
        out_shape=jax.ShapeDtypeStruct((B, S, ATT_Q_W), BF16),
        compiler_params=pltpu.CompilerParams(dimension_semantics=("arbitrary", "arbitrary")),
        name="swa_attention",
    )(q, ak, ak, av, av, bias, sinks)


def _ffn_kernel(x_ref, yr_ref, ya_ref, mod_ref, g2_ref, wo_ref, wg_ref, wu_ref, wd_ref, o_ref):
    x = x_ref[0]
    wo = wo_ref[...]
    mix = (jnp.dot(yr_ref[0], wo[:RWKV_W], preferred_element_type=F32)
           + jnp.dot(ya_ref[0], wo[RWKV_W:], preferred_element_type=F32))
    h_res = x + mod_ref[0, 2:3, :] * mix
    ms = jnp.mean(h_res * h_res, axis=-1, keepdims=True)
    h2 = (h_res * lax.rsqrt(ms + NORM_EPS) * g2_ref[...] * (1.0 + mod_ref[0, 4:5, :])
          + mod_ref[0, 3:4, :]).astype(BF16)
    d_ff = wg_ref.shape[1]
    ffn = jnp.zeros_like(x)
    for lo in range(0, d_ff, FFN_BLOCK):
        cs = slice(lo, min(lo + FFN_BLOCK, d_ff))
        gt = jnp.dot(h2, wg_ref[:, cs], preferred_element_type=F32)
        up = jnp.dot(h2, wu_ref[:, cs], preferred_element_type=F32)
        act = (gt * _sigmoid(gt) * up).astype(BF16)
        ffn = ffn + jnp.dot(act, wd_ref[cs, :], preferred_element_type=F32)
    o_ref[0] = h_res + mod_ref[0, 5:6, :] * ffn


def _out_ffn(x, y_rwkv, y_att, mod3, g2, w_out, w_gate, w_up, w_down):
    B, S, D = x.shape
    tm = min(FFN_TILE, S)
    resident = lambda shape: pl.BlockSpec(shape, lambda b, j: (0,) * len(shape),
                                          pipeline_mode=pl.Buffered(1))
    row_spec = lambda w: pl.BlockSpec((1, tm, w), lambda b, j: (b, j, 0))
    return pl.pallas_call(
        _ffn_kernel,
        grid=(B, S // tm),
        in_specs=[row_spec(D), row_spec(RWKV_W), row_spec(ATT_Q_W),
                  pl.BlockSpec((1, N_MOD, D), lambda b, j: (b, 0, 0)),
                  resident(g2.shape), resident(w_out.shape), resident(w_gate.shape),
                  resident(w_up.shape), resident(w_down.shape)],
        out_specs=row_spec(D),
        out_shape=jax.ShapeDtypeStruct((B, S, D), F32),
        compiler_params=pltpu.CompilerParams(dimension_semantics=("arbitrary", "arbitrary"),
                                             vmem_limit_bytes=VMEM_LIMIT),
        name="out_proj_ffn",
    )(x, y_rwkv, y_att, mod3, g2, w_out, w_gate, w_up, w_down)


def kernel(x, c, w_ada, b_ada, norm1_g, w_in, rwkv_mu, w0, w_lora_up, a0, a_lora_up, g_lora_up, k_k, k_a, r_k, lnx_g, lnx_b, q_norm_g, k_norm_g, sinks, rel_bias, w_out, norm2_g, w_gate, w_up, w_down):
    B, S, D = x.shape
    row = lambda t: t.reshape(1, -1).astype(F32)
    lane = jnp.arange(MXU_TILE) // HEAD_DIM
    seg = (lane[:, None] == lane[None, :]).astype(BF16)
    t_idx = jnp.arange(min(IN_SUB, S))
    tri = ((t_idx[:, None] >= t_idx[None, :])
           & (t_idx[:, None] // CHUNK == t_idx[None, :] // CHUNK)).astype(BF16)

    mod3 = _modulation(c.astype(F32), w_ada, b_ada).reshape(B, N_MOD, D)
    bias = _bias_table(rel_bias)

    at, bt, kt, rt, bh, kh, v, wl, bonus, g, q, ak, av = _in_projection(
        x, mod3, row(norm1_g), w_in.astype(BF16), row(rwkv_mu), row(w0), w_lora_up.astype(BF16),
        row(a0), a_lora_up.astype(BF16), g_lora_up.astype(BF16), row(k_k), row(k_a), row(r_k),
        row(jnp.tile(q_norm_g, N_Q_HEADS)), row(jnp.tile(k_norm_g, N_KV_HEADS)), seg, tri)

    y_rwkv = _rwkv(at, bt, kt, rt, bh, kh, v, wl, bonus, g, row(lnx_g), row(lnx_b), seg)
    y_att = _swa(q, ak, av, bias, sinks)

    out = _out_ffn(x, y_rwkv, y_att, mod3, row(norm2_g), w_out.astype(BF16),
                   w_gate.astype(BF16), w_up.astype(BF16), w_down.astype(BF16))
    return out.astype(x.dtype)
```

```python
import functools
import math

import jax
import jax.numpy as jnp
from jax import lax
from jax.experimental import pallas as pl
from jax.experimental.pallas import tpu as pltpu

F32 = jnp.float32
BF16 = jnp.bfloat16

HEAD_DIM = 64
RWKV_HEADS = 8
RWKV_W = RWKV_HEADS * HEAD_DIM
DECAY_LORA = 64
AAA_LORA = 64
GATE_LORA = 128
RWKV_COLS = 3 * RWKV_W + DECAY_LORA + AAA_LORA + GATE_LORA
RWKV_GN_EPS = 64e-5
N_Q_HEADS = 8
N_KV_HEADS = 2
GQA_GROUP = N_Q_HEADS // N_KV_HEADS
ATT_Q_W = N_Q_HEADS * HEAD_DIM
ATT_KV_W = N_KV_HEADS * HEAD_DIM
ATT_BLOCK = 128
WINDOW = 128
ATT_SCALE = 1.0 / math.sqrt(HEAD_DIM)
NUM_BUCKETS = 32
MAX_DISTANCE = 128
N_MOD = 6
NORM_EPS = 1e-6
NEG_INF = -1e30

CHUNK = 64
IN_TILE = 1024
IN_SUB = 256
RWKV_TILE = 512
RWKV_GROUP = 8
SWA_TILE = 512
FFN_TILE = 512
MXU_TILE = 256
FFN_BLOCK = 4 * MXU_TILE
VMEM_LIMIT = 56 * 1024 * 1024


def _dot(a, b):
    return jnp.dot(a.astype(BF16), b.astype(BF16), preferred_element_type=F32)


def _dot_nt(a, b):
    return lax.dot_general(a.astype(BF16), b.astype(BF16), (((1,), (1,)), ((), ())),
                           preferred_element_type=F32)


def _dot_tn(a, b):
    return lax.dot_general(a.astype(BF16), b.astype(BF16), (((0,), (0,)), ((), ())),
                           preferred_element_type=F32)


def _split(x):
    hi = x.astype(BF16)
    lo = (x - hi.astype(F32)).astype(BF16)
    return hi, lo


def _seg_sum(x, seg):
    m, w = x.shape
    xb = x.astype(BF16)
    blk = seg.shape[0]
    if w <= blk:
        return jnp.dot(xb, seg[:w, :w], preferred_element_type=F32)
    n = w // blk
    stacked = jnp.concatenate([xb[:, i * blk:(i + 1) * blk] for i in range(n)], axis=0)
    res = jnp.dot(stacked, seg, preferred_element_type=F32)
    return jnp.concatenate([res[i * m:(i + 1) * m] for i in range(n)], axis=1)


def _dot_lhs_exact(m, x):
    hi, lo = _split(x)
    return (jnp.dot(m, hi, preferred_element_type=F32) + jnp.dot(m, lo, preferred_element_type=F32))


def _sigmoid(z):
    return 1.0 / (1.0 + jnp.exp(-z))


def _mod_kernel(c_ref, w_ref, b_ref, o_ref):
    c = c_ref[...]
    s = c * _sigmoid(c)
    s_hi, s_lo = _split(s)
    w = w_ref[...]
    w_hi, w_lo = _split(w)
    d = lambda x, y: jnp.dot(x, y, preferred_element_type=F32)
    o_ref[...] = d(s_hi, w_hi) + d(s_hi, w_lo) + d(s_lo, w_hi) + b_ref[...]


def _modulation(c, w_ada, b_ada):
    B, D = c.shape
    n = w_ada.shape[1]
    blk = D
    return pl.pallas_call(
        _mod_kernel,
        grid=(n // blk,),
        in_specs=[pl.BlockSpec((B, D), lambda j: (0, 0)),
                  pl.BlockSpec((D, blk), lambda j: (0, j)),
                  pl.BlockSpec((1, blk), lambda j: (0, j))],
        out_specs=pl.BlockSpec((B, blk), lambda j: (0, j)),
        out_shape=jax.ShapeDtypeStruct((B, n), F32),
        name="adaln_mod",
    )(c, w_ada, b_ada.reshape(1, n))


def _t5_thresholds():
    max_exact = NUM_BUCKETS // 2
    out = []
    for k in range(1, NUM_BUCKETS - max_exact):
        edge = max_exact * (MAX_DISTANCE / max_exact) ** (k / (NUM_BUCKETS - max_exact))
        assert abs(edge - round(edge)) > 1e-6, "bucket edge on an integer distance: floor() would be ambiguous"
        out.append(math.ceil(edge))
    return tuple(out)


_T5_THRESHOLDS = _t5_thresholds()


def _bias_kernel(rb_ref, o_ref):
    qi = lax.broadcasted_iota(jnp.int32, (ATT_BLOCK, ATT_BLOCK), 0)
    kj = lax.broadcasted_iota(jnp.int32, (ATT_BLOCK, ATT_BLOCK), 1)
    n = jnp.where(kj <= qi, qi - kj, qi + ATT_BLOCK - kj)
    max_exact = NUM_BUCKETS // 2
    large = jnp.full(n.shape, max_exact, jnp.int32)
    for t in _T5_THRESHOLDS:
        large = large + jnp.where(n >= t, 1, 0)
    bucket = jnp.where(n < max_exact, n, large)
    for h in range(N_Q_HEADS):
        acc = jnp.zeros((ATT_BLOCK, ATT_BLOCK), F32)
        for b in range(NUM_BUCKETS):
            acc = jnp.where(bucket == b, rb_ref[b, h], acc)
        o_ref[h] = acc


def _bias_table(rel_bias):
    assert WINDOW == ATT_BLOCK
    return pl.pallas_call(
        _bias_kernel,
        in_specs=[pl.BlockSpec(memory_space=pltpu.SMEM)],
        out_specs=pl.BlockSpec(memory_space=pltpu.VMEM),
        out_shape=jax.ShapeDtypeStruct((N_Q_HEADS, ATT_BLOCK, ATT_BLOCK), F32),
        name="rel_bias_table",
    )(rel_bias)


def _inproj_kernel(x_ref, mod_ref, g1_ref, win_ref, mu_ref, w0_ref, wlu_ref, a0_ref, alu_ref,
                   glu_ref, kk_ref, ka_ref, rk_ref, qg_ref, kg_ref, seg_ref, tri_ref,
                   at_out, bt_out, kt_out, rt_out, bh_out, kh_out, v_out, wl_out, bonus_out, g_out,
                   q_out, ak_out, av_out, carry_ref):
    @pl.when(pl.program_id(1) == 0)
    def _():
        carry_ref[...] = jnp.zeros_like(carry_ref)

    tm = tri_ref.shape[0]
    refs = (mu_ref, w0_ref, wlu_ref, a0_ref, alu_ref, glu_ref, kk_ref, ka_ref, rk_ref, qg_ref, kg_ref,
            seg_ref, tri_ref)
    outs = (at_out, bt_out, kt_out, rt_out, bh_out, kh_out, v_out, wl_out, bonus_out, g_out, q_out,
            ak_out, av_out)
    tail = iter(())
    for s in range(x_ref.shape[1] // tm):
        rs = pl.ds(s * tm, tm)
        x = x_ref[0, rs, :]
        ms = jnp.mean(x * x, axis=-1, keepdims=True)
        h = (x * lax.rsqrt(ms + NORM_EPS) * g1_ref[...] * (1.0 + mod_ref[0, 1:2, :])
             + mod_ref[0, 0:1, :]).astype(BF16)
        pieces = []
        for lo in range(0, win_ref.shape[1], MXU_TILE):
            pieces.append(jnp.dot(h, win_ref[:, lo:lo + MXU_TILE], preferred_element_type=F32))
            next(tail, None)
        for _ in tail:
            pass
        tail = _inproj_tail(jnp.concatenate(pieces, axis=1), rs, s * (tm // CHUNK), refs, outs, carry_ref)
    for _ in tail:
        pass


def _inproj_tail(p, rs, c0, refs, outs, carry_ref):
    (mu_ref, w0_ref, wlu_ref, a0_ref, alu_ref, glu_ref, kk_ref, ka_ref, rk_ref, qg_ref, kg_ref,
     seg_ref, tri_ref) = refs
    (at_out, bt_out, kt_out, rt_out, bh_out, kh_out, v_out, wl_out, bonus_out, g_out, q_out,
     ak_out, av_out) = outs
    tm = p.shape[0]

    mm = lambda a, b: jnp.dot(a, b, preferred_element_type=F32)

    pr = p[:, :RWKV_COLS]
    row = lax.broadcasted_iota(jnp.int32, pr.shape, 0)
    prev = jnp.where(row == 0, carry_ref[...], pltpu.roll(pr, 1, 0))
    carry_ref[...] = pr[tm - 1:tm, :]
    pm = pr + (prev - pr) * mu_ref[...]
    r = pm[:, 0:RWKV_W]
    k = pm[:, RWKV_W:2 * RWKV_W]
    v = pm[:, 2 * RWKV_W:3 * RWKV_W]
    o = 3 * RWKV_W
    tanh_w = jnp.tanh(pm[:, o:o + DECAY_LORA]).astype(BF16)
    xa = pm[:, o + DECAY_LORA:o + DECAY_LORA + AAA_LORA].astype(BF16)
    sig_g = _sigmoid(pm[:, o + DECAY_LORA + AAA_LORA:RWKV_COLS]).astype(BF16)
    yield
    q = p[:, RWKV_COLS:RWKV_COLS + ATT_Q_W]
    ak = p[:, RWKV_COLS + ATT_Q_W:RWKV_COLS + ATT_Q_W + ATT_KV_W]
    av = p[:, RWKV_COLS + ATT_Q_W + ATT_KV_W:]
    kk = k * kk_ref[...]
    kk_sq, q_sq, ak_sq = kk * kk, q * q, ak * ak
    v_out[0, rs, :] = v.astype(BF16)
    av_out[0, rs, :] = av.astype(BF16)
    yield

    seg = seg_ref[...]
    z_mm = mm(tanh_w, wlu_ref[...])
    a_mm = mm(xa, alu_ref[...])
    g_mm = mm(sig_g, glu_ref[...])
    kk_ss = _seg_sum(kk_sq, seg)
    q_ss = _seg_sum(q_sq, seg)
    ak_ss = _seg_sum(ak_sq, seg)
    yield

    lw = -math.exp(-0.5) * _sigmoid(w0_ref[...] + z_mm)
    lw_hi, lw_lo = _split(lw)
    a = _sigmoid(a0_ref[...] + a_mm)
    g_out[0, rs, :] = g_mm
    kk = kk / jnp.maximum(jnp.sqrt(kk_ss), 1e-12)
    bv = kk * a
    k = k * (1.0 + (a - 1.0) * ka_ref[...])
    rk_prod = r * k * rk_ref[...]
    yield
    q_out[0, rs, :] = (q * lax.rsqrt(q_ss * (1.0 / HEAD_DIM) + NORM_EPS) * (qg_ref[...] * ATT_SCALE)).astype(BF16)
    ak_out[0, rs, :] = (ak * lax.rsqrt(ak_ss * (1.0 / HEAD_DIM) + NORM_EPS) * kg_ref[...]).astype(BF16)
    yield

    tri = tri_ref[...]
    cum = mm(tri, lw_hi) + mm(tri, lw_lo)
    bonus_ss = _seg_sum(rk_prod, seg)
    yield

    n_chunks = tm // CHUNK
    ends = [cum[(c + 1) * CHUNK - 1:(c + 1) * CHUNK, :] for c in range(n_chunks)]
    cl = jnp.concatenate([jnp.broadcast_to(e, (CHUNK, RWKV_W)) for e in ends], axis=0)
    for c in range(n_chunks):
        wl_out[0, c0 + c] = jnp.exp(ends[c])
    bonus_out[0, rs, :] = bonus_ss * v
    yield
    e_neg = jnp.exp(-cum)
    bt_out[0, rs, :] = (bv * e_neg).astype(BF16)
    kt_out[0, rs, :] = (k * e_neg).astype(BF16)
    yield
    e_hat = jnp.exp(cl - cum)
    bh_out[0, rs, :] = (bv * e_hat).astype(BF16)
    kh_out[0, rs, :] = (k * e_hat).astype(BF16)
    yield
    at_out[0, rs, :] = (-kk * jnp.exp(cum - lw)).astype(BF16)
    rt_out[0, rs, :] = (r * jnp.exp(cum)).astype(BF16)


def _in_projection(x, mod3, g1, w_in, mu, w0, wlu, a0, alu, glu, k_k, k_a, r_k, qg, kg, seg, tri):
    B, S, D = x.shape
    tm = min(IN_TILE, S)
    const = lambda shape: pl.BlockSpec(shape, lambda b, j: (0,) * len(shape))
    row_spec = lambda w: pl.BlockSpec((1, tm, w), lambda b, j: (b, j, 0))
    rows = lambda w, dt: jax.ShapeDtypeStruct((B, S, w), dt)
    return pl.pallas_call(
        _inproj_kernel,
        grid=(B, S // tm),
        in_specs=[row_spec(D),
                  pl.BlockSpec((1, N_MOD, D), lambda b, j: (b, 0, 0)),
                  const(g1.shape), const(w_in.shape), const(mu.shape), const(w0.shape),
                  const(wlu.shape), const(a0.shape), const(alu.shape), const(glu.shape),
                  const(k_k.shape), const(k_a.shape), const(r_k.shape), const(qg.shape),
                  const(kg.shape), const(seg.shape), const(tri.shape)],
        out_specs=([row_spec(RWKV_W)] * 7
                   + [pl.BlockSpec((1, tm // CHUNK, 1, RWKV_W), lambda b, j: (b, j, 0, 0))]
                   + [row_spec(RWKV_W)] * 2
                   + [row_spec(ATT_Q_W), row_spec(ATT_KV_W), row_spec(ATT_KV_W)]),
        out_shape=([rows(RWKV_W, BF16)] * 7
                   + [jax.ShapeDtypeStruct((B, S // CHUNK, 1, RWKV_W), F32)]
                   + [rows(RWKV_W, F32)] * 2
                   + [rows(ATT_Q_W, BF16), rows(ATT_KV_W, BF16), rows(ATT_KV_W, BF16)]),
        scratch_shapes=[pltpu.VMEM((1, RWKV_COLS), F32)],
        compiler_params=pltpu.CompilerParams(dimension_semantics=("arbitrary", "arbitrary"),
                                             vmem_limit_bytes=VMEM_LIMIT),
        name="in_projection",
    )(x, mod3, g1, w_in, mu, w0, wlu, a0, alu, glu, k_k, k_a, r_k, qg, kg, seg, tri)


def _rwkv_kernel(at_ref, bt_ref, kt_ref, rt_ref, bh_ref, kh_ref, v_ref, wl_ref, bonus_ref, g_ref,
                 lg_ref, lb_ref, seg_ref, o_ref, state_ref):
    @pl.when(pl.program_id(1) == 0)
    def _():
        state_ref[...] = jnp.zeros_like(state_ref)

    L, D, P = CHUNK, HEAD_DIM, 2 * HEAD_DIM
    n_chunks = at_ref.shape[1] // L
    n_pairs = RWKV_W // P
    row = lax.broadcasted_iota(jnp.int32, (L, P), 0)
    lane = lax.broadcasted_iota(jnp.int32, (L, P), 1)
    low = lane < D
    s_idx = jnp.where(low, lane, lane - D)
    strict = row > s_idx
    incl = row >= s_idx
    zeros_lp = jnp.zeros((L, P), F32)
    own = (lambda t: jnp.where(low, t, jnp.zeros_like(t)), lambda t: jnp.where(low, jnp.zeros_like(t), t))
    other = (own[1], own[0])
    diag = (lane == row, lane == row + D)
    seg = seg_ref[...]

    mid = {}

    def front(g):
        chunks = range(g * RWKV_GROUP, (g + 1) * RWKV_GROUP)
        items = [(c, j, par) for c in chunks for j in range(n_pairs) for par in range(2)]
        chunk_in, top, bot, v_o, x = {}, {}, {}, {}, {}
        for c in chunks:
            rows = pl.ds(c * L, L)
            ci = dict(a_t=at_ref[0, rows, :], b_t=bt_ref[0, rows, :], k_t=kt_ref[0, rows, :],
                      r_t=rt_ref[0, rows, :], b_h=bh_ref[0, rows, :], k_h=kh_ref[0, rows, :],
                      v=v_ref[0, rows, :], w_l=wl_ref[0, c])
            chunk_in[c] = ci
            for j in range(n_pairs):
                ps = slice(j * P, (j + 1) * P)
                atp, rtp = ci["a_t"][:, ps], ci["r_t"][:, ps]
                btp, ktp = ci["b_t"][:, ps], ci["k_t"][:, ps]
                s1 = _dot_nt(jnp.concatenate([atp, rtp], axis=0),
                             jnp.concatenate([own[0](btp), own[0](ktp), own[1](btp), own[1](ktp)], axis=0))
                v_swap = pltpu.roll(ci["v"][:, ps].astype(F32), D, 1)
                for par in range(2):
                    it = (c, j, par)
                    top[it] = jnp.where(strict, s1[:L, P * par:P * (par + 1)], 0.0)
                    bot[it] = jnp.where(incl, s1[L:, P * par:P * (par + 1)], 0.0)
                    v_o[it] = other[par](v_swap)
                    x[it] = own[par](atp).astype(F32)
        yield
        for it in items:
            x[it] = x[it] + _dot(top[it], jnp.concatenate([zeros_lp, v_o[it]], axis=0))
        yield
        a_pow = {it: top[it][:, :D].astype(BF16) for it in items}
        n = 1
        while n < L:
            last = 2 * n >= L
            for it in items:
                xb = x[it].astype(BF16)
                if last:
                    x[it] = x[it] + jnp.dot(a_pow[it], xb, preferred_element_type=F32)
                else:
                    res = jnp.dot(a_pow[it], jnp.concatenate([xb, a_pow[it]], axis=1),
                                  preferred_element_type=F32)
                    x[it] = x[it] + res[:, :P]
                    a_pow[it] = res[:, P:].astype(BF16)
            n *= 2
            mid[g] = (chunk_in, bot, v_o, x)
            yield

    def back(g):
        chunk_in, bot, v_o, x = mid.pop(g)
        chunks = range(g * RWKV_GROUP, (g + 1) * RWKV_GROUP)
        items = [(c, j, par) for c in chunks for j in range(n_pairs) for par in range(2)]
        rhs2 = {it: jnp.concatenate([x[it], v_o[it]], axis=0).astype(BF16) for it in items}
        rb = {it: jnp.dot(bot[it].astype(BF16), rhs2[it], preferred_element_type=F32) for it in items}
        mn = {}
        for c in chunks:
            ci = chunk_in[c]
            for j in range(n_pairs):
                ps = slice(j * P, (j + 1) * P)
                res = _dot_tn(jnp.concatenate([ci["b_h"][:, ps], ci["k_h"][:, ps]], axis=0),
                              jnp.concatenate([rhs2[(c, j, 0)], rhs2[(c, j, 1)]], axis=1))
                for par in range(2):
                    mn[(c, j, par)] = res[D * par:D * (par + 1), P * par:P * (par + 1)]
        yield
        y = {}
        for c in chunks:
            ci = chunk_in[c]
            for j in range(n_pairs):
                ps = slice(j * P, (j + 1) * P)
                for par in range(2):
                    it = (c, j, par)
                    h = 2 * j + par
                    g_p = rb[it] + ci["r_t"][:, ps]
                    m_p = mn[it] + jnp.where(diag[par], ci["w_l"][:, ps], 0.0)
                    z = state_ref[h]
                    rhs = jnp.concatenate([z, zeros_lp] if par == 0 else [zeros_lp, z], axis=0)
                    res = _dot(jnp.concatenate([g_p, m_p], axis=0), rhs)
                    y[it] = res[:L] + rb[it]
                    state_ref[h] = other[par](res[L:] + mn[it])
            yield
        for c in chunks:
            rows = pl.ds(c * L, L)
            yc = jnp.concatenate(
                [pltpu.roll(jnp.where(low, y[c, j, 1], y[c, j, 0]), D, 1) for j in range(n_pairs)], axis=1)
            mu = _seg_sum(yc, seg) * (1.0 / HEAD_DIM)
            yc = yc - mu
            var = _seg_sum(yc * yc, seg) * (1.0 / HEAD_DIM)
            yn = yc * lax.rsqrt(var + RWKV_GN_EPS) * lg_ref[...] + lb_ref[...]
            o_ref[0, rows, :] = ((yn + bonus_ref[0, rows, :]) * g_ref[0, rows, :]).astype(BF16)
            yield

    n_groups = n_chunks // RWKV_GROUP
    prev = iter(())
    for g in range(n_groups):
        cur = front(g)
        for _ in cur:
            next(prev, None)
        for _ in prev:
            pass
        prev = back(g)
    for _ in prev:
        pass


def _rwkv(at, bt, kt, rt, bh, kh, v, wl, bonus, g, lnx_g, lnx_b, seg):
    B, S, W = at.shape
    tc = min(RWKV_TILE, S)
    const = lambda shape: pl.BlockSpec(shape, lambda b, j: (0,) * len(shape))
    row_spec = pl.BlockSpec((1, tc, W), lambda b, j: (b, j, 0))
    return pl.pallas_call(
        _rwkv_kernel,
        grid=(B, S // tc),
        in_specs=([row_spec] * 7 + [pl.BlockSpec((1, tc // CHUNK, 1, W), lambda b, j: (b, j, 0, 0))]
                  + [row_spec] * 2 + [const(lnx_g.shape), const(lnx_b.shape), const(seg.shape)]),
        out_specs=row_spec,
        out_shape=jax.ShapeDtypeStruct((B, S, W), BF16),
        scratch_shapes=[pltpu.VMEM((RWKV_HEADS, CHUNK, 2 * HEAD_DIM), F32)],
        compiler_params=pltpu.CompilerParams(dimension_semantics=("arbitrary", "arbitrary"),
                                             vmem_limit_bytes=VMEM_LIMIT),
        name="rwkv7_chunked",
    )(at, bt, kt, rt, bh, kh, v, wl, bonus, g, lnx_g, lnx_b, seg)


def _swa_kernel(q_ref, kc_ref, kp_ref, vc_ref, vp_ref, bias_ref, sink_ref, o_ref):
    first = pl.program_id(1) == 0
    rows = GQA_GROUP * ATT_BLOCK
    col = lax.broadcasted_iota(jnp.int32, (rows, ATT_BLOCK), 1)
    qpos = lax.broadcasted_iota(jnp.int32, (rows, ATT_BLOCK), 0) % ATT_BLOCK
    use_cur = col <= qpos
    no_key = jnp.logical_and(first, jnp.logical_not(use_cur))
    row = lax.broadcasted_iota(jnp.int32, (rows, 1), 0)
    sink = []
    for hk in range(N_KV_HEADS):
        s = jnp.full((rows, 1), sink_ref[hk * GQA_GROUP], F32)
        for g in range(1, GQA_GROUP):
            s = jnp.where(row >= g * ATT_BLOCK, sink_ref[hk * GQA_GROUP + g], s)
        sink.append(s)

    items = [(blk, hk) for blk in range(q_ref.shape[1] // ATT_BLOCK) for hk in range(N_KV_HEADS)]
    logits, vw = {}, {}
    for blk, hk in items:
        ks = slice(hk * HEAD_DIM, (hk + 1) * HEAD_DIM)
        rs = pl.ds(blk * ATT_BLOCK, ATT_BLOCK)
        if blk == 0:
            k_prev, v_prev = kp_ref[0, :, ks], vp_ref[0, :, ks]
        else:
            ps = pl.ds((blk - 1) * ATT_BLOCK, ATT_BLOCK)
            k_prev, v_prev = kc_ref[0, ps, ks], vc_ref[0, ps, ks]
        kw = jnp.concatenate([k_prev, kc_ref[0, rs, ks]], axis=0)
        v2 = jnp.concatenate([v_prev, vc_ref[0, rs, ks]], axis=0)
        vw[blk, hk] = jnp.concatenate([v2, v2, jnp.ones((2 * ATT_BLOCK, 2 * HEAD_DIM), BF16)], axis=1)
        q = q_ref[0, rs, :]
        qg = jnp.concatenate([q[:, (hk * GQA_GROUP + g) * HEAD_DIM:(hk * GQA_GROUP + g + 1) * HEAD_DIM]
                              for g in range(GQA_GROUP)], axis=0)
        lg = lax.dot_general(qg, kw, (((1,), (1,)), ((), ())), preferred_element_type=F32)
        lg = jnp.where(use_cur, lg[:, ATT_BLOCK:], lg[:, :ATT_BLOCK]) + bias_ref[hk]
        logits[blk, hk] = jnp.where(no_key, NEG_INF, lg) if blk == 0 else lg
    m = {it: jnp.maximum(jnp.max(logits[it], axis=-1, keepdims=True), sink[it[1]]) for it in items}
    e = {it: jnp.exp(logits[it] - m[it]) for it in items}
    e_sink = {it: jnp.exp(sink[it[1]] - m[it]) for it in items}
    low = lax.broadcasted_iota(jnp.int32, (ATT_BLOCK, 2 * HEAD_DIM), 1) < HEAD_DIM
    for blk, hk in items:
        eb = e[blk, hk].astype(BF16)
        zero = jnp.zeros_like(eb)
        e2 = jnp.concatenate([jnp.where(use_cur, zero, eb), jnp.where(use_cur, eb, zero)], axis=1)
        res = jnp.dot(e2, vw[blk, hk], preferred_element_type=F32)
        out = res[:, :2 * HEAD_DIM] / (res[:, 2 * HEAD_DIM:] + e_sink[blk, hk])
        for g in range(0, GQA_GROUP, 2):
            h = hk * GQA_GROUP + g
            pair = jnp.where(low, out[g * ATT_BLOCK:(g + 1) * ATT_BLOCK], out[(g + 1) * ATT_BLOCK:(g + 2) * ATT_BLOCK])
            o_ref[0, pl.ds(blk * ATT_BLOCK, ATT_BLOCK), h * HEAD_DIM:(h + 2) * HEAD_DIM] = pair.astype(BF16)


def _swa(q, ak, av, bias, sinks):
    B, S, _ = q.shape
    tq = min(SWA_TILE, S)
    nb = tq // ATT_BLOCK
    bias = bias.reshape(N_KV_HEADS, GQA_GROUP * ATT_BLOCK, ATT_BLOCK)
    cur = lambda w: pl.BlockSpec((1, tq, w), lambda b, i: (b, i, 0))
    prev = lambda w: pl.BlockSpec((1, ATT_BLOCK, w), lambda b, i: (b, jnp.maximum(i * nb - 1, 0), 0))
    return pl.pallas_call(
        _swa_kernel,
        grid=(B, S // tq),
        in_specs=[cur(ATT_Q_W), cur(ATT_KV_W), prev(ATT_KV_W), cur(ATT_KV_W), prev(ATT_KV_W),
                  pl.BlockSpec(bias.shape, lambda b, i: (0, 0, 0)),
                  pl.BlockSpec(memory_space=pltpu.SMEM)],
        out_specs=cur(ATT_Q_W),
        out_shape=jax.ShapeDtypeStruct((B, S, ATT_Q_W), BF16),
        compiler_params=pltpu.CompilerParams(dimension_semantics=("arbitrary", "arbitrary")),
        name="swa_attention",
    )(q, ak, ak, av, av, bias, sinks)


def _ffn_kernel(x_ref, yr_ref, ya_ref, mod_ref, g2_ref, wo_ref, wg_ref, wu_ref, wd_ref, o_ref):
    x = x_ref[0]
    wo = wo_ref[...]
    mix = (jnp.dot(yr_ref[0], wo[:RWKV_W], preferred_element_type=F32)
           + jnp.dot(ya_ref[0], wo[RWKV_W:], preferred_element_type=F32))
    h_res = x + mod_ref[0, 2:3, :] * mix
    ms = jnp.mean(h_res * h_res, axis=-1, keepdims=True)
    h2 = (h_res * lax.rsqrt(ms + NORM_EPS) * g2_ref[...] * (1.0 + mod_ref[0, 4:5, :])
          + mod_ref[0, 3:4, :]).astype(BF16)
    d_ff = wg_ref.shape[1]
    ffn = jnp.zeros_like(x)
    for lo in range(0, d_ff, FFN_BLOCK):
        cs = slice(lo, min(lo + FFN_BLOCK, d_ff))
        gt = jnp.dot(h2, wg_ref[:, cs], preferred_element_type=F32)
        up = jnp.dot(h2, wu_ref[:, cs], preferred_element_type=F32)
        act = (gt * _sigmoid(gt) * up).astype(BF16)
        ffn = ffn + jnp.dot(act, wd_ref[cs, :], preferred_element_type=F32)
    o_ref[0] = h_res + mod_ref[0, 5:6, :] * ffn


def _out_ffn(x, y_rwkv, y_att, mod3, g2, w_out, w_gate, w_up, w_down):
    B, S, D = x.shape
    tm = min(FFN_TILE, S)
    resident = lambda shape: pl.BlockSpec(shape, lambda b, j: (0,) * len(shape),
                                          pipeline_mode=pl.Buffered(1))
    row_spec = lambda w: pl.BlockSpec((1, tm, w), lambda b, j: (b, j, 0))
    return pl.pallas_call(
        _ffn_kernel,
        grid=(B, S // tm),
        in_specs=[row_spec(D), row_spec(RWKV_W), row_spec(ATT_Q_W),
                  pl.BlockSpec((1, N_MOD, D), lambda b, j: (b, 0, 0)),
                  resident(g2.shape), resident(w_out.shape), resident(w_gate.shape),
                  resident(w_up.shape), resident(w_down.shape)],
        out_specs=row_spec(D),
        out_shape=jax.ShapeDtypeStruct((B, S, D), F32),
        compiler_params=pltpu.CompilerParams(dimension_semantics=("arbitrary", "arbitrary"),
                                             vmem_limit_bytes=VMEM_LIMIT),
        name="out_proj_ffn",
    )(x, y_rwkv, y_att, mod3, g2, w_out, w_gate, w_up, w_down)


def kernel(x, c, w_ada, b_ada, norm1_g, w_in, rwkv_mu, w0, w_lora_up, a0, a_lora_up, g_lora_up, k_k, k_a, r_k, lnx_g, lnx_b, q_norm_g, k_norm_g, sinks, rel_bias, w_out, norm2_g, w_gate, w_up, w_down):
    B, S, D = x.shape
    row = lambda t: t.reshape(1, -1).astype(F32)
    lane = jnp.arange(MXU_TILE) // HEAD_DIM
    seg = (lane[:, None] == lane[None, :]).astype(BF16)
    t_idx = jnp.arange(min(IN_SUB, S))
    tri = ((t_idx[:, None] >= t_idx[None, :])
           & (t_idx[:, None] // CHUNK == t_idx[None, :] // CHUNK)).astype(BF16)

    mod3 = _modulation(c.astype(F32), w_ada, b_ada).reshape(B, N_MOD, D)
    bias = _bias_table(rel_bias)

    at, bt, kt, rt, bh, kh, v, wl, bonus, g, q, ak, av = _in_projection(
        x, mod3, row(norm1_g), w_in.astype(BF16), row(rwkv_mu), row(w0), w_lora_up.astype(BF16),
        row(a0), a_lora_up.astype(BF16), g_lora_up.astype(BF16), row(k_k), row(k_a), row(r_k),
        row(jnp.tile(q_norm_g, N_Q_HEADS)), row(jnp.tile(k_norm_g, N_KV_HEADS)), seg, tri)

    y_rwkv = _rwkv(at, bt, kt, rt, bh, kh, v, wl, bonus, g, row(lnx_g), row(lnx_b), seg)
    y_att = _swa(q, ak, av, bias, sinks)

    out = _out_ffn(x, y_rwkv, y_att, mod3, row(norm2_g), w_out.astype(BF16),
                   w_gate.astype(BF16), w_up.astype(BF16), w_down.astype(BF16))
    return out.astype(x.dtype)
```

```python
import functools
import math

import jax
import jax.numpy as jnp
from jax import lax
from jax.experimental import pallas as pl
from jax.experimental.pallas import tpu as pltpu

F32 = jnp.float32
BF16 = jnp.bfloat16

HEAD_DIM = 64
RWKV_HEADS = 8
RWKV_W = RWKV_HEADS * HEAD_DIM
DECAY_LORA = 64
AAA_LORA = 64
GATE_LORA = 128
RWKV_COLS = 3 * RWKV_W + DECAY_LORA + AAA_LORA + GATE_LORA
RWKV_GN_EPS = 64e-5
N_Q_HEADS = 8
N_KV_HEADS = 2
GQA_GROUP = N_Q_HEADS // N_KV_HEADS
ATT_Q_W = N_Q_HEADS * HEAD_DIM
ATT_KV_W = N_KV_HEADS * HEAD_DIM
ATT_BLOCK = 128
WINDOW = 128
ATT_SCALE = 1.0 / math.sqrt(HEAD_DIM)
NUM_BUCKETS = 32
MAX_DISTANCE = 128
N_MOD = 6
NORM_EPS = 1e-6
NEG_INF = -1e30

CHUNK = 64
IN_TILE = 1024
IN_SUB = 256
RWKV_TILE = 512
RWKV_GROUP = 8
SWA_TILE = 512
FFN_TILE = 512
MXU_TILE = 256
F32_SUBLANES = 8
FFN_BLOCK = 4 * MXU_TILE
VMEM_LIMIT = 56 * 1024 * 1024


def _dot(a, b):
    return jnp.dot(a.astype(BF16), b.astype(BF16), preferred_element_type=F32)


def _dot_nt(a, b):
    return lax.dot_general(a.astype(BF16), b.astype(BF16), (((1,), (1,)), ((), ())),
                           preferred_element_type=F32)


def _dot_tn(a, b):
    return lax.dot_general(a.astype(BF16), b.astype(BF16), (((0,), (0,)), ((), ())),
                           preferred_element_type=F32)


def _split(x):
    hi = x.astype(BF16)
    lo = (x - hi.astype(F32)).astype(BF16)
    return hi, lo


def _seg_sum(x, seg):
    m, w = x.shape
    xb = x.astype(BF16)
    blk = seg.shape[0]
    if w <= blk:
        return jnp.dot(xb, seg[:w, :w], preferred_element_type=F32)
    n = w // blk
    stacked = jnp.concatenate([xb[:, i * blk:(i + 1) * blk] for i in range(n)], axis=0)
    res = jnp.dot(stacked, seg, preferred_element_type=F32)
    return jnp.concatenate([res[i * m:(i + 1) * m] for i in range(n)], axis=1)


def _dot_lhs_exact(m, x):
    hi, lo = _split(x)
    return (jnp.dot(m, hi, preferred_element_type=F32) + jnp.dot(m, lo, preferred_element_type=F32))


def _sigmoid(z):
    return 0.5 * jnp.tanh(0.5 * z) + 0.5


def _mod_kernel(c_ref, w_ref, b_ref, o_ref):
    c = c_ref[...]
    s = c * _sigmoid(c)
    s_hi, s_lo = _split(s)
    w = w_ref[...]
    w_hi, w_lo = _split(w)
    d = lambda x, y: jnp.dot(x, y, preferred_element_type=F32)
    o_ref[...] = d(s_hi, w_hi) + d(s_hi, w_lo) + d(s_lo, w_hi) + b_ref[...]


def _modulation(c, w_ada, b_ada):
    B, D = c.shape
    n = w_ada.shape[1]
    blk = D
    return pl.pallas_call(
        _mod_kernel,
        grid=(n // blk,),
        in_specs=[pl.BlockSpec((B, D), lambda j: (0, 0)),
                  pl.BlockSpec((D, blk), lambda j: (0, j)),
                  pl.BlockSpec((1, blk), lambda j: (0, j))],
        out_specs=pl.BlockSpec((B, blk), lambda j: (0, j)),
        out_shape=jax.ShapeDtypeStruct((B, n), F32),
        name="adaln_mod",
    )(c, w_ada, b_ada.reshape(1, n))


def _t5_thresholds():
    max_exact = NUM_BUCKETS // 2
    out = []
    for k in range(1, NUM_BUCKETS - max_exact):
        edge = max_exact * (MAX_DISTANCE / max_exact) ** (k / (NUM_BUCKETS - max_exact))
        assert abs(edge - round(edge)) > 1e-6, "bucket edge on an integer distance: floor() would be ambiguous"
        out.append(math.ceil(edge))
    return tuple(out)


_T5_THRESHOLDS = _t5_thresholds()


def _bias_kernel(rb_ref, o_ref):
    qi = lax.broadcasted_iota(jnp.int32, (ATT_BLOCK, ATT_BLOCK), 0)
    kj = lax.broadcasted_iota(jnp.int32, (ATT_BLOCK, ATT_BLOCK), 1)
    n = jnp.where(kj <= qi, qi - kj, qi + ATT_BLOCK - kj)
    max_exact = NUM_BUCKETS // 2
    large = jnp.full(n.shape, max_exact, jnp.int32)
    for t in _T5_THRESHOLDS:
        large = large + jnp.where(n >= t, 1, 0)
    bucket = jnp.where(n < max_exact, n, large)
    for h in range(N_Q_HEADS):
        acc = jnp.zeros((ATT_BLOCK, ATT_BLOCK), F32)
        for b in range(NUM_BUCKETS):
            acc = jnp.where(bucket == b, rb_ref[b, h], acc)
        o_ref[h] = acc


def _bias_table(rel_bias):
    assert WINDOW == ATT_BLOCK
    return pl.pallas_call(
        _bias_kernel,
        in_specs=[pl.BlockSpec(memory_space=pltpu.SMEM)],
        out_specs=pl.BlockSpec(memory_space=pltpu.VMEM),
        out_shape=jax.ShapeDtypeStruct((N_Q_HEADS, ATT_BLOCK, ATT_BLOCK), F32),
        name="rel_bias_table",
    )(rel_bias)


def _inproj_kernel(x_ref, mod_ref, g1_ref, win_ref, mu_ref, w0_ref, wlu_ref, a0_ref, alu_ref,
                   glu_ref, kk_ref, ka_ref, rk_ref, qg_ref, kg_ref, seg_ref, tri_ref,
                   at_out, bt_out, kt_out, rt_out, bh_out, kh_out, v_out, wl_out, bonus_out, g_out,
                   q_out, ak_out, av_out, carry_ref):
    @pl.when(pl.program_id(1) == 0)
    def _():
        carry_ref[...] = jnp.zeros_like(carry_ref)

    tm = tri_ref.shape[0]
    refs = (mu_ref, w0_ref, wlu_ref, a0_ref, alu_ref, glu_ref, kk_ref, ka_ref, rk_ref, qg_ref, kg_ref,
            seg_ref, tri_ref)
    outs = (at_out, bt_out, kt_out, rt_out, bh_out, kh_out, v_out, wl_out, bonus_out, g_out, q_out,
            ak_out, av_out)
    tail = iter(())
    gain = g1_ref[...] * (1.0 + mod_ref[0, 1:2, :])
    for s in range(x_ref.shape[1] // tm):
        rs = pl.ds(s * tm, tm)
        x = x_ref[0, rs, :]
        ms = jnp.mean(x * x, axis=-1, keepdims=True)
        h = (x * lax.rsqrt(ms + NORM_EPS) * gain + mod_ref[0, 0:1, :]).astype(BF16)
        pieces = []
        for lo in range(0, win_ref.shape[1], MXU_TILE):
            pieces.append(jnp.dot(h, win_ref[:, lo:lo + MXU_TILE], preferred_element_type=F32))
            next(tail, None)
        for _ in tail:
            pass
        tail = _inproj_tail(jnp.concatenate(pieces, axis=1), rs, s * (tm // CHUNK), refs, outs, carry_ref)
    for _ in tail:
        pass


def _inproj_tail(p, rs, c0, refs, outs, carry_ref):
    (mu_ref, w0_ref, wlu_ref, a0_ref, alu_ref, glu_ref, kk_ref, ka_ref, rk_ref, qg_ref, kg_ref,
     seg_ref, tri_ref) = refs
    (at_out, bt_out, kt_out, rt_out, bh_out, kh_out, v_out, wl_out, bonus_out, g_out, q_out,
     ak_out, av_out) = outs
    tm = p.shape[0]

    mm = lambda a, b: jnp.dot(a, b, preferred_element_type=F32)

    pr = p[:, :RWKV_COLS]
    rolled = pltpu.roll(pr, 1, 0)
    first = lax.broadcasted_iota(jnp.int32, (F32_SUBLANES, RWKV_COLS), 0) == 0
    prev = jnp.concatenate([jnp.where(first, carry_ref[...], rolled[:F32_SUBLANES]), rolled[F32_SUBLANES:]],
                           axis=0)
    carry_ref[...] = pr[tm - 1:tm, :]
    pm = pr + (prev - pr) * mu_ref[...]
    r = pm[:, 0:RWKV_W]
    k = pm[:, RWKV_W:2 * RWKV_W]
    v = pm[:, 2 * RWKV_W:3 * RWKV_W]
    o = 3 * RWKV_W
    tanh_w = jnp.tanh(pm[:, o:o + DECAY_LORA]).astype(BF16)
    xa = pm[:, o + DECAY_LORA:o + DECAY_LORA + AAA_LORA].astype(BF16)
    sig_g = _sigmoid(pm[:, o + DECAY_LORA + AAA_LORA:RWKV_COLS]).astype(BF16)
    yield
    q = p[:, RWKV_COLS:RWKV_COLS + ATT_Q_W]
    ak = p[:, RWKV_COLS + ATT_Q_W:RWKV_COLS + ATT_Q_W + ATT_KV_W]
    av = p[:, RWKV_COLS + ATT_Q_W + ATT_KV_W:]
    kk = k * kk_ref[...]
    kk_sq, q_sq, ak_sq = kk * kk, q * q, ak * ak
    v_out[0, rs, :] = v.astype(BF16)
    av_out[0, rs, :] = av.astype(BF16)
    yield

    seg = seg_ref[...]
    z_mm = mm(tanh_w, wlu_ref[...])
    a_mm = mm(xa, alu_ref[...])
    g_mm = mm(sig_g, glu_ref[...])
    kk_ss = _seg_sum(kk_sq, seg)
    q_ss = _seg_sum(q_sq, seg)
    ak_ss = _seg_sum(ak_sq, seg)
    yield

    lw = (-math.exp(-0.5) * math.log2(math.e)) * _sigmoid(w0_ref[...] + z_mm)
    lw_hi, lw_lo = _split(lw)
    a = _sigmoid(a0_ref[...] + a_mm)
    g_out[0, rs, :] = g_mm
    kk = kk * lax.rsqrt(jnp.maximum(kk_ss, 1e-24))
    bv = kk * a
    k = k * (1.0 + (a - 1.0) * ka_ref[...])
    rk_prod = r * k * rk_ref[...]
    yield
    q_out[0, rs, :] = (q * lax.rsqrt(q_ss * (1.0 / HEAD_DIM) + NORM_EPS) * (qg_ref[...] * ATT_SCALE)).astype(BF16)
    ak_out[0, rs, :] = (ak * lax.rsqrt(ak_ss * (1.0 / HEAD_DIM) + NORM_EPS) * kg_ref[...]).astype(BF16)
    yield

    tri = tri_ref[...]
    cum = mm(tri, lw_hi) + mm(tri, lw_lo)
    bonus_ss = _seg_sum(rk_prod, seg)
    yield

    n_chunks = tm // CHUNK
    ends = [cum[(c + 1) * CHUNK - 1:(c + 1) * CHUNK, :] for c in range(n_chunks)]
    cl = jnp.concatenate([jnp.broadcast_to(e, (CHUNK, RWKV_W)) for e in ends], axis=0)
    for c in range(n_chunks):
        wl_out[0, c0 + c] = jnp.exp2(ends[c])
    bonus_out[0, rs, :] = bonus_ss * v
    yield
    e_neg = jnp.exp2(-cum)
    bt_out[0, rs, :] = (bv * e_neg).astype(BF16)
    kt_out[0, rs, :] = (k * e_neg).astype(BF16)
    yield
    e_hat = jnp.exp2(cl - cum)
    bh_out[0, rs, :] = (bv * e_hat).astype(BF16)
    kh_out[0, rs, :] = (k * e_hat).astype(BF16)
    yield
    at_out[0, rs, :] = (-kk * jnp.exp2(cum - lw)).astype(BF16)
    rt_out[0, rs, :] = (r * jnp.exp2(cum)).astype(BF16)


def _in_projection(x, mod3, g1, w_in, mu, w0, wlu, a0, alu, glu, k_k, k_a, r_k, qg, kg, seg, tri):
    B, S, D = x.shape
    tm = min(IN_TILE, S)
    const = lambda shape: pl.BlockSpec(shape, lambda b, j: (0,) * len(shape))
    row_spec = lambda w: pl.BlockSpec((1, tm, w), lambda b, j: (b, j, 0))
    rows = lambda w, dt: jax.ShapeDtypeStruct((B, S, w), dt)
    return pl.pallas_call(
        _inproj_kernel,
        grid=(B, S // tm),
        in_specs=[row_spec(D),
                  pl.BlockSpec((1, N_MOD, D), lambda b, j: (b, 0, 0)),
                  const(g1.shape), const(w_in.shape), const(mu.shape), const(w0.shape),
                  const(wlu.shape), const(a0.shape), const(alu.shape), const(glu.shape),
                  const(k_k.shape), const(k_a.shape), const(r_k.shape), const(qg.shape),
                  const(kg.shape), const(seg.shape), const(tri.shape)],
        out_specs=([row_spec(RWKV_W)] * 7
                   + [pl.BlockSpec((1, tm // CHUNK, 1, RWKV_W), lambda b, j: (b, j, 0, 0))]
                   + [row_spec(RWKV_W)] * 2
                   + [row_spec(ATT_Q_W), row_spec(ATT_KV_W), row_spec(ATT_KV_W)]),
        out_shape=([rows(RWKV_W, BF16)] * 7
                   + [jax.ShapeDtypeStruct((B, S // CHUNK, 1, RWKV_W), F32)]
                   + [rows(RWKV_W, F32)] * 2
                   + [rows(ATT_Q_W, BF16), rows(ATT_KV_W, BF16), rows(ATT_KV_W, BF16)]),
        scratch_shapes=[pltpu.VMEM((1, RWKV_COLS), F32)],
        compiler_params=pltpu.CompilerParams(dimension_semantics=("arbitrary", "arbitrary"),
                                             vmem_limit_bytes=VMEM_LIMIT),
        name="in_projection",
    )(x, mod3, g1, w_in, mu, w0, wlu, a0, alu, glu, k_k, k_a, r_k, qg, kg, seg, tri)


def _rwkv_kernel(at_ref, bt_ref, kt_ref, rt_ref, bh_ref, kh_ref, v_ref, wl_ref, bonus_ref, g_ref,
                 lg_ref, lb_ref, seg_ref, o_ref, state_ref):
    @pl.when(pl.program_id(1) == 0)
    def _():
        state_ref[...] = jnp.zeros_like(state_ref)

    L, D, P = CHUNK, HEAD_DIM, 2 * HEAD_DIM
    n_chunks = at_ref.shape[1] // L
    n_pairs = RWKV_W // P
    row = lax.broadcasted_iota(jnp.int32, (L, P), 0)
    lane = lax.broadcasted_iota(jnp.int32, (L, P), 1)
    low = lane < D
    s_idx = jnp.where(low, lane, lane - D)
    strict = row > s_idx
    incl = row >= s_idx
    zeros_lp = jnp.zeros((L, P), F32)
    own = (lambda t: jnp.where(low, t, jnp.zeros_like(t)), lambda t: jnp.where(low, jnp.zeros_like(t), t))
    other = (own[1], own[0])
    diag = (lane == row, lane == row + D)
    seg = seg_ref[...]

    mid = {}

    def front(g):
        chunks = range(g * RWKV_GROUP, (g + 1) * RWKV_GROUP)
        items = [(c, j, par) for c in chunks for j in range(n_pairs) for par in range(2)]
        chunk_in, top, bot, v_o, x = {}, {}, {}, {}, {}
        for c in chunks:
            rows = pl.ds(c * L, L)
            ci = dict(a_t=at_ref[0, rows, :], b_t=bt_ref[0, rows, :], k_t=kt_ref[0, rows, :],
                      r_t=rt_ref[0, rows, :], b_h=bh_ref[0, rows, :], k_h=kh_ref[0, rows, :],
                      v=v_ref[0, rows, :], w_l=wl_ref[0, c])
            chunk_in[c] = ci
            for j in range(n_pairs):
                ps = slice(j * P, (j + 1) * P)
                atp, rtp = ci["a_t"][:, ps], ci["r_t"][:, ps]
                btp, ktp = ci["b_t"][:, ps], ci["k_t"][:, ps]
                s1 = _dot_nt(jnp.concatenate([atp, rtp], axis=0),
                             jnp.concatenate([own[0](btp), own[0](ktp), own[1](btp), own[1](ktp)], axis=0))
                v_swap = pltpu.roll(ci["v"][:, ps].astype(F32), D, 1)
                for par in range(2):
                    it = (c, j, par)
                    top[it] = jnp.where(strict, s1[:L, P * par:P * (par + 1)], 0.0)
                    bot[it] = jnp.where(incl, s1[L:, P * par:P * (par + 1)], 0.0)
                    v_o[it] = other[par](v_swap)
                    x[it] = own[par](atp).astype(F32)
        yield
        for it in items:
            x[it] = x[it] + _dot(top[it], jnp.concatenate([zeros_lp, v_o[it]], axis=0))
        yield
        a_pow = {it: top[it][:, :D].astype(BF16) for it in items}
        n = 1
        while n < L:
            last = 2 * n >= L
            for it in items:
                xb = x[it].astype(BF16)
                if last:
                    x[it] = x[it] + jnp.dot(a_pow[it], xb, preferred_element_type=F32)
                else:
                    res = jnp.dot(a_pow[it], jnp.concatenate([xb, a_pow[it]], axis=1),
                                  preferred_element_type=F32)
                    x[it] = x[it] + res[:, :P]
                    a_pow[it] = res[:, P:].astype(BF16)
            n *= 2
            mid[g] = (chunk_in, bot, v_o, x)
            yield

    def back(g):
        chunk_in, bot, v_o, x = mid.pop(g)
        chunks = range(g * RWKV_GROUP, (g + 1) * RWKV_GROUP)
        items = [(c, j, par) for c in chunks for j in range(n_pairs) for par in range(2)]
        rhs2 = {it: jnp.concatenate([x[it], v_o[it]], axis=0).astype(BF16) for it in items}
        rb = {it: jnp.dot(bot[it].astype(BF16), rhs2[it], preferred_element_type=F32) for it in items}
        mn = {}
        for c in chunks:
            ci = chunk_in[c]
            for j in range(n_pairs):
                ps = slice(j * P, (j + 1) * P)
                res = _dot_tn(jnp.concatenate([ci["b_h"][:, ps], ci["k_h"][:, ps]], axis=0),
                              jnp.concatenate([rhs2[(c, j, 0)], rhs2[(c, j, 1)]], axis=1))
                for par in range(2):
                    mn[(c, j, par)] = res[D * par:D * (par + 1), P * par:P * (par + 1)]
        yield
        y = {}
        for c in chunks:
            ci = chunk_in[c]
            for j in range(n_pairs):
                ps = slice(j * P, (j + 1) * P)
                for par in range(2):
                    it = (c, j, par)
                    h = 2 * j + par
                    g_p = rb[it] + ci["r_t"][:, ps]
                    m_p = mn[it] + jnp.where(diag[par], ci["w_l"][:, ps], 0.0)
                    z = state_ref[h]
                    rhs = jnp.concatenate([z, zeros_lp] if par == 0 else [zeros_lp, z], axis=0)
                    res = _dot(jnp.concatenate([g_p, m_p], axis=0), rhs)
                    y[it] = res[:L] + rb[it]
                    state_ref[h] = other[par](res[L:] + mn[it])
            yield
        for c in chunks:
            rows = pl.ds(c * L, L)
            yc = jnp.concatenate(
                [pltpu.roll(jnp.where(low, y[c, j, 1], y[c, j, 0]), D, 1) for j in range(n_pairs)], axis=1)
            mu = _seg_sum(yc, seg) * (1.0 / HEAD_DIM)
            yc = yc - mu
            var = _seg_sum(yc * yc, seg) * (1.0 / HEAD_DIM)
            yn = yc * lax.rsqrt(var + RWKV_GN_EPS) * lg_ref[...] + lb_ref[...]
            o_ref[0, rows, :] = ((yn + bonus_ref[0, rows, :]) * g_ref[0, rows, :]).astype(BF16)
            yield

    n_groups = n_chunks // RWKV_GROUP
    prev = iter(())
    for g in range(n_groups):
        cur = front(g)
        for _ in cur:
            next(prev, None)
        for _ in prev:
            pass
        prev = back(g)
    for _ in prev:
        pass


def _rwkv(at, bt, kt, rt, bh, kh, v, wl, bonus, g, lnx_g, lnx_b, seg):
    B, S, W = at.shape
    tc = min(RWKV_TILE, S)
    const = lambda shape: pl.BlockSpec(shape, lambda b, j: (0,) * len(shape))
    row_spec = pl.BlockSpec((1, tc, W), lambda b, j: (b, j, 0))
    return pl.pallas_call(
        _rwkv_kernel,
        grid=(B, S // tc),
        in_specs=([row_spec] * 7 + [pl.BlockSpec((1, tc // CHUNK, 1, W), lambda b, j: (b, j, 0, 0))]
                  + [row_spec] * 2 + [const(lnx_g.shape), const(lnx_b.shape), const(seg.shape)]),
        out_specs=row_spec,
        out_shape=jax.ShapeDtypeStruct((B, S, W), BF16),
        scratch_shapes=[pltpu.VMEM((RWKV_HEADS, CHUNK, 2 * HEAD_DIM), F32)],
        compiler_params=pltpu.CompilerParams(dimension_semantics=("arbitrary", "arbitrary"),
                                             vmem_limit_bytes=VMEM_LIMIT),
        name="rwkv7_chunked",
    )(at, bt, kt, rt, bh, kh, v, wl, bonus, g, lnx_g, lnx_b, seg)


def _swa_kernel(q_ref, kc_ref, kp_ref, vc_ref, vp_ref, bias_ref, sink_ref, o_ref):
    first = pl.program_id(1) == 0
    rows = GQA_GROUP * ATT_BLOCK
    col = lax.broadcasted_iota(jnp.int32, (rows, ATT_BLOCK), 1)
    qpos = lax.broadcasted_iota(jnp.int32, (rows, ATT_BLOCK), 0) % ATT_BLOCK
    use_cur = col <= qpos
    no_key = jnp.logical_and(first, jnp.logical_not(use_cur))
    row = lax.broadcasted_iota(jnp.int32, (rows, 1), 0)
    sink = []
    for hk in range(N_KV_HEADS):
        s = jnp.full((rows, 1), sink_ref[hk * GQA_GROUP], F32)
        for g in range(1, GQA_GROUP):
            s = jnp.where(row >= g * ATT_BLOCK, sink_ref[hk * GQA_GROUP + g], s)
        sink.append(s)

    items = [(blk, hk) for blk in range(q_ref.shape[1] // ATT_BLOCK) for hk in range(N_KV_HEADS)]
    logits, vw = {}, {}
    for blk, hk in items:
        ks = slice(hk * HEAD_DIM, (hk + 1) * HEAD_DIM)
        rs = pl.ds(blk * ATT_BLOCK, ATT_BLOCK)
        if blk == 0:
            k_prev, v_prev = kp_ref[0, :, ks], vp_ref[0, :, ks]
        else:
            ps = pl.ds((blk - 1) * ATT_BLOCK, ATT_BLOCK)
            k_prev, v_prev = kc_ref[0, ps, ks], vc_ref[0, ps, ks]
        kw = jnp.concatenate([k_prev, kc_ref[0, rs, ks]], axis=0)
        v2 = jnp.concatenate([v_prev, vc_ref[0, rs, ks]], axis=0)
        vw[blk, hk] = jnp.concatenate([v2, v2, jnp.ones((2 * ATT_BLOCK, 2 * HEAD_DIM), BF16)], axis=1)
        q = q_ref[0, rs, :]
        qg = jnp.concatenate([q[:, (hk * GQA_GROUP + g) * HEAD_DIM:(hk * GQA_GROUP + g + 1) * HEAD_DIM]
                              for g in range(GQA_GROUP)], axis=0)
        lg = lax.dot_general(qg, kw, (((1,), (1,)), ((), ())), preferred_element_type=F32)
        lg = jnp.where(use_cur, lg[:, ATT_BLOCK:], lg[:, :ATT_BLOCK]) + bias_ref[hk]
        logits[blk, hk] = jnp.where(no_key, NEG_INF, lg) if blk == 0 else lg
    m = {it: jnp.maximum(jnp.max(logits[it], axis=-1, keepdims=True), sink[it[1]]) for it in items}
    e = {it: jnp.exp(logits[it] - m[it]) for it in items}
    e_sink = {it: jnp.exp(sink[it[1]] - m[it]) for it in items}
    low = lax.broadcasted_iota(jnp.int32, (ATT_BLOCK, 2 * HEAD_DIM), 1) < HEAD_DIM
    for blk, hk in items:
        eb = e[blk, hk].astype(BF16)
        zero = jnp.zeros_like(eb)
        e2 = jnp.concatenate([jnp.where(use_cur, zero, eb), jnp.where(use_cur, eb, zero)], axis=1)
        res = jnp.dot(e2, vw[blk, hk], preferred_element_type=F32)
        out = res[:, :2 * HEAD_DIM] / (res[:, 2 * HEAD_DIM:] + e_sink[blk, hk])
        for g in range(0, GQA_GROUP, 2):
            h = hk * GQA_GROUP + g
            pair = jnp.where(low, out[g * ATT_BLOCK:(g + 1) * ATT_BLOCK], out[(g + 1) * ATT_BLOCK:(g + 2) * ATT_BLOCK])
            o_ref[0, pl.ds(blk * ATT_BLOCK, ATT_BLOCK), h * HEAD_DIM:(h + 2) * HEAD_DIM] = pair.astype(BF16)


def _swa(q, ak, av, bias, sinks):
    B, S, _ = q.shape
    tq = min(SWA_TILE, S)
    nb = tq // ATT_BLOCK
    bias = bias.reshape(N_KV_HEADS, GQA_GROUP * ATT_BLOCK, ATT_BLOCK)
    cur = lambda w: pl.BlockSpec((1, tq, w), lambda b, i: (b, i, 0))
    prev = lambda w: pl.BlockSpec((1, ATT_BLOCK, w), lambda b, i: (b, jnp.maximum(i * nb - 1, 0), 0))
    return pl.pallas_call(
        _swa_kernel,
        grid=(B, S // tq),
        in_specs=[cur(ATT_Q_W), cur(ATT_KV_W), prev(ATT_KV_W), cur(ATT_KV_W), prev(ATT_KV_W),
                  pl.BlockSpec(bias.shape, lambda b, i: (0, 0, 0)),
                  pl.BlockSpec(memory_space=pltpu.SMEM)],
        out_specs=cur(ATT_Q_W),
        out_shape=jax.ShapeDtypeStruct((B, S, ATT_Q_W), BF16),
        compiler_params=pltpu.CompilerParams(dimension_semantics=("arbitrary", "arbitrary")),
        name="swa_attention",
    )(q, ak, ak, av, av, bias, sinks)


def _ffn_kernel(x_ref, yr_ref, ya_ref, mod_ref, g2_ref, wo_ref, wg_ref, wu_ref, wd_ref, o_ref):
    x = x_ref[0]
    wo = wo_ref[...]
    mix = (jnp.dot(yr_ref[0], wo[:RWKV_W], preferred_element_type=F32)
           + jnp.dot(ya_ref[0], wo[RWKV_W:], preferred_element_type=F32))
    h_res = x + mod_ref[0, 2:3, :] * mix
    ms = jnp.mean(h_res * h_res, axis=-1, keepdims=True)
    h2 = (h_res * lax.rsqrt(ms + NORM_EPS) * g2_ref[...] * (1.0 + mod_ref[0, 4:5, :])
          + mod_ref[0, 3:4, :]).astype(BF16)
    d_ff = wg_ref.shape[1]
    ffn = jnp.zeros_like(x)
    for lo in range(0, d_ff, FFN_BLOCK):
        cs = slice(lo, min(lo + FFN_BLOCK, d_ff))
        gt = jnp.dot(h2, wg_ref[:, cs], preferred_element_type=F32)
        up = jnp.dot(h2, wu_ref[:, cs], preferred_element_type=F32)
        act = (gt * _sigmoid(gt) * up).astype(BF16)
        ffn = ffn + jnp.dot(act, wd_ref[cs, :], preferred_element_type=F32)
    o_ref[0] = h_res + mod_ref[0, 5:6, :] * ffn


def _out_ffn(x, y_rwkv, y_att, mod3, g2, w_out, w_gate, w_up, w_down):
    B, S, D = x.shape
    tm = min(FFN_TILE, S)
    resident = lambda shape: pl.BlockSpec(shape, lambda b, j: (0,) * len(shape),
                                          pipeline_mode=pl.Buffered(1))
    row_spec = lambda w: pl.BlockSpec((1, tm, w), lambda b, j: (b, j, 0))
    return pl.pallas_call(
        _ffn_kernel,
        grid=(B, S // tm),
        in_specs=[row_spec(D), row_spec(RWKV_W), row_spec(ATT_Q_W),
                  pl.BlockSpec((1, N_MOD, D), lambda b, j: (b, 0, 0)),
                  resident(g2.shape), resident(w_out.shape), resident(w_gate.shape),
                  resident(w_up.shape), resident(w_down.shape)],
        out_specs=row_spec(D),
        out_shape=jax.ShapeDtypeStruct((B, S, D), F32),
        compiler_params=pltpu.CompilerParams(dimension_semantics=("arbitrary", "arbitrary"),
                                             vmem_limit_bytes=VMEM_LIMIT),
        name="out_proj_ffn",
    )(x, y_rwkv, y_att, mod3, g2, w_out, w_gate, w_up, w_down)


def kernel(x, c, w_ada, b_ada, norm1_g, w_in, rwkv_mu, w0, w_lora_up, a0, a_lora_up, g_lora_up, k_k, k_a, r_k, lnx_g, lnx_b, q_norm_g, k_norm_g, sinks, rel_bias, w_out, norm2_g, w_gate, w_up, w_down):
    B, S, D = x.shape
    row = lambda t: t.reshape(1, -1).astype(F32)
    lane = jnp.arange(MXU_TILE) // HEAD_DIM
    seg = (lane[:, None] == lane[None, :]).astype(BF16)
    t_idx = jnp.arange(min(IN_SUB, S))
    tri = ((t_idx[:, None] >= t_idx[None, :])
           & (t_idx[:, None] // CHUNK == t_idx[None, :] // CHUNK)).astype(BF16)

    mod3 = _modulation(c.astype(F32), w_ada, b_ada).reshape(B, N_MOD, D)
    bias = _bias_table(rel_bias)

    at, bt, kt, rt, bh, kh, v, wl, bonus, g, q, ak, av = _in_projection(
        x, mod3, row(norm1_g), w_in.astype(BF16), row(rwkv_mu), row(w0), w_lora_up.astype(BF16),
        row(a0), a_lora_up.astype(BF16), g_lora_up.astype(BF16), row(k_k), row(k_a), row(r_k),
        row(jnp.tile(q_norm_g, N_Q_HEADS)), row(jnp.tile(k_norm_g, N_KV_HEADS)), seg, tri)

    y_rwkv = _rwkv(at, bt, kt, rt, bh, kh, v, wl, bonus, g, row(lnx_g), row(lnx_b), seg)
    y_att = _swa(q, ak, av, bias, sinks)

    out = _out_ffn(x, y_rwkv, y_att, mod3, row(norm2_g), w_out.astype(BF16),
                   w_gate.astype(BF16), w_up.astype(BF16), w_down.astype(BF16))
    return out.astype(x.dtype)
```

```python
import functools
import math

import jax
import jax.numpy as jnp
from jax import lax
from jax.experimental import pallas as pl
from jax.experimental.pallas import tpu as pltpu

F32 = jnp.float32
BF16 = jnp.bfloat16

HEAD_DIM = 64
RWKV_HEADS = 8
RWKV_W = RWKV_HEADS * HEAD_DIM
DECAY_LORA = 64
AAA_LORA = 64
GATE_LORA = 128
RWKV_COLS = 3 * RWKV_W + DECAY_LORA + AAA_LORA + GATE_LORA
RWKV_GN_EPS = 64e-5
N_Q_HEADS = 8
N_KV_HEADS = 2
GQA_GROUP = N_Q_HEADS // N_KV_HEADS
ATT_Q_W = N_Q_HEADS * HEAD_DIM
ATT_KV_W = N_KV_HEADS * HEAD_DIM
ATT_BLOCK = 128
WINDOW = 128
ATT_SCALE = 1.0 / math.sqrt(HEAD_DIM)
NUM_BUCKETS = 32
MAX_DISTANCE = 128
N_MOD = 6
NORM_EPS = 1e-6
NEG_INF = -1e30

CHUNK = 64
IN_TILE = 1024
IN_SUB = 256
MIX_TILE = 512
RWKV_GROUP = 8
FFN_TILE = 512
MXU_TILE = 256
F32_SUBLANES = 8
FFN_BLOCK = 4 * MXU_TILE
VMEM_LIMIT = 56 * 1024 * 1024


def _dot(a, b):
    return jnp.dot(a.astype(BF16), b.astype(BF16), preferred_element_type=F32)


def _dot_nt(a, b):
    return lax.dot_general(a.astype(BF16), b.astype(BF16), (((1,), (1,)), ((), ())),
                           preferred_element_type=F32)


def _dot_tn(a, b):
    return lax.dot_general(a.astype(BF16), b.astype(BF16), (((0,), (0,)), ((), ())),
                           preferred_element_type=F32)


def _split(x):
    hi = x.astype(BF16)
    lo = (x - hi.astype(F32)).astype(BF16)
    return hi, lo


def _seg_sum(x, seg):
    m, w = x.shape
    xb = x.astype(BF16)
    blk = seg.shape[0]
    if w <= blk:
        return jnp.dot(xb, seg[:w, :w], preferred_element_type=F32)
    n = w // blk
    stacked = jnp.concatenate([xb[:, i * blk:(i + 1) * blk] for i in range(n)], axis=0)
    res = jnp.dot(stacked, seg, preferred_element_type=F32)
    return jnp.concatenate([res[i * m:(i + 1) * m] for i in range(n)], axis=1)


def _dot_lhs_exact(m, x):
    hi, lo = _split(x)
    return (jnp.dot(m, hi, preferred_element_type=F32) + jnp.dot(m, lo, preferred_element_type=F32))


def _sigmoid(z):
    return 0.5 * jnp.tanh(0.5 * z) + 0.5


def _mod_kernel(c_ref, w_ref, b_ref, o_ref):
    c = c_ref[...]
    s = c * _sigmoid(c)
    s_hi, s_lo = _split(s)
    w = w_ref[...]
    w_hi, w_lo = _split(w)
    d = lambda x, y: jnp.dot(x, y, preferred_element_type=F32)
    o_ref[...] = d(s_hi, w_hi) + d(s_hi, w_lo) + d(s_lo, w_hi) + b_ref[...]


def _modulation(c, w_ada, b_ada):
    B, D = c.shape
    n = w_ada.shape[1]
    blk = D
    return pl.pallas_call(
        _mod_kernel,
        grid=(n // blk,),
        in_specs=[pl.BlockSpec((B, D), lambda j: (0, 0)),
                  pl.BlockSpec((D, blk), lambda j: (0, j)),
                  pl.BlockSpec((1, blk), lambda j: (0, j))],
        out_specs=pl.BlockSpec((B, blk), lambda j: (0, j)),
        out_shape=jax.ShapeDtypeStruct((B, n), F32),
        name="adaln_mod",
    )(c, w_ada, b_ada.reshape(1, n))


def _t5_thresholds():
    max_exact = NUM_BUCKETS // 2
    out = []
    for k in range(1, NUM_BUCKETS - max_exact):
        edge = max_exact * (MAX_DISTANCE / max_exact) ** (k / (NUM_BUCKETS - max_exact))
        assert abs(edge - round(edge)) > 1e-6, "bucket edge on an integer distance: floor() would be ambiguous"
        out.append(math.ceil(edge))
    return tuple(out)


_T5_THRESHOLDS = _t5_thresholds()


def _bias_kernel(rb_ref, o_ref):
    qi = lax.broadcasted_iota(jnp.int32, (ATT_BLOCK, ATT_BLOCK), 0)
    kj = lax.broadcasted_iota(jnp.int32, (ATT_BLOCK, ATT_BLOCK), 1)
    n = jnp.where(kj <= qi, qi - kj, qi + ATT_BLOCK - kj)
    max_exact = NUM_BUCKETS // 2
    large = jnp.full(n.shape, max_exact, jnp.int32)
    for t in _T5_THRESHOLDS:
        large = large + jnp.where(n >= t, 1, 0)
    bucket = jnp.where(n < max_exact, n, large)
    for h in range(N_Q_HEADS):
        acc = jnp.zeros((ATT_BLOCK, ATT_BLOCK), F32)
        for b in range(NUM_BUCKETS):
            acc = jnp.where(bucket == b, rb_ref[b, h], acc)
        o_ref[h] = acc


def _bias_table(rel_bias):
    assert WINDOW == ATT_BLOCK
    return pl.pallas_call(
        _bias_kernel,
        in_specs=[pl.BlockSpec(memory_space=pltpu.SMEM)],
        out_specs=pl.BlockSpec(memory_space=pltpu.VMEM),
        out_shape=jax.ShapeDtypeStruct((N_Q_HEADS, ATT_BLOCK, ATT_BLOCK), F32),
        name="rel_bias_table",
    )(rel_bias)


def _inproj_kernel(x_ref, mod_ref, g1_ref, win_ref, mu_ref, w0_ref, wlu_ref, a0_ref, alu_ref,
                   glu_ref, kk_ref, ka_ref, rk_ref, qg_ref, kg_ref, seg_ref, tri_ref,
                   at_out, bt_out, kt_out, rt_out, bh_out, kh_out, v_out, wl_out, bonus_out, g_out,
                   q_out, ak_out, av_out, carry_ref):
    @pl.when(pl.program_id(1) == 0)
    def _():
        carry_ref[...] = jnp.zeros_like(carry_ref)

    tm = tri_ref.shape[0]
    refs = (mu_ref, w0_ref, wlu_ref, a0_ref, alu_ref, glu_ref, kk_ref, ka_ref, rk_ref, qg_ref, kg_ref,
            seg_ref, tri_ref)
    outs = (at_out, bt_out, kt_out, rt_out, bh_out, kh_out, v_out, wl_out, bonus_out, g_out, q_out,
            ak_out, av_out)
    tail = iter(())
    gain = g1_ref[...] * (1.0 + mod_ref[0, 1:2, :])
    for s in range(x_ref.shape[1] // tm):
        rs = pl.ds(s * tm, tm)
        x = x_ref[0, rs, :]
        ms = jnp.mean(x * x, axis=-1, keepdims=True)
        h = (x * lax.rsqrt(ms + NORM_EPS) * gain + mod_ref[0, 0:1, :]).astype(BF16)
        pieces = []
        for lo in range(0, win_ref.shape[1], MXU_TILE):
            pieces.append(jnp.dot(h, win_ref[:, lo:lo + MXU_TILE], preferred_element_type=F32))
            next(tail, None)
        for _ in tail:
            pass
        tail = _inproj_tail(jnp.concatenate(pieces, axis=1), rs, s * (tm // CHUNK), refs, outs, carry_ref)
    for _ in tail:
        pass


def _inproj_tail(p, rs, c0, refs, outs, carry_ref):
    (mu_ref, w0_ref, wlu_ref, a0_ref, alu_ref, glu_ref, kk_ref, ka_ref, rk_ref, qg_ref, kg_ref,
     seg_ref, tri_ref) = refs
    (at_out, bt_out, kt_out, rt_out, bh_out, kh_out, v_out, wl_out, bonus_out, g_out, q_out,
     ak_out, av_out) = outs
    tm = p.shape[0]

    mm = lambda a, b: jnp.dot(a, b, preferred_element_type=F32)

    pr = p[:, :RWKV_COLS]
    rolled = pltpu.roll(pr, 1, 0)
    first = lax.broadcasted_iota(jnp.int32, (F32_SUBLANES, RWKV_COLS), 0) == 0
    prev = jnp.concatenate([jnp.where(first, carry_ref[...], rolled[:F32_SUBLANES]), rolled[F32_SUBLANES:]],
                           axis=0)
    carry_ref[...] = pr[tm - 1:tm, :]
    pm = pr + (prev - pr) * mu_ref[...]
    r = pm[:, 0:RWKV_W]
    k = pm[:, RWKV_W:2 * RWKV_W]
    v = pm[:, 2 * RWKV_W:3 * RWKV_W]
    o = 3 * RWKV_W
    tanh_w = jnp.tanh(pm[:, o:o + DECAY_LORA]).astype(BF16)
    xa = pm[:, o + DECAY_LORA:o + DECAY_LORA + AAA_LORA].astype(BF16)
    sig_g = _sigmoid(pm[:, o + DECAY_LORA + AAA_LORA:RWKV_COLS]).astype(BF16)
    yield
    q = p[:, RWKV_COLS:RWKV_COLS + ATT_Q_W]
    ak = p[:, RWKV_COLS + ATT_Q_W:RWKV_COLS + ATT_Q_W + ATT_KV_W]
    av = p[:, RWKV_COLS + ATT_Q_W + ATT_KV_W:]
    kk = k * kk_ref[...]
    kk_sq, q_sq, ak_sq = kk * kk, q * q, ak * ak
    v_out[0, rs, :] = v.astype(BF16)
    av_out[0, rs, :] = av.astype(BF16)
    yield

    seg = seg_ref[...]
    z_mm = mm(tanh_w, wlu_ref[...])
    a_mm = mm(xa, alu_ref[...])
    g_mm = mm(sig_g, glu_ref[...])
    kk_ss = _seg_sum(kk_sq, seg)
    q_ss = _seg_sum(q_sq, seg)
    ak_ss = _seg_sum(ak_sq, seg)
    yield

    lw = (-math.exp(-0.5) * math.log2(math.e)) * _sigmoid(w0_ref[...] + z_mm)
    lw_hi, lw_lo = _split(lw)
    a = _sigmoid(a0_ref[...] + a_mm)
    g_out[0, rs, :] = g_mm
    kk = kk * lax.rsqrt(jnp.maximum(kk_ss, 1e-24))
    bv = kk * a
    k = k * (1.0 + (a - 1.0) * ka_ref[...])
    rk_prod = r * k * rk_ref[...]
    yield
    q_out[0, rs, :] = (q * lax.rsqrt(q_ss * (1.0 / HEAD_DIM) + NORM_EPS) * (qg_ref[...] * ATT_SCALE)).astype(BF16)
    ak_out[0, rs, :] = (ak * lax.rsqrt(ak_ss * (1.0 / HEAD_DIM) + NORM_EPS) * kg_ref[...]).astype(BF16)
    yield

    tri = tri_ref[...]
    cum = mm(tri, lw_hi) + mm(tri, lw_lo)
    bonus_ss = _seg_sum(rk_prod, seg)
    yield

    n_chunks = tm // CHUNK
    ends = [cum[(c + 1) * CHUNK - 1:(c + 1) * CHUNK, :] for c in range(n_chunks)]
    cl = jnp.concatenate([jnp.broadcast_to(e, (CHUNK, RWKV_W)) for e in ends], axis=0)
    for c in range(n_chunks):
        wl_out[0, c0 + c] = jnp.exp2(ends[c])
    bonus_out[0, rs, :] = bonus_ss * v
    yield
    e_neg = jnp.exp2(-cum)
    bt_out[0, rs, :] = (bv * e_neg).astype(BF16)
    kt_out[0, rs, :] = (k * e_neg).astype(BF16)
    yield
    e_hat = jnp.exp2(cl - cum)
    bh_out[0, rs, :] = (bv * e_hat).astype(BF16)
    kh_out[0, rs, :] = (k * e_hat).astype(BF16)
    yield
    at_out[0, rs, :] = (-kk * jnp.exp2(cum - lw)).astype(BF16)
    rt_out[0, rs, :] = (r * jnp.exp2(cum)).astype(BF16)


def _in_projection(x, mod3, g1, w_in, mu, w0, wlu, a0, alu, glu, k_k, k_a, r_k, qg, kg, seg, tri):
    B, S, D = x.shape
    tm = min(IN_TILE, S)
    const = lambda shape: pl.BlockSpec(shape, lambda b, j: (0,) * len(shape))
    row_spec = lambda w: pl.BlockSpec((1, tm, w), lambda b, j: (b, j, 0))
    rows = lambda w, dt: jax.ShapeDtypeStruct((B, S, w), dt)
    return pl.pallas_call(
        _inproj_kernel,
        grid=(B, S // tm),
        in_specs=[row_spec(D),
                  pl.BlockSpec((1, N_MOD, D), lambda b, j: (b, 0, 0)),
                  const(g1.shape), const(w_in.shape), const(mu.shape), const(w0.shape),
                  const(wlu.shape), const(a0.shape), const(alu.shape), const(glu.shape),
                  const(k_k.shape), const(k_a.shape), const(r_k.shape), const(qg.shape),
                  const(kg.shape), const(seg.shape), const(tri.shape)],
        out_specs=([row_spec(RWKV_W)] * 7
                   + [pl.BlockSpec((1, tm // CHUNK, 1, RWKV_W), lambda b, j: (b, j, 0, 0))]
                   + [row_spec(RWKV_W)] * 2
                   + [row_spec(ATT_Q_W), row_spec(ATT_KV_W), row_spec(ATT_KV_W)]),
        out_shape=([rows(RWKV_W, BF16)] * 7
                   + [jax.ShapeDtypeStruct((B, S // CHUNK, 1, RWKV_W), F32)]
                   + [rows(RWKV_W, F32)] * 2
                   + [rows(ATT_Q_W, BF16), rows(ATT_KV_W, BF16), rows(ATT_KV_W, BF16)]),
        scratch_shapes=[pltpu.VMEM((1, RWKV_COLS), F32)],
        compiler_params=pltpu.CompilerParams(dimension_semantics=("arbitrary", "arbitrary"),
                                             vmem_limit_bytes=VMEM_LIMIT),
        name="in_projection",
    )(x, mod3, g1, w_in, mu, w0, wlu, a0, alu, glu, k_k, k_a, r_k, qg, kg, seg, tri)


def _mixer_kernel(at_ref, bt_ref, kt_ref, rt_ref, bh_ref, kh_ref, v_ref, wl_ref, bonus_ref, g_ref,
                  lg_ref, lb_ref, seg_ref, q_ref, kc_ref, kp_ref, vc_ref, vp_ref, bias_ref, sink_ref,
                  o_ref, oatt_ref, state_ref):
    @pl.when(pl.program_id(1) == 0)
    def _():
        state_ref[...] = jnp.zeros_like(state_ref)

    L, D, P = CHUNK, HEAD_DIM, 2 * HEAD_DIM
    n_chunks = at_ref.shape[1] // L
    n_pairs = RWKV_W // P
    row = lax.broadcasted_iota(jnp.int32, (L, P), 0)
    lane = lax.broadcasted_iota(jnp.int32, (L, P), 1)
    low = lane < D
    s_idx = jnp.where(low, lane, lane - D)
    strict = row > s_idx
    incl = row >= s_idx
    zeros_lp = jnp.zeros((L, P), F32)
    own = (lambda t: jnp.where(low, t, jnp.zeros_like(t)), lambda t: jnp.where(low, jnp.zeros_like(t), t))
    other = (own[1], own[0])
    diag = (lane == row, lane == row + D)
    seg = seg_ref[...]

    mid = {}

    def front(g):
        chunks = range(g * RWKV_GROUP, (g + 1) * RWKV_GROUP)
        items = [(c, j, par) for c in chunks for j in range(n_pairs) for par in range(2)]
        chunk_in, top, bot, v_o, x = {}, {}, {}, {}, {}
        for c in chunks:
            rows = pl.ds(c * L, L)
            ci = dict(a_t=at_ref[0, rows, :], b_t=bt_ref[0, rows, :], k_t=kt_ref[0, rows, :],
                      r_t=rt_ref[0, rows, :], b_h=bh_ref[0, rows, :], k_h=kh_ref[0, rows, :],
                      v=v_ref[0, rows, :], w_l=wl_ref[0, c])
            chunk_in[c] = ci
            for j in range(n_pairs):
                ps = slice(j * P, (j + 1) * P)
                atp, rtp = ci["a_t"][:, ps], ci["r_t"][:, ps]
                btp, ktp = ci["b_t"][:, ps], ci["k_t"][:, ps]
                s1 = _dot_nt(jnp.concatenate([atp, rtp], axis=0),
                             jnp.concatenate([own[0](btp), own[0](ktp), own[1](btp), own[1](ktp)], axis=0))
                v_swap = pltpu.roll(ci["v"][:, ps].astype(F32), D, 1)
                for par in range(2):
                    it = (c, j, par)
                    top[it] = jnp.where(strict, s1[:L, P * par:P * (par + 1)], 0.0)
                    bot[it] = jnp.where(incl, s1[L:, P * par:P * (par + 1)], 0.0)
                    v_o[it] = other[par](v_swap)
                    x[it] = own[par](atp).astype(F32)
        yield
        for it in items:
            x[it] = x[it] + _dot(top[it], jnp.concatenate([zeros_lp, v_o[it]], axis=0))
        yield
        a_pow = {it: top[it][:, :D].astype(BF16) for it in items}
        n = 1
        while n < L:
            last = 2 * n >= L
            for it in items:
                xb = x[it].astype(BF16)
                if last:
                    x[it] = x[it] + jnp.dot(a_pow[it], xb, preferred_element_type=F32)
                else:
                    res = jnp.dot(a_pow[it], jnp.concatenate([xb, a_pow[it]], axis=1),
                                  preferred_element_type=F32)
                    x[it] = x[it] + res[:, :P]
                    a_pow[it] = res[:, P:].astype(BF16)
            n *= 2
            mid[g] = (chunk_in, bot, v_o, x)
            yield

    def back(g):
        chunk_in, bot, v_o, x = mid.pop(g)
        chunks = range(g * RWKV_GROUP, (g + 1) * RWKV_GROUP)
        items = [(c, j, par) for c in chunks for j in range(n_pairs) for par in range(2)]
        rhs2 = {it: jnp.concatenate([x[it], v_o[it]], axis=0).astype(BF16) for it in items}
        rb = {it: jnp.dot(bot[it].astype(BF16), rhs2[it], preferred_element_type=F32) for it in items}
        mn = {}
        for c in chunks:
            ci = chunk_in[c]
            for j in range(n_pairs):
                ps = slice(j * P, (j + 1) * P)
                res = _dot_tn(jnp.concatenate([ci["b_h"][:, ps], ci["k_h"][:, ps]], axis=0),
                              jnp.concatenate([rhs2[(c, j, 0)], rhs2[(c, j, 1)]], axis=1))
                for par in range(2):
                    mn[(c, j, par)] = res[D * par:D * (par + 1), P * par:P * (par + 1)]
        yield
        y = {}
        for c in chunks:
            ci = chunk_in[c]
            for j in range(n_pairs):
                ps = slice(j * P, (j + 1) * P)
                for par in range(2):
                    it = (c, j, par)
                    h = 2 * j + par
                    g_p = rb[it] + ci["r_t"][:, ps]
                    m_p = mn[it] + jnp.where(diag[par], ci["w_l"][:, ps], 0.0)
                    z = state_ref[h]
                    rhs = jnp.concatenate([z, zeros_lp] if par == 0 else [zeros_lp, z], axis=0)
                    res = _dot(jnp.concatenate([g_p, m_p], axis=0), rhs)
                    y[it] = res[:L] + rb[it]
                    state_ref[h] = other[par](res[L:] + mn[it])
            yield
        for c in chunks:
            rows = pl.ds(c * L, L)
            yc = jnp.concatenate(
                [pltpu.roll(jnp.where(low, y[c, j, 1], y[c, j, 0]), D, 1) for j in range(n_pairs)], axis=1)
            mu = _seg_sum(yc, seg) * (1.0 / HEAD_DIM)
            yc = yc - mu
            var = _seg_sum(yc * yc, seg) * (1.0 / HEAD_DIM)
            yn = yc * lax.rsqrt(var + RWKV_GN_EPS) * lg_ref[...] + lb_ref[...]
            o_ref[0, rows, :] = ((yn + bonus_ref[0, rows, :]) * g_ref[0, rows, :]).astype(BF16)
            yield

    swa = _swa_stages(q_ref, kc_ref, kp_ref, vc_ref, vp_ref, bias_ref, sink_ref, oatt_ref)
    next(swa)
    n_groups = n_chunks // RWKV_GROUP
    prev = iter(())
    for g in range(n_groups):
        cur = front(g)
        for _ in cur:
            next(prev, None)
            next(swa, None)
        for _ in prev:
            next(swa, None)
        prev = back(g)
    for _ in prev:
        next(swa, None)
    for _ in swa:
        pass


def _mixers(at, bt, kt, rt, bh, kh, v, wl, bonus, g, lnx_g, lnx_b, seg, q, ak, av, bias, sinks):
    B, S, W = at.shape
    tc = min(MIX_TILE, S)
    nb = tc // ATT_BLOCK
    bias = bias.reshape(N_KV_HEADS, GQA_GROUP * ATT_BLOCK, ATT_BLOCK)
    const = lambda shape: pl.BlockSpec(shape, lambda b, j: (0,) * len(shape))
    rows = lambda w: pl.BlockSpec((1, tc, w), lambda b, j: (b, j, 0))
    prev_block = lambda w: pl.BlockSpec((1, ATT_BLOCK, w), lambda b, j: (b, jnp.maximum(j * nb - 1, 0), 0))
    return pl.pallas_call(
        _mixer_kernel,
        grid=(B, S // tc),
        in_specs=([rows(W)] * 7 + [pl.BlockSpec((1, tc // CHUNK, 1, W), lambda b, j: (b, j, 0, 0))]
                  + [rows(W)] * 2 + [const(lnx_g.shape), const(lnx_b.shape), const(seg.shape)]
                  + [rows(ATT_Q_W), rows(ATT_KV_W), prev_block(ATT_KV_W), rows(ATT_KV_W), prev_block(ATT_KV_W),
                     const(bias.shape), pl.BlockSpec(memory_space=pltpu.SMEM)]),
        out_specs=[rows(W), rows(ATT_Q_W)],
        out_shape=[jax.ShapeDtypeStruct((B, S, W), BF16), jax.ShapeDtypeStruct((B, S, ATT_Q_W), BF16)],
        scratch_shapes=[pltpu.VMEM((RWKV_HEADS, CHUNK, 2 * HEAD_DIM), F32)],
        compiler_params=pltpu.CompilerParams(dimension_semantics=("arbitrary", "arbitrary"),
                                             vmem_limit_bytes=VMEM_LIMIT),
        name="rwkv7_swa_mixers",
    )(at, bt, kt, rt, bh, kh, v, wl, bonus, g, lnx_g, lnx_b, seg, q, ak, ak, av, av, bias, sinks)


def _swa_stages(q_ref, kc_ref, kp_ref, vc_ref, vp_ref, bias_ref, sink_ref, o_ref):
    first = pl.program_id(1) == 0
    rows = GQA_GROUP * ATT_BLOCK
    col = lax.broadcasted_iota(jnp.int32, (rows, ATT_BLOCK), 1)
    qpos = lax.broadcasted_iota(jnp.int32, (rows, ATT_BLOCK), 0) % ATT_BLOCK
    use_cur = col <= qpos
    no_key = jnp.logical_and(first, jnp.logical_not(use_cur))
    row = lax.broadcasted_iota(jnp.int32, (rows, 1), 0)
    sink = []
    for hk in range(N_KV_HEADS):
        s = jnp.full((rows, 1), sink_ref[hk * GQA_GROUP], F32)
        for g in range(1, GQA_GROUP):
            s = jnp.where(row >= g * ATT_BLOCK, sink_ref[hk * GQA_GROUP + g], s)
        sink.append(s)

    def window(ref_cur, ref_prev, blk, hk):
        ks = slice(hk * HEAD_DIM, (hk + 1) * HEAD_DIM)
        prev = (ref_prev[0, :, ks] if blk == 0
                else ref_cur[0, pl.ds((blk - 1) * ATT_BLOCK, ATT_BLOCK), ks])
        return jnp.concatenate([prev, ref_cur[0, pl.ds(blk * ATT_BLOCK, ATT_BLOCK), ks]], axis=0)

    items = [(blk, hk) for blk in range(q_ref.shape[1] // ATT_BLOCK) for hk in range(N_KV_HEADS)]
    logits = {}
    for blk, hk in items:
        q = q_ref[0, pl.ds(blk * ATT_BLOCK, ATT_BLOCK), :]
        qg = jnp.concatenate([q[:, (hk * GQA_GROUP + g) * HEAD_DIM:(hk * GQA_GROUP + g + 1) * HEAD_DIM]
                              for g in range(GQA_GROUP)], axis=0)
        lg = lax.dot_general(qg, window(kc_ref, kp_ref, blk, hk), (((1,), (1,)), ((), ())),
                             preferred_element_type=F32)
        lg = jnp.where(use_cur, lg[:, ATT_BLOCK:], lg[:, :ATT_BLOCK]) + bias_ref[hk]
        logits[blk, hk] = jnp.where(no_key, NEG_INF, lg) if blk == 0 else lg
    yield

    e2, e_sink = {}, {}
    for it in items:
        m = jnp.maximum(jnp.max(logits[it], axis=-1, keepdims=True), sink[it[1]])
        eb = jnp.exp(logits.pop(it) - m).astype(BF16)
        e_sink[it] = jnp.exp(sink[it[1]] - m)
        zero = jnp.zeros_like(eb)
        e2[it] = jnp.concatenate([jnp.where(use_cur, zero, eb), jnp.where(use_cur, eb, zero)], axis=1)
        yield

    low = lax.broadcasted_iota(jnp.int32, (ATT_BLOCK, 2 * HEAD_DIM), 1) < HEAD_DIM
    for blk, hk in items:
        v2 = window(vc_ref, vp_ref, blk, hk)
        vw = jnp.concatenate([v2, v2, jnp.ones((2 * ATT_BLOCK, 2 * HEAD_DIM), BF16)], axis=1)
        res = jnp.dot(e2.pop((blk, hk)), vw, preferred_element_type=F32)
        out = res[:, :2 * HEAD_DIM] / (res[:, 2 * HEAD_DIM:] + e_sink.pop((blk, hk)))
        for g in range(0, GQA_GROUP, 2):
            h = hk * GQA_GROUP + g
            pair = jnp.where(low, out[g * ATT_BLOCK:(g + 1) * ATT_BLOCK], out[(g + 1) * ATT_BLOCK:(g + 2) * ATT_BLOCK])
            o_ref[0, pl.ds(blk * ATT_BLOCK, ATT_BLOCK), h * HEAD_DIM:(h + 2) * HEAD_DIM] = pair.astype(BF16)
        yield


def _ffn_kernel(x_ref, yr_ref, ya_ref, mod_ref, g2_ref, wo_ref, wg_ref, wu_ref, wd_ref, o_ref):
    x = x_ref[0]
    wo = wo_ref[...]
    mix = (jnp.dot(yr_ref[0], wo[:RWKV_W], preferred_element_type=F32)
           + jnp.dot(ya_ref[0], wo[RWKV_W:], preferred_element_type=F32))
    h_res = x + mod_ref[0, 2:3, :] * mix
    ms = jnp.mean(h_res * h_res, axis=-1, keepdims=True)
    h2 = (h_res * lax.rsqrt(ms + NORM_EPS) * g2_ref[...] * (1.0 + mod_ref[0, 4:5, :])
          + mod_ref[0, 3:4, :]).astype(BF16)
    d_ff = wg_ref.shape[1]
    ffn = jnp.zeros_like(x)
    for lo in range(0, d_ff, FFN_BLOCK):
        cs = slice(lo, min(lo + FFN_BLOCK, d_ff))
        gt = jnp.dot(h2, wg_ref[:, cs], preferred_element_type=F32)
        up = jnp.dot(h2, wu_ref[:, cs], preferred_element_type=F32)
        act = (gt * _sigmoid(gt) * up).astype(BF16)
        ffn = ffn + jnp.dot(act, wd_ref[cs, :], preferred_element_type=F32)
    o_ref[0] = h_res + mod_ref[0, 5:6, :] * ffn


def _out_ffn(x, y_rwkv, y_att, mod3, g2, w_out, w_gate, w_up, w_down):
    B, S, D = x.shape
    tm = min(FFN_TILE, S)
    resident = lambda shape: pl.BlockSpec(shape, lambda b, j: (0,) * len(shape),
                                          pipeline_mode=pl.Buffered(1))
    row_spec = lambda w: pl.BlockSpec((1, tm, w), lambda b, j: (b, j, 0))
    return pl.pallas_call(
        _ffn_kernel,
        grid=(B, S // tm),
        in_specs=[row_spec(D), row_spec(RWKV_W), row_spec(ATT_Q_W),
                  pl.BlockSpec((1, N_MOD, D), lambda b, j: (b, 0, 0)),
                  resident(g2.shape), resident(w_out.shape), resident(w_gate.shape),
                  resident(w_up.shape), resident(w_down.shape)],
        out_specs=row_spec(D),
        out_shape=jax.ShapeDtypeStruct((B, S, D), F32),
        compiler_params=pltpu.CompilerParams(dimension_semantics=("arbitrary", "arbitrary"),
                                             vmem_limit_bytes=VMEM_LIMIT),
        name="out_proj_ffn",
    )(x, y_rwkv, y_att, mod3, g2, w_out, w_gate, w_up, w_down)


def kernel(x, c, w_ada, b_ada, norm1_g, w_in, rwkv_mu, w0, w_lora_up, a0, a_lora_up, g_lora_up, k_k, k_a, r_k, lnx_g, lnx_b, q_norm_g, k_norm_g, sinks, rel_bias, w_out, norm2_g, w_gate, w_up, w_down):
    B, S, D = x.shape
    row = lambda t: t.reshape(1, -1).astype(F32)
    lane = jnp.arange(MXU_TILE) // HEAD_DIM
    seg = (lane[:, None] == lane[None, :]).astype(BF16)
    t_idx = jnp.arange(min(IN_SUB, S))
    tri = ((t_idx[:, None] >= t_idx[None, :])
           & (t_idx[:, None] // CHUNK == t_idx[None, :] // CHUNK)).astype(BF16)

    mod3 = _modulation(c.astype(F32), w_ada, b_ada).reshape(B, N_MOD, D)
    bias = _bias_table(rel_bias)

    at, bt, kt, rt, bh, kh, v, wl, bonus, g, q, ak, av = _in_projection(
        x, mod3, row(norm1_g), w_in.astype(BF16), row(rwkv_mu), row(w0), w_lora_up.astype(BF16),
        row(a0), a_lora_up.astype(BF16), g_lora_up.astype(BF16), row(k_k), row(k_a), row(r_k),
        row(jnp.tile(q_norm_g, N_Q_HEADS)), row(jnp.tile(k_norm_g, N_KV_HEADS)), seg, tri)

    y_rwkv, y_att = _mixers(at, bt, kt, rt, bh, kh, v, wl, bonus, g, row(lnx_g), row(lnx_b), seg,
                            q, ak, av, bias, sinks)

    out = _out_ffn(x, y_rwkv, y_att, mod3, row(norm2_g), w_out.astype(BF16),
                   w_gate.astype(BF16), w_up.astype(BF16), w_down.astype(BF16))
    return out.astype(x.dtype)
```

```python
import functools
import math

import jax
import jax.numpy as jnp
from jax import lax
from jax.experimental import pallas as pl
from jax.experimental.pallas import tpu as pltpu

F32 = jnp.float32
BF16 = jnp.bfloat16

HEAD_DIM = 64
RWKV_HEADS = 8
RWKV_W = RWKV_HEADS * HEAD_DIM
DECAY_LORA = 64
AAA_LORA = 64
GATE_LORA = 128
RWKV_COLS = 3 * RWKV_W + DECAY_LORA + AAA_LORA + GATE_LORA
RWKV_GN_EPS = 64e-5
N_Q_HEADS = 8
N_KV_HEADS = 2
GQA_GROUP = N_Q_HEADS // N_KV_HEADS
ATT_Q_W = N_Q_HEADS * HEAD_DIM
ATT_KV_W = N_KV_HEADS * HEAD_DIM
ATT_BLOCK = 128
WINDOW = 128
ATT_SCALE = 1.0 / math.sqrt(HEAD_DIM)
LOG2E = math.log2(math.e)
NUM_BUCKETS = 32
MAX_DISTANCE = 128
N_MOD = 6
NORM_EPS = 1e-6
NEG_INF = -1e30

CHUNK = 64
IN_TILE = 1024
IN_SUB = 256
MIX_TILE = 512
MIX_BATCH = 1
FFN_TILE = 512
FFN_SUB = 256
MXU_TILE = 256
F32_SUBLANES = 8
FFN_BLOCK = 4 * MXU_TILE
VMEM_LIMIT = 56 * 1024 * 1024


def _dot(a, b):
    return jnp.dot(a.astype(BF16), b.astype(BF16), preferred_element_type=F32)


def _dot_nt(a, b):
    return lax.dot_general(a.astype(BF16), b.astype(BF16), (((1,), (1,)), ((), ())),
                           preferred_element_type=F32)


def _dot_tn(a, b):
    return lax.dot_general(a.astype(BF16), b.astype(BF16), (((0,), (0,)), ((), ())),
                           preferred_element_type=F32)


def _split(x):
    hi = x.astype(BF16)
    lo = (x - hi.astype(F32)).astype(BF16)
    return hi, lo


def _seg_sum(x, seg):
    m, w = x.shape
    xb = x.astype(BF16)
    blk = seg.shape[0]
    if w <= blk:
        return jnp.dot(xb, seg[:w, :w], preferred_element_type=F32)
    n = w // blk
    stacked = jnp.concatenate([xb[:, i * blk:(i + 1) * blk] for i in range(n)], axis=0)
    res = jnp.dot(stacked, seg, preferred_element_type=F32)
    return jnp.concatenate([res[i * m:(i + 1) * m] for i in range(n)], axis=1)


def _dot_lhs_exact(m, x):
    hi, lo = _split(x)
    return (jnp.dot(m, hi, preferred_element_type=F32) + jnp.dot(m, lo, preferred_element_type=F32))


def _sigmoid(z):
    return 0.5 * jnp.tanh(0.5 * z) + 0.5


def _mod_kernel(c_ref, w_ref, b_ref, o_ref):
    c = c_ref[...]
    s = c * _sigmoid(c)
    s_hi, s_lo = _split(s)
    w = w_ref[...]
    w_hi, w_lo = _split(w)
    d = lambda x, y: jnp.dot(x, y, preferred_element_type=F32)
    o_ref[...] = d(s_hi, w_hi) + d(s_hi, w_lo) + d(s_lo, w_hi) + b_ref[...]


def _modulation(c, w_ada, b_ada):
    B, D = c.shape
    n = w_ada.shape[1]
    blk = D
    return pl.pallas_call(
        _mod_kernel,
        grid=(n // blk,),
        in_specs=[pl.BlockSpec((B, D), lambda j: (0, 0)),
                  pl.BlockSpec((D, blk), lambda j: (0, j)),
                  pl.BlockSpec((1, blk), lambda j: (0, j))],
        out_specs=pl.BlockSpec((B, blk), lambda j: (0, j)),
        out_shape=jax.ShapeDtypeStruct((B, n), F32),
        name="adaln_mod",
    )(c, w_ada, b_ada.reshape(1, n))


def _t5_thresholds():
    max_exact = NUM_BUCKETS // 2
    out = []
    for k in range(1, NUM_BUCKETS - max_exact):
        edge = max_exact * (MAX_DISTANCE / max_exact) ** (k / (NUM_BUCKETS - max_exact))
        assert abs(edge - round(edge)) > 1e-6, "bucket edge on an integer distance: floor() would be ambiguous"
        out.append(math.ceil(edge))
    return tuple(out)


_T5_THRESHOLDS = _t5_thresholds()


def _bias_kernel(rb_ref, o_ref):
    qi = lax.broadcasted_iota(jnp.int32, (ATT_BLOCK, ATT_BLOCK), 0)
    kj = lax.broadcasted_iota(jnp.int32, (ATT_BLOCK, ATT_BLOCK), 1)
    n = jnp.where(kj <= qi, qi - kj, qi + ATT_BLOCK - kj)
    max_exact = NUM_BUCKETS // 2
    large = jnp.full(n.shape, max_exact, jnp.int32)
    for t in _T5_THRESHOLDS:
        large = large + jnp.where(n >= t, 1, 0)
    bucket = jnp.where(n < max_exact, n, large)
    for h in range(N_Q_HEADS):
        acc = jnp.zeros((ATT_BLOCK, ATT_BLOCK), F32)
        for b in range(NUM_BUCKETS):
            acc = jnp.where(bucket == b, rb_ref[b, h], acc)
        o_ref[h] = acc * LOG2E


def _bias_table(rel_bias):
    assert WINDOW == ATT_BLOCK
    return pl.pallas_call(
        _bias_kernel,
        in_specs=[pl.BlockSpec(memory_space=pltpu.SMEM)],
        out_specs=pl.BlockSpec(memory_space=pltpu.VMEM),
        out_shape=jax.ShapeDtypeStruct((N_Q_HEADS, ATT_BLOCK, ATT_BLOCK), F32),
        name="rel_bias_table",
    )(rel_bias)


def _inproj_kernel(x_ref, mod_ref, g1_ref, win_ref, mu_ref, w0_ref, wlu_ref, a0_ref, alu_ref,
                   glu_ref, kk_ref, ka_ref, rk_ref, qg_ref, kg_ref, seg_ref, tri_ref,
                   at_out, bt_out, kt_out, rt_out, bh_out, kh_out, v_out, wl_out, bonus_out, g_out,
                   q_out, ak_out, av_out, carry_ref):
    @pl.when(pl.program_id(1) == 0)
    def _():
        carry_ref[...] = jnp.zeros_like(carry_ref)

    tm = tri_ref.shape[0]
    refs = (mu_ref, w0_ref, wlu_ref, a0_ref, alu_ref, glu_ref, kk_ref, ka_ref, rk_ref, qg_ref, kg_ref,
            seg_ref, tri_ref)
    outs = (at_out, bt_out, kt_out, rt_out, bh_out, kh_out, v_out, wl_out, bonus_out, g_out, q_out,
            ak_out, av_out)
    tail = iter(())
    gain = g1_ref[...] * (1.0 + mod_ref[0, 1:2, :])
    for s in range(x_ref.shape[1] // tm):
        rs = pl.ds(s * tm, tm)
        x = x_ref[0, rs, :]
        ms = jnp.mean(x * x, axis=-1, keepdims=True)
        h = (x * lax.rsqrt(ms + NORM_EPS) * gain + mod_ref[0, 0:1, :]).astype(BF16)
        pieces = []
        for lo in range(0, win_ref.shape[1], MXU_TILE):
            pieces.append(jnp.dot(h, win_ref[:, lo:lo + MXU_TILE], preferred_element_type=F32))
            next(tail, None)
        for _ in tail:
            pass
        tail = _inproj_tail(jnp.concatenate(pieces, axis=1), rs, s * (tm // CHUNK), refs, outs, carry_ref)
    for _ in tail:
        pass


def _inproj_tail(p, rs, c0, refs, outs, carry_ref):
    (mu_ref, w0_ref, wlu_ref, a0_ref, alu_ref, glu_ref, kk_ref, ka_ref, rk_ref, qg_ref, kg_ref,
     seg_ref, tri_ref) = refs
    (at_out, bt_out, kt_out, rt_out, bh_out, kh_out, v_out, wl_out, bonus_out, g_out, q_out,
     ak_out, av_out) = outs
    tm = p.shape[0]

    mm = lambda a, b: jnp.dot(a, b, preferred_element_type=F32)

    pr = p[:, :RWKV_COLS]
    rolled = pltpu.roll(pr, 1, 0)
    first = lax.broadcasted_iota(jnp.int32, (F32_SUBLANES, RWKV_COLS), 0) == 0
    prev = jnp.concatenate([jnp.where(first, carry_ref[...], rolled[:F32_SUBLANES]), rolled[F32_SUBLANES:]],
                           axis=0)
    carry_ref[...] = pr[tm - 1:tm, :]
    pm = pr + (prev - pr) * mu_ref[...]
    r = pm[:, 0:RWKV_W]
    k = pm[:, RWKV_W:2 * RWKV_W]
    v = pm[:, 2 * RWKV_W:3 * RWKV_W]
    o = 3 * RWKV_W
    tanh_w = jnp.tanh(pm[:, o:o + DECAY_LORA]).astype(BF16)
    xa = pm[:, o + DECAY_LORA:o + DECAY_LORA + AAA_LORA].astype(BF16)
    sig_g = _sigmoid(pm[:, o + DECAY_LORA + AAA_LORA:RWKV_COLS]).astype(BF16)
    yield
    q = p[:, RWKV_COLS:RWKV_COLS + ATT_Q_W]
    ak = p[:, RWKV_COLS + ATT_Q_W:RWKV_COLS + ATT_Q_W + ATT_KV_W]
    av = p[:, RWKV_COLS + ATT_Q_W + ATT_KV_W:]
    kk = k * kk_ref[...]
    kk_sq, q_sq, ak_sq = kk * kk, q * q, ak * ak
    v_out[0, rs, :] = v.astype(BF16)
    av_out[0, rs, :] = av.astype(BF16)
    yield

    seg = seg_ref[...]
    z_mm = mm(tanh_w, wlu_ref[...])
    a_mm = mm(xa, alu_ref[...])
    g_mm = mm(sig_g, glu_ref[...])
    kk_ss = _seg_sum(kk_sq, seg)
    q_ss = _seg_sum(q_sq, seg)
    ak_ss = _seg_sum(ak_sq, seg)
    yield

    lw = (-math.exp(-0.5) * math.log2(math.e)) * _sigmoid(w0_ref[...] + z_mm)
    lw_hi, lw_lo = _split(lw)
    a = _sigmoid(a0_ref[...] + a_mm)
    g_out[0, rs, :] = g_mm
    kk = kk * lax.rsqrt(jnp.maximum(kk_ss, 1e-24))
    bv = kk * a
    k = k * (1.0 + (a - 1.0) * ka_ref[...])
    rk_prod = r * k * rk_ref[...]
    yield
    q_out[0, rs, :] = (q * lax.rsqrt(q_ss * (1.0 / HEAD_DIM) + NORM_EPS) * (qg_ref[...] * (ATT_SCALE * LOG2E))).astype(BF16)
    ak_out[0, rs, :] = (ak * lax.rsqrt(ak_ss * (1.0 / HEAD_DIM) + NORM_EPS) * kg_ref[...]).astype(BF16)
    yield

    tri = tri_ref[...]
    cum = mm(tri, lw_hi) + mm(tri, lw_lo)
    bonus_ss = _seg_sum(rk_prod, seg)
    yield

    n_chunks = tm // CHUNK
    ends = [cum[(c + 1) * CHUNK - 1:(c + 1) * CHUNK, :] for c in range(n_chunks)]
    cl = jnp.concatenate([jnp.broadcast_to(e, (CHUNK, RWKV_W)) for e in ends], axis=0)
    for c in range(n_chunks):
        wl_out[0, c0 + c] = jnp.exp2(ends[c])
    bonus_out[0, rs, :] = bonus_ss * v
    yield
    e_neg = jnp.exp2(-cum)
    bt_out[0, rs, :] = (bv * e_neg).astype(BF16)
    kt_out[0, rs, :] = (k * e_neg).astype(BF16)
    yield
    e_hat = jnp.exp2(cl - cum)
    bh_out[0, rs, :] = (bv * e_hat).astype(BF16)
    kh_out[0, rs, :] = (k * e_hat).astype(BF16)
    yield
    at_out[0, rs, :] = (-kk * jnp.exp2(cum - lw)).astype(BF16)
    rt_out[0, rs, :] = (r * jnp.exp2(cum)).astype(BF16)


def _in_projection(x, mod3, g1, w_in, mu, w0, wlu, a0, alu, glu, k_k, k_a, r_k, qg, kg, seg, tri):
    B, S, D = x.shape
    tm = min(IN_TILE, S)
    const = lambda shape: pl.BlockSpec(shape, lambda b, j: (0,) * len(shape))
    row_spec = lambda w: pl.BlockSpec((1, tm, w), lambda b, j: (b, j, 0))
    rows = lambda w, dt: jax.ShapeDtypeStruct((B, S, w), dt)
    return pl.pallas_call(
        _inproj_kernel,
        grid=(B, S // tm),
        in_specs=[row_spec(D),
                  pl.BlockSpec((1, N_MOD, D), lambda b, j: (b, 0, 0)),
                  const(g1.shape), const(w_in.shape), const(mu.shape), const(w0.shape),
                  const(wlu.shape), const(a0.shape), const(alu.shape), const(glu.shape),
                  const(k_k.shape), const(k_a.shape), const(r_k.shape), const(qg.shape),
                  const(kg.shape), const(seg.shape), const(tri.shape)],
        out_specs=([row_spec(RWKV_W)] * 7
                   + [pl.BlockSpec((1, tm // CHUNK, 1, RWKV_W), lambda b, j: (b, j, 0, 0))]
                   + [row_spec(RWKV_W)] * 2
                   + [row_spec(ATT_Q_W), row_spec(ATT_KV_W), row_spec(ATT_KV_W)]),
        out_shape=([rows(RWKV_W, BF16)] * 7
                   + [jax.ShapeDtypeStruct((B, S // CHUNK, 1, RWKV_W), F32)]
                   + [rows(RWKV_W, F32)] * 2
                   + [rows(ATT_Q_W, BF16), rows(ATT_KV_W, BF16), rows(ATT_KV_W, BF16)]),
        scratch_shapes=[pltpu.VMEM((1, RWKV_COLS), F32)],
        compiler_params=pltpu.CompilerParams(dimension_semantics=("arbitrary", "arbitrary"),
                                             vmem_limit_bytes=VMEM_LIMIT),
        name="in_projection",
    )(x, mod3, g1, w_in, mu, w0, wlu, a0, alu, glu, k_k, k_a, r_k, qg, kg, seg, tri)


def _mixer_kernel(at_ref, bt_ref, kt_ref, rt_ref, bh_ref, kh_ref, v_ref, wl_ref, bonus_ref, g_ref,
                  lg_ref, lb_ref, seg_ref, q_ref, kc_ref, kp_ref, vc_ref, vp_ref, bias_ref, sink_ref,
                  o_ref, oatt_ref, state_ref):
    @pl.when(pl.program_id(1) == 0)
    def _():
        state_ref[...] = jnp.zeros_like(state_ref)

    L, D, P = CHUNK, HEAD_DIM, 2 * HEAD_DIM
    n_batch = at_ref.shape[0]
    n_chunks = at_ref.shape[1] // L
    n_pairs = RWKV_W // P
    row = lax.broadcasted_iota(jnp.int32, (L, P), 0)
    lane = lax.broadcasted_iota(jnp.int32, (L, P), 1)
    low = lane < D
    s_idx = jnp.where(low, lane, lane - D)
    strict = row > s_idx
    incl = row >= s_idx
    zeros_lp = jnp.zeros((L, P), F32)
    own = (lambda t: jnp.where(low, t, jnp.zeros_like(t)), lambda t: jnp.where(low, jnp.zeros_like(t), t))
    other = (own[1], own[0])
    diag = (lane == row, lane == row + D)
    seg = seg_ref[...]

    chunks = [(bi, c) for c in range(n_chunks) for bi in range(n_batch)]
    items = [(ck, j, par) for ck in chunks for j in range(n_pairs) for par in range(2)]
    chunk_in, top, bot, v_o, x = {}, {}, {}, {}, {}

    def front():
        for ck in chunks:
            bi, c = ck
            rows = pl.ds(c * L, L)
            ci = dict(a_t=at_ref[bi, rows, :], b_t=bt_ref[bi, rows, :], k_t=kt_ref[bi, rows, :],
                      r_t=rt_ref[bi, rows, :], b_h=bh_ref[bi, rows, :], k_h=kh_ref[bi, rows, :],
                      v=v_ref[bi, rows, :], w_l=wl_ref[bi, c])
            chunk_in[ck] = ci
            for j in range(n_pairs):
                ps = slice(j * P, (j + 1) * P)
                atp, rtp = ci["a_t"][:, ps], ci["r_t"][:, ps]
                btp, ktp = ci["b_t"][:, ps], ci["k_t"][:, ps]
                s1 = _dot_nt(jnp.concatenate([atp, rtp], axis=0),
                             jnp.concatenate([own[0](btp), own[0](ktp), own[1](btp), own[1](ktp)], axis=0))
                v_swap = pltpu.roll(ci["v"][:, ps].astype(F32), D, 1)
                for par in range(2):
                    it = (ck, j, par)
                    top[it] = jnp.where(strict, s1[:L, P * par:P * (par + 1)], 0.0)
                    bot[it] = jnp.where(incl, s1[L:, P * par:P * (par + 1)], 0.0)
                    v_o[it] = other[par](v_swap)
                    x[it] = own[par](atp).astype(F32)
        yield
        for it in items:
            x[it] = x[it] + _dot(top[it], jnp.concatenate([zeros_lp, v_o[it]], axis=0))
        yield
        a_pow = {it: top[it][:, :D].astype(BF16) for it in items}
        n = 1
        while n < L:
            last = 2 * n >= L
            for it in items:
                xb = x[it].astype(BF16)
                if last:
                    x[it] = x[it] + jnp.dot(a_pow[it], xb, preferred_element_type=F32)
                else:
                    res = jnp.dot(a_pow[it], jnp.concatenate([xb, a_pow[it]], axis=1),
                                  preferred_element_type=F32)
                    x[it] = x[it] + res[:, :P]
                    a_pow[it] = res[:, P:].astype(BF16)
            n *= 2
            yield

    def back():
        rhs2 = {it: jnp.concatenate([x[it], v_o[it]], axis=0).astype(BF16) for it in items}
        rb = {it: jnp.dot(bot[it].astype(BF16), rhs2[it], preferred_element_type=F32) for it in items}
        mn = {}
        for ck in chunks:
            ci = chunk_in[ck]
            for j in range(n_pairs):
                ps = slice(j * P, (j + 1) * P)
                res = _dot_tn(jnp.concatenate([ci["b_h"][:, ps], ci["k_h"][:, ps]], axis=0),
                              jnp.concatenate([rhs2[(ck, j, 0)], rhs2[(ck, j, 1)]], axis=1))
                for par in range(2):
                    mn[(ck, j, par)] = res[D * par:D * (par + 1), P * par:P * (par + 1)]
        yield
        y = {}
        for ck in chunks:
            ci = chunk_in[ck]
            for j in range(n_pairs):
                ps = slice(j * P, (j + 1) * P)
                for par in range(2):
                    it = (ck, j, par)
                    h = (ck[0], 2 * j + par)
                    g_p = rb[it] + ci["r_t"][:, ps]
                    m_p = mn[it] + jnp.where(diag[par], ci["w_l"][:, ps], 0.0)
                    z = state_ref[h]
                    rhs = jnp.concatenate([z, zeros_lp] if par == 0 else [zeros_lp, z], axis=0)
                    res = _dot(jnp.concatenate([g_p, m_p], axis=0), rhs)
                    y[it] = res[:L] + rb[it]
                    state_ref[h] = other[par](res[L:] + mn[it])
            if ck[0] == n_batch - 1:
                yield
        for ck in chunks:
            bi, c = ck
            rows = pl.ds(c * L, L)
            yc = jnp.concatenate(
                [pltpu.roll(jnp.where(low, y[ck, j, 1], y[ck, j, 0]), D, 1) for j in range(n_pairs)], axis=1)
            mu = _seg_sum(yc, seg) * (1.0 / HEAD_DIM)
            yc = yc - mu
            var = _seg_sum(yc * yc, seg) * (1.0 / HEAD_DIM)
            yn = yc * lax.rsqrt(var + RWKV_GN_EPS) * lg_ref[...] + lb_ref[...]
            o_ref[bi, rows, :] = ((yn + bonus_ref[bi, rows, :]) * g_ref[bi, rows, :]).astype(BF16)
            if bi == n_batch - 1:
                yield

    swa = _swa_stages(q_ref, kc_ref, kp_ref, vc_ref, vp_ref, bias_ref, sink_ref, oatt_ref)
    next(swa)
    for _ in front():
        next(swa, None)
    for _ in back():
        next(swa, None)
    for _ in swa:
        pass


def _mixers(at, bt, kt, rt, bh, kh, v, wl, bonus, g, lnx_g, lnx_b, seg, q, ak, av, bias, sinks):
    B, S, W = at.shape
    tc = min(MIX_TILE, S)
    nbat = MIX_BATCH if B % MIX_BATCH == 0 else 1
    nb = tc // ATT_BLOCK
    bias = bias.reshape(N_KV_HEADS, GQA_GROUP * ATT_BLOCK, ATT_BLOCK)
    const = lambda shape: pl.BlockSpec(shape, lambda b, j: (0,) * len(shape))
    rows = lambda w: pl.BlockSpec((nbat, tc, w), lambda b, j: (b, j, 0))
    prev_block = lambda w: pl.BlockSpec((nbat, ATT_BLOCK, w), lambda b, j: (b, jnp.maximum(j * nb - 1, 0), 0))
    return pl.pallas_call(
        _mixer_kernel,
        grid=(B // nbat, S // tc),
        in_specs=([rows(W)] * 7 + [pl.BlockSpec((nbat, tc // CHUNK, 1, W), lambda b, j: (b, j, 0, 0))]
                  + [rows(W)] * 2 + [const(lnx_g.shape), const(lnx_b.shape), const(seg.shape)]
                  + [rows(ATT_Q_W), rows(ATT_KV_W), prev_block(ATT_KV_W), rows(ATT_KV_W), prev_block(ATT_KV_W),
                     const(bias.shape), pl.BlockSpec(memory_space=pltpu.SMEM)]),
        out_specs=[rows(W), rows(ATT_Q_W)],
        out_shape=[jax.ShapeDtypeStruct((B, S, W), BF16), jax.ShapeDtypeStruct((B, S, ATT_Q_W), BF16)],
        scratch_shapes=[pltpu.VMEM((nbat, RWKV_HEADS, CHUNK, 2 * HEAD_DIM), F32)],
        compiler_params=pltpu.CompilerParams(dimension_semantics=("arbitrary", "arbitrary"),
                                             vmem_limit_bytes=VMEM_LIMIT),
        name="rwkv7_swa_mixers",
    )(at, bt, kt, rt, bh, kh, v, wl, bonus, g, lnx_g, lnx_b, seg, q, ak, ak, av, av, bias, sinks)


def _swa_stages(q_ref, kc_ref, kp_ref, vc_ref, vp_ref, bias_ref, sink_ref, o_ref):
    first = pl.program_id(1) == 0
    rows = GQA_GROUP * ATT_BLOCK
    col = lax.broadcasted_iota(jnp.int32, (rows, ATT_BLOCK), 1)
    qpos = lax.broadcasted_iota(jnp.int32, (rows, ATT_BLOCK), 0) % ATT_BLOCK
    use_cur = col <= qpos
    no_key = jnp.logical_and(first, jnp.logical_not(use_cur))
    row = lax.broadcasted_iota(jnp.int32, (rows, 1), 0)
    sink = []
    for hk in range(N_KV_HEADS):
        s = jnp.full((rows, 1), sink_ref[hk * GQA_GROUP], F32)
        for g in range(1, GQA_GROUP):
            s = jnp.where(row >= g * ATT_BLOCK, sink_ref[hk * GQA_GROUP + g], s)
        sink.append(s * LOG2E)

    def window(ref_cur, ref_prev, bi, blk, hk):
        ks = slice(hk * HEAD_DIM, (hk + 1) * HEAD_DIM)
        prev = (ref_prev[bi, :, ks] if blk == 0
                else ref_cur[bi, pl.ds((blk - 1) * ATT_BLOCK, ATT_BLOCK), ks])
        return jnp.concatenate([prev, ref_cur[bi, pl.ds(blk * ATT_BLOCK, ATT_BLOCK), ks]], axis=0)

    items = [(bi, blk, hk) for bi in range(q_ref.shape[0]) for blk in range(q_ref.shape[1] // ATT_BLOCK)
             for hk in range(N_KV_HEADS)]
    logits = {}
    for bi, blk, hk in items:
        q = q_ref[bi, pl.ds(blk * ATT_BLOCK, ATT_BLOCK), :]
        qg = jnp.concatenate([q[:, (hk * GQA_GROUP + g) * HEAD_DIM:(hk * GQA_GROUP + g + 1) * HEAD_DIM]
                              for g in range(GQA_GROUP)], axis=0)
        lg = lax.dot_general(qg, window(kc_ref, kp_ref, bi, blk, hk), (((1,), (1,)), ((), ())),
                             preferred_element_type=F32)
        lg = jnp.where(use_cur, lg[:, ATT_BLOCK:], lg[:, :ATT_BLOCK]) + bias_ref[hk]
        logits[bi, blk, hk] = jnp.where(no_key, NEG_INF, lg) if blk == 0 else lg
    yield

    e2, e_sink = {}, {}
    for it in items:
        m = jnp.maximum(jnp.max(logits[it], axis=-1, keepdims=True), sink[it[2]])
        eb = jnp.exp2(logits.pop(it) - m).astype(BF16)
        e_sink[it] = jnp.exp2(sink[it[2]] - m)
        zero = jnp.zeros_like(eb)
        e2[it] = jnp.concatenate([jnp.where(use_cur, zero, eb), jnp.where(use_cur, eb, zero)], axis=1)
        yield

    low = lax.broadcasted_iota(jnp.int32, (ATT_BLOCK, 2 * HEAD_DIM), 1) < HEAD_DIM
    for it in items:
        bi, blk, hk = it
        v2 = window(vc_ref, vp_ref, bi, blk, hk)
        vw = jnp.concatenate([v2, v2, jnp.ones((2 * ATT_BLOCK, 2 * HEAD_DIM), BF16)], axis=1)
        res = jnp.dot(e2.pop(it), vw, preferred_element_type=F32)
        out = res[:, :2 * HEAD_DIM] / (res[:, 2 * HEAD_DIM:] + e_sink.pop(it))
        for g in range(0, GQA_GROUP, 2):
            h = hk * GQA_GROUP + g
            pair = jnp.where(low, out[g * ATT_BLOCK:(g + 1) * ATT_BLOCK], out[(g + 1) * ATT_BLOCK:(g + 2) * ATT_BLOCK])
            o_ref[bi, pl.ds(blk * ATT_BLOCK, ATT_BLOCK), h * HEAD_DIM:(h + 2) * HEAD_DIM] = pair.astype(BF16)
        yield


def _ffn_kernel(x_ref, yr_ref, ya_ref, mod_ref, g2_ref, wo_ref, wg_ref, wu_ref, wd_ref, o_ref):
    n_sub = max(x_ref.shape[1] // FFN_SUB, 1)
    tm = x_ref.shape[1] // n_sub
    subs = [_ffn_rows(pl.ds(s * tm, tm), x_ref, yr_ref, ya_ref, mod_ref, g2_ref, wo_ref, wg_ref, wu_ref,
                      wd_ref, o_ref) for s in range(n_sub)]
    next(subs[0])
    live = list(subs)
    while live:
        live = [s for s in live if next(s, True) is None]


def _ffn_rows(rs, x_ref, yr_ref, ya_ref, mod_ref, g2_ref, wo_ref, wg_ref, wu_ref, wd_ref, o_ref):
    x = x_ref[0, rs, :]
    wo = wo_ref[...]
    mix = (jnp.dot(yr_ref[0, rs, :], wo[:RWKV_W], preferred_element_type=F32)
           + jnp.dot(ya_ref[0, rs, :], wo[RWKV_W:], preferred_element_type=F32))
    yield
    h_res = x + mod_ref[0, 2:3, :] * mix
    ms = jnp.mean(h_res * h_res, axis=-1, keepdims=True)
    h2 = (h_res * lax.rsqrt(ms + NORM_EPS) * (g2_ref[...] * (1.0 + mod_ref[0, 4:5, :]))
          + mod_ref[0, 3:4, :]).astype(BF16)
    yield
    d_ff = wg_ref.shape[1]
    ffn = jnp.zeros_like(x)
    for lo in range(0, d_ff, FFN_BLOCK):
        cs = slice(lo, min(lo + FFN_BLOCK, d_ff))
        gt = jnp.dot(h2, wg_ref[:, cs], preferred_element_type=F32)
        up = jnp.dot(h2, wu_ref[:, cs], preferred_element_type=F32)
        yield
        act = (gt * _sigmoid(gt) * up).astype(BF16)
        ffn = ffn + jnp.dot(act, wd_ref[cs, :], preferred_element_type=F32)
        yield
    o_ref[0, rs, :] = h_res + mod_ref[0, 5:6, :] * ffn


def _out_ffn(x, y_rwkv, y_att, mod3, g2, w_out, w_gate, w_up, w_down):
    B, S, D = x.shape
    tm = min(FFN_TILE, S)
    resident = lambda shape: pl.BlockSpec(shape, lambda b, j: (0,) * len(shape),
                                          pipeline_mode=pl.Buffered(1))
    row_spec = lambda w: pl.BlockSpec((1, tm, w), lambda b, j: (b, j, 0))
    return pl.pallas_call(
        _ffn_kernel,
        grid=(B, S // tm),
        in_specs=[row_spec(D), row_spec(RWKV_W), row_spec(ATT_Q_W),
                  pl.BlockSpec((1, N_MOD, D), lambda b, j: (b, 0, 0)),
                  resident(g2.shape), resident(w_out.shape), resident(w_gate.shape),
                  resident(w_up.shape), resident(w_down.shape)],
        out_specs=row_spec(D),
        out_shape=jax.ShapeDtypeStruct((B, S, D), F32),
        compiler_params=pltpu.CompilerParams(dimension_semantics=("arbitrary", "arbitrary"),
                                             vmem_limit_bytes=VMEM_LIMIT),
        name="out_proj_ffn",
    )(x, y_rwkv, y_att, mod3, g2, w_out, w_gate, w_up, w_down)


def kernel(x, c, w_ada, b_ada, norm1_g, w_in, rwkv_mu, w0, w_lora_up, a0, a_lora_up, g_lora_up, k_k, k_a, r_k, lnx_g, lnx_b, q_norm_g, k_norm_g, sinks, rel_bias, w_out, norm2_g, w_gate, w_up, w_down):
    B, S, D = x.shape
    row = lambda t: t.reshape(1, -1).astype(F32)
    lane = jnp.arange(MXU_TILE) // HEAD_DIM
    seg = (lane[:, None] == lane[None, :]).astype(BF16)
    t_idx = jnp.arange(min(IN_SUB, S))
    tri = ((t_idx[:, None] >= t_idx[None, :])
           & (t_idx[:, None] // CHUNK == t_idx[None, :] // CHUNK)).astype(BF16)

    mod3 = _modulation(c.astype(F32), w_ada, b_ada).reshape(B, N_MOD, D)
    bias = _bias_table(rel_bias)

    at, bt, kt, rt, bh, kh, v, wl, bonus, g, q, ak, av = _in_projection(
        x, mod3, row(norm1_g), w_in.astype(BF16), row(rwkv_mu), row(w0), w_lora_up.astype(BF16),
        row(a0), a_lora_up.astype(BF16), g_lora_up.astype(BF16), row(k_k), row(k_a), row(r_k),
        row(jnp.tile(q_norm_g, N_Q_HEADS)), row(jnp.tile(k_norm_g, N_KV_HEADS)), seg, tri)

    y_rwkv, y_att = _mixers(at, bt, kt, rt, bh, kh, v, wl, bonus, g, row(lnx_g), row(lnx_b), seg,
                            q, ak, av, bias, sinks)

    out = _out_ffn(x, y_rwkv, y_att, mod3, row(norm2_g), w_out.astype(BF16),
                   w_gate.astype(BF16), w_up.astype(BF16), w_down.astype(BF16))
    return out.astype(x.dtype)
```

```python
import math

import jax
import jax.numpy as jnp
from jax import lax
from jax.experimental import pallas as pl
from jax.experimental.pallas import tpu as pltpu

F32 = jnp.float32
BF16 = jnp.bfloat16

HEAD_DIM = 64
RWKV_HEADS = 8
RWKV_W = RWKV_HEADS * HEAD_DIM
DECAY_LORA = 64
AAA_LORA = 64
GATE_LORA = 128
RWKV_COLS = 3 * RWKV_W + DECAY_LORA + AAA_LORA + GATE_LORA
RWKV_GN_EPS = 64e-5
N_Q_HEADS = 8
N_KV_HEADS = 2
GQA_GROUP = N_Q_HEADS // N_KV_HEADS
ATT_Q_W = N_Q_HEADS * HEAD_DIM
ATT_KV_W = N_KV_HEADS * HEAD_DIM
ATT_BLOCK = 128
WINDOW = 128
ATT_SCALE = 1.0 / math.sqrt(HEAD_DIM)
LOG2E = math.log2(math.e)
NUM_BUCKETS = 32
MAX_DISTANCE = 128
N_MOD = 6
NORM_EPS = 1e-6
NEG_INF = -1e30

CHUNK = 64
IN_TILE = 1024
IN_SUB = 256
MIX_TILE = 512
MIX_BATCH = 1
FFN_TILE = 512
FFN_SUB = 256
MXU_TILE = 256
F32_SUBLANES = 8
BF16_SUBLANES = 16
FFN_BLOCK = 4 * MXU_TILE
VMEM_LIMIT = 56 * 1024 * 1024


def _dot(a, b):
    return jnp.dot(a.astype(BF16), b.astype(BF16), preferred_element_type=F32)


def _dot_nt(a, b):
    return lax.dot_general(a.astype(BF16), b.astype(BF16), (((1,), (1,)), ((), ())),
                           preferred_element_type=F32)


def _dot_tn(a, b):
    return lax.dot_general(a.astype(BF16), b.astype(BF16), (((0,), (0,)), ((), ())),
                           preferred_element_type=F32)


def _split(x):
    hi = x.astype(BF16)
    lo = (x - hi.astype(F32)).astype(BF16)
    return hi, lo


def _seg_sum(x, seg):
    m, w = x.shape
    xb = x.astype(BF16)
    blk = seg.shape[0]
    if w <= blk:
        return jnp.dot(xb, seg[:w, :w], preferred_element_type=F32)
    n = w // blk
    stacked = jnp.concatenate([xb[:, i * blk:(i + 1) * blk] for i in range(n)], axis=0)
    res = jnp.dot(stacked, seg, preferred_element_type=F32)
    return jnp.concatenate([res[i * m:(i + 1) * m] for i in range(n)], axis=1)


def _sigmoid(z):
    return 0.5 * jnp.tanh(0.5 * z) + 0.5


def _mod_kernel(c_ref, w_ref, b_ref, o_ref):
    c = c_ref[...]
    s = c * _sigmoid(c)
    s_hi, s_lo = _split(s)
    w = w_ref[...]
    w_hi, w_lo = _split(w)
    d = lambda x, y: jnp.dot(x, y, preferred_element_type=F32)
    o_ref[...] = d(s_hi, w_hi) + d(s_hi, w_lo) + d(s_lo, w_hi) + b_ref[...]


def _modulation(c, w_ada, b_ada):
    B, D = c.shape
    n = w_ada.shape[1]
    blk = D
    return pl.pallas_call(
        _mod_kernel,
        grid=(n // blk,),
        in_specs=[pl.BlockSpec((B, D), lambda j: (0, 0)),
                  pl.BlockSpec((D, blk), lambda j: (0, j)),
                  pl.BlockSpec((1, blk), lambda j: (0, j))],
        out_specs=pl.BlockSpec((B, blk), lambda j: (0, j)),
        out_shape=jax.ShapeDtypeStruct((B, n), F32),
        name="adaln_mod",
    )(c, w_ada, b_ada.reshape(1, n))


def _t5_thresholds():
    max_exact = NUM_BUCKETS // 2
    out = []
    for k in range(1, NUM_BUCKETS - max_exact):
        edge = max_exact * (MAX_DISTANCE / max_exact) ** (k / (NUM_BUCKETS - max_exact))
        assert abs(edge - round(edge)) > 1e-6, "bucket edge on an integer distance: floor() would be ambiguous"
        out.append(math.ceil(edge))
    return tuple(out)


_T5_THRESHOLDS = _t5_thresholds()


def _bias_kernel(rb_ref, o_ref):
    qi = lax.broadcasted_iota(jnp.int32, (ATT_BLOCK, ATT_BLOCK), 0)
    kj = lax.broadcasted_iota(jnp.int32, (ATT_BLOCK, ATT_BLOCK), 1)
    n = jnp.where(kj <= qi, qi - kj, qi + ATT_BLOCK - kj)
    max_exact = NUM_BUCKETS // 2
    large = jnp.full(n.shape, max_exact, jnp.int32)
    for t in _T5_THRESHOLDS:
        large = large + jnp.where(n >= t, 1, 0)
    bucket = jnp.where(n < max_exact, n, large)
    for h in range(N_Q_HEADS):
        acc = jnp.zeros((ATT_BLOCK, ATT_BLOCK), F32)
        for b in range(NUM_BUCKETS):
            acc = jnp.where(bucket == b, rb_ref[b, h], acc)
        o_ref[h] = acc * LOG2E


def _bias_table(rel_bias):
    assert WINDOW == ATT_BLOCK
    return pl.pallas_call(
        _bias_kernel,
        in_specs=[pl.BlockSpec(memory_space=pltpu.SMEM)],
        out_specs=pl.BlockSpec(memory_space=pltpu.VMEM),
        out_shape=jax.ShapeDtypeStruct((N_Q_HEADS, ATT_BLOCK, ATT_BLOCK), F32),
        name="rel_bias_table",
    )(rel_bias)


def _inproj_kernel(x_ref, mod_ref, g1_ref, win_ref, mu_ref, w0_ref, wlu_ref, a0_ref, alu_ref,
                   glu_ref, kk_ref, ka_ref, rk_ref, qg_ref, kg_ref, seg_ref, tri_ref, *rest):
    n_cast = (len(rest) - 14) // 2
    cast_in, rest = rest[:n_cast], rest[n_cast:]
    (at_out, bt_out, kt_out, rt_out, bh_out, kh_out, v_out, wl_out, bonus_out, g_out,
     q_out, ak_out, av_out) = rest[:13]
    cast_out, carry_ref = rest[13:13 + n_cast], rest[-1]

    @pl.when(pl.program_id(1) == 0)
    def _():
        carry_ref[...] = jnp.zeros_like(carry_ref)

    for src, dst in zip(cast_in, cast_out):
        dst[...] = src[...].astype(BF16)

    tm = tri_ref.shape[0]
    refs = (mu_ref, w0_ref, wlu_ref, a0_ref, alu_ref, glu_ref, kk_ref, ka_ref, rk_ref, qg_ref, kg_ref,
            seg_ref, tri_ref)
    outs = (at_out, bt_out, kt_out, rt_out, bh_out, kh_out, v_out, wl_out, bonus_out, g_out, q_out,
            ak_out, av_out)
    tail = iter(())
    gain = g1_ref[...] * (1.0 + mod_ref[0, 1:2, :])
    for s in range(x_ref.shape[1] // tm):
        rs = pl.ds(s * tm, tm)
        x = x_ref[0, rs, :]
        ms = jnp.mean(x * x, axis=-1, keepdims=True)
        h = (x * lax.rsqrt(ms + NORM_EPS) * gain + mod_ref[0, 0:1, :]).astype(BF16)
        pieces = []
        for lo in range(0, win_ref.shape[1], MXU_TILE):
            pieces.append(jnp.dot(h, win_ref[:, lo:lo + MXU_TILE], preferred_element_type=F32))
            next(tail, None)
        for _ in tail:
            pass
        tail = _inproj_tail(jnp.concatenate(pieces, axis=1), rs, s * (tm // CHUNK), refs, outs, carry_ref)
    for _ in tail:
        pass


def _inproj_tail(p, rs, c0, refs, outs, carry_ref):
    (mu_ref, w0_ref, wlu_ref, a0_ref, alu_ref, glu_ref, kk_ref, ka_ref, rk_ref, qg_ref, kg_ref,
     seg_ref, tri_ref) = refs
    (at_out, bt_out, kt_out, rt_out, bh_out, kh_out, v_out, wl_out, bonus_out, g_out, q_out,
     ak_out, av_out) = outs
    tm = p.shape[0]

    mm = lambda a, b: jnp.dot(a, b, preferred_element_type=F32)

    pr = p[:, :RWKV_COLS]
    rolled = pltpu.roll(pr, 1, 0)
    first = lax.broadcasted_iota(jnp.int32, (F32_SUBLANES, RWKV_COLS), 0) == 0
    prev = jnp.concatenate([jnp.where(first, carry_ref[...], rolled[:F32_SUBLANES]), rolled[F32_SUBLANES:]],
                           axis=0)
    carry_ref[...] = pr[tm - 1:tm, :]
    pm = pr + (prev - pr) * mu_ref[...]
    r = pm[:, 0:RWKV_W]
    k = pm[:, RWKV_W:2 * RWKV_W]
    v = pm[:, 2 * RWKV_W:3 * RWKV_W]
    o = 3 * RWKV_W
    tanh_w = jnp.tanh(pm[:, o:o + DECAY_LORA]).astype(BF16)
    xa = pm[:, o + DECAY_LORA:o + DECAY_LORA + AAA_LORA].astype(BF16)
    sig_g = _sigmoid(pm[:, o + DECAY_LORA + AAA_LORA:RWKV_COLS]).astype(BF16)
    yield
    q = p[:, RWKV_COLS:RWKV_COLS + ATT_Q_W]
    ak = p[:, RWKV_COLS + ATT_Q_W:RWKV_COLS + ATT_Q_W + ATT_KV_W]
    av = p[:, RWKV_COLS + ATT_Q_W + ATT_KV_W:]
    kk = k * kk_ref[...]
    kk_sq, q_sq, ak_sq = kk * kk, q * q, ak * ak
    v_out[0, rs, :] = v.astype(BF16)
    av_out[0, rs, :] = av.astype(BF16)
    yield

    seg = seg_ref[...]
    z_mm = mm(tanh_w, wlu_ref[...])
    a_mm = mm(xa, alu_ref[...])
    g_mm = mm(sig_g, glu_ref[...])
    kk_ss = _seg_sum(kk_sq, seg)
    q_ss = _seg_sum(q_sq, seg)
    ak_ss = _seg_sum(ak_sq, seg)
    yield

    lw = (-math.exp(-0.5) * math.log2(math.e)) * _sigmoid(w0_ref[...] + z_mm)
    lw_hi, lw_lo = _split(lw)
    a = _sigmoid(a0_ref[...] + a_mm)
    g_out[0, rs, :] = g_mm
    kk = kk * lax.rsqrt(jnp.maximum(kk_ss, 1e-24))
    bv = kk * a
    k = k * (1.0 + (a - 1.0) * ka_ref[...])
    rk_prod = r * k * rk_ref[...]
    yield
    q_out[0, rs, :] = (q * lax.rsqrt(q_ss * (1.0 / HEAD_DIM) + NORM_EPS) * (qg_ref[...] * (ATT_SCALE * LOG2E))).astype(BF16)
    ak_out[0, rs, :] = (ak * lax.rsqrt(ak_ss * (1.0 / HEAD_DIM) + NORM_EPS) * kg_ref[...]).astype(BF16)
    yield

    tri = tri_ref[...]
    cum = mm(tri, lw_hi) + mm(tri, lw_lo)
    bonus_ss = _seg_sum(rk_prod, seg)
    yield

    n_chunks = tm // CHUNK
    ends = [cum[(c + 1) * CHUNK - 1:(c + 1) * CHUNK, :] for c in range(n_chunks)]
    cl = jnp.concatenate([jnp.broadcast_to(e, (CHUNK, RWKV_W)) for e in ends], axis=0)
    for c in range(n_chunks):
        wl_out[0, c0 + c] = jnp.exp2(ends[c])
    bonus_out[0, rs, :] = bonus_ss * v
    yield
    e_neg = jnp.exp2(-cum)
    bt_out[0, rs, :] = (bv * e_neg).astype(BF16)
    kt_out[0, rs, :] = (k * e_neg).astype(BF16)
    yield
    e_hat = jnp.exp2(cl - cum)
    bh_out[0, rs, :] = (bv * e_hat).astype(BF16)
    kh_out[0, rs, :] = (k * e_hat).astype(BF16)
    yield
    at_out[0, rs, :] = (-kk * jnp.exp2(cum - lw)).astype(BF16)
    rt_out[0, rs, :] = (r * jnp.exp2(cum)).astype(BF16)


def _in_projection(x, mod3, g1, w_in, mu, w0, wlu, a0, alu, glu, k_k, k_a, r_k, qg, kg, seg, tri, to_cast):
    B, S, D = x.shape
    tm = min(IN_TILE, S)
    nj = S // tm
    n_steps = B * nj
    const = lambda shape: pl.BlockSpec(shape, lambda b, j: (0,) * len(shape))
    row_spec = lambda w: pl.BlockSpec((1, tm, w), lambda b, j: (b, j, 0))
    rows = lambda w, dt: jax.ShapeDtypeStruct((B, S, w), dt)

    def cast_spec(w):
        n = n_steps
        while n > 1 and (n_steps % n or w.shape[0] % (BF16_SUBLANES * n)):
            n -= 1
        per = n_steps // n
        return pl.BlockSpec((w.shape[0] // n, w.shape[1]), lambda b, j: ((b * nj + j) // per, 0))

    cast_specs = [cast_spec(w) for w in to_cast]
    return pl.pallas_call(
        _inproj_kernel,
        grid=(B, nj),
        in_specs=[row_spec(D),
                  pl.BlockSpec((1, N_MOD, D), lambda b, j: (b, 0, 0)),
                  const(g1.shape), const(w_in.shape), const(mu.shape), const(w0.shape),
                  const(wlu.shape), const(a0.shape), const(alu.shape), const(glu.shape),
                  const(k_k.shape), const(k_a.shape), const(r_k.shape), const(qg.shape),
                  const(kg.shape), const(seg.shape), const(tri.shape)] + cast_specs,
        out_specs=([row_spec(RWKV_W)] * 7
                   + [pl.BlockSpec((1, tm // CHUNK, 1, RWKV_W), lambda b, j: (b, j, 0, 0))]
                   + [row_spec(RWKV_W)] * 2
                   + [row_spec(ATT_Q_W), row_spec(ATT_KV_W), row_spec(ATT_KV_W)] + cast_specs),
        out_shape=([rows(RWKV_W, BF16)] * 7
                   + [jax.ShapeDtypeStruct((B, S // CHUNK, 1, RWKV_W), F32)]
                   + [rows(RWKV_W, F32)] * 2
                   + [rows(ATT_Q_W, BF16), rows(ATT_KV_W, BF16), rows(ATT_KV_W, BF16)]
                   + [jax.ShapeDtypeStruct(w.shape, BF16) for w in to_cast]),
        scratch_shapes=[pltpu.VMEM((1, RWKV_COLS), F32)],
        compiler_params=pltpu.CompilerParams(dimension_semantics=("arbitrary", "arbitrary"),
                                             vmem_limit_bytes=VMEM_LIMIT),
        name="in_projection",
    )(x, mod3, g1, w_in, mu, w0, wlu, a0, alu, glu, k_k, k_a, r_k, qg, kg, seg, tri, *to_cast)


def _mixer_kernel(at_ref, bt_ref, kt_ref, rt_ref, bh_ref, kh_ref, v_ref, wl_ref, bonus_ref, g_ref,
                  lg_ref, lb_ref, seg_ref, q_ref, kc_ref, kp_ref, vc_ref, vp_ref, bias_ref, sink_ref,
                  o_ref, oatt_ref, state_ref):
    @pl.when(pl.program_id(1) == 0)
    def _():
        state_ref[...] = jnp.zeros_like(state_ref)

    L, D, P = CHUNK, HEAD_DIM, 2 * HEAD_DIM
    n_batch = at_ref.shape[0]
    n_chunks = at_ref.shape[1] // L
    n_pairs = RWKV_W // P
    row = lax.broadcasted_iota(jnp.int32, (L, P), 0)
    lane = lax.broadcasted_iota(jnp.int32, (L, P), 1)
    low = lane < D
    s_idx = jnp.where(low, lane, lane - D)
    strict = row > s_idx
    incl = row >= s_idx
    zeros_lp = jnp.zeros((L, P), F32)
    own = (lambda t: jnp.where(low, t, jnp.zeros_like(t)), lambda t: jnp.where(low, jnp.zeros_like(t), t))
    other = (own[1], own[0])
    diag = (lane == row, lane == row + D)
    seg = seg_ref[...]

    chunks = [(bi, c) for c in range(n_chunks) for bi in range(n_batch)]
    items = [(ck, j, par) for ck in chunks for j in range(n_pairs) for par in range(2)]
    chunk_in, top, bot, v_o, x = {}, {}, {}, {}, {}

    def front():
        for ck in chunks:
            bi, c = ck
            rows = pl.ds(c * L, L)
            ci = dict(a_t=at_ref[bi, rows, :], b_t=bt_ref[bi, rows, :], k_t=kt_ref[bi, rows, :],
                      r_t=rt_ref[bi, rows, :], b_h=bh_ref[bi, rows, :], k_h=kh_ref[bi, rows, :],
                      v=v_ref[bi, rows, :], w_l=wl_ref[bi, c])
            chunk_in[ck] = ci
            for j in range(n_pairs):
                ps = slice(j * P, (j + 1) * P)
                atp, rtp = ci["a_t"][:, ps], ci["r_t"][:, ps]
                btp, ktp = ci["b_t"][:, ps], ci["k_t"][:, ps]
                s1 = _dot_nt(jnp.concatenate([atp, rtp], axis=0),
                             jnp.concatenate([own[0](btp), own[0](ktp), own[1](btp), own[1](ktp)], axis=0))
                v_swap = pltpu.roll(ci["v"][:, ps].astype(F32), D, 1)
                for par in range(2):
                    it = (ck, j, par)
                    top[it] = jnp.where(strict, s1[:L, P * par:P * (par + 1)], 0.0)
                    bot[it] = jnp.where(incl, s1[L:, P * par:P * (par + 1)], 0.0)
                    v_o[it] = other[par](v_swap)
                    x[it] = own[par](atp).astype(F32)
        yield
        for it in items:
            x[it] = x[it] + _dot(top[it], jnp.concatenate([zeros_lp, v_o[it]], axis=0))
        yield
        a_pow = {it: top[it][:, :D].astype(BF16) for it in items}
        n = 1
        while n < L:
            last = 2 * n >= L
            for it in items:
                xb = x[it].astype(BF16)
                if last:
                    x[it] = x[it] + jnp.dot(a_pow[it], xb, preferred_element_type=F32)
                else:
                    res = jnp.dot(a_pow[it], jnp.concatenate([xb, a_pow[it]], axis=1),
                                  preferred_element_type=F32)
                    x[it] = x[it] + res[:, :P]
                    a_pow[it] = res[:, P:].astype(BF16)
            n *= 2
            yield

    def back():
        rhs2 = {it: jnp.concatenate([x[it], v_o[it]], axis=0).astype(BF16) for it in items}
        rb = {it: jnp.dot(bot[it].astype(BF16), rhs2[it], preferred_element_type=F32) for it in items}
        mn = {}
        for ck in chunks:
            ci = chunk_in[ck]
            for j in range(n_pairs):
                ps = slice(j * P, (j + 1) * P)
                res = _dot_tn(jnp.concatenate([ci["b_h"][:, ps], ci["k_h"][:, ps]], axis=0),
                              jnp.concatenate([rhs2[(ck, j, 0)], rhs2[(ck, j, 1)]], axis=1))
                for par in range(2):
                    mn[(ck, j, par)] = res[D * par:D * (par + 1), P * par:P * (par + 1)]
        yield
        y = {}
        for ck in chunks:
            ci = chunk_in[ck]
            for j in range(n_pairs):
                ps = slice(j * P, (j + 1) * P)
                for par in range(2):
                    it = (ck, j, par)
                    h = (ck[0], 2 * j + par)
                    g_p = rb[it] + ci["r_t"][:, ps]
                    m_p = mn[it] + jnp.where(diag[par], ci["w_l"][:, ps], 0.0)
                    z = state_ref[h]
                    rhs = jnp.concatenate([z, zeros_lp] if par == 0 else [zeros_lp, z], axis=0)
                    res = _dot(jnp.concatenate([g_p, m_p], axis=0), rhs)
                    y[it] = res[:L] + rb[it]
                    state_ref[h] = other[par](res[L:] + mn[it])
            if ck[0] == n_batch - 1:
                yield
        for ck in chunks:
            bi, c = ck
            rows = pl.ds(c * L, L)
            yc = jnp.concatenate(
                [pltpu.roll(jnp.where(low, y[ck, j, 1], y[ck, j, 0]), D, 1) for j in range(n_pairs)], axis=1)
            mu = _seg_sum(yc, seg) * (1.0 / HEAD_DIM)
            yc = yc - mu
            var = _seg_sum(yc * yc, seg) * (1.0 / HEAD_DIM)
            yn = yc * lax.rsqrt(var + RWKV_GN_EPS) * lg_ref[...] + lb_ref[...]
            o_ref[bi, rows, :] = ((yn + bonus_ref[bi, rows, :]) * g_ref[bi, rows, :]).astype(BF16)
            if bi == n_batch - 1:
                yield

    swa = _swa_stages(q_ref, kc_ref, kp_ref, vc_ref, vp_ref, bias_ref, sink_ref, oatt_ref)
    next(swa)
    for _ in front():
        next(swa, None)
    for _ in back():
        next(swa, None)
    for _ in swa:
        pass


def _mixers(at, bt, kt, rt, bh, kh, v, wl, bonus, g, lnx_g, lnx_b, seg, q, ak, av, bias, sinks):
    B, S, W = at.shape
    tc = min(MIX_TILE, S)
    nbat = MIX_BATCH if B % MIX_BATCH == 0 else 1
    nb = tc // ATT_BLOCK
    bias = bias.reshape(N_KV_HEADS, GQA_GROUP * ATT_BLOCK, ATT_BLOCK)
    const = lambda shape: pl.BlockSpec(shape, lambda b, j: (0,) * len(shape))
    rows = lambda w: pl.BlockSpec((nbat, tc, w), lambda b, j: (b, j, 0))
    prev_block = lambda w: pl.BlockSpec((nbat, ATT_BLOCK, w), lambda b, j: (b, jnp.maximum(j * nb - 1, 0), 0))
    return pl.pallas_call(
        _mixer_kernel,
        grid=(B // nbat, S // tc),
        in_specs=([rows(W)] * 7 + [pl.BlockSpec((nbat, tc // CHUNK, 1, W), lambda b, j: (b, j, 0, 0))]
                  + [rows(W)] * 2 + [const(lnx_g.shape), const(lnx_b.shape), const(seg.shape)]
                  + [rows(ATT_Q_W), rows(ATT_KV_W), prev_block(ATT_KV_W), rows(ATT_KV_W), prev_block(ATT_KV_W),
                     const(bias.shape), pl.BlockSpec(memory_space=pltpu.SMEM)]),
        out_specs=[rows(W), rows(ATT_Q_W)],
        out_shape=[jax.ShapeDtypeStruct((B, S, W), BF16), jax.ShapeDtypeStruct((B, S, ATT_Q_W), BF16)],
        scratch_shapes=[pltpu.VMEM((nbat, RWKV_HEADS, CHUNK, 2 * HEAD_DIM), F32)],
        compiler_params=pltpu.CompilerParams(dimension_semantics=("arbitrary", "arbitrary"),
                                             vmem_limit_bytes=VMEM_LIMIT),
        name="rwkv7_swa_mixers",
    )(at, bt, kt, rt, bh, kh, v, wl, bonus, g, lnx_g, lnx_b, seg, q, ak, ak, av, av, bias, sinks)


def _swa_stages(q_ref, kc_ref, kp_ref, vc_ref, vp_ref, bias_ref, sink_ref, o_ref):
    first = pl.program_id(1) == 0
    rows = GQA_GROUP * ATT_BLOCK
    col = lax.broadcasted_iota(jnp.int32, (rows, ATT_BLOCK), 1)
    qpos = lax.broadcasted_iota(jnp.int32, (rows, ATT_BLOCK), 0) % ATT_BLOCK
    use_cur = col <= qpos
    no_key = jnp.logical_and(first, jnp.logical_not(use_cur))
    row = lax.broadcasted_iota(jnp.int32, (rows, 1), 0)
    sink = []
    for hk in range(N_KV_HEADS):
        s = jnp.full((rows, 1), sink_ref[hk * GQA_GROUP], F32)
        for g in range(1, GQA_GROUP):
            s = jnp.where(row >= g * ATT_BLOCK, sink_ref[hk * GQA_GROUP + g], s)
        sink.append(s * LOG2E)

    def window(ref_cur, ref_prev, bi, blk, hk):
        ks = slice(hk * HEAD_DIM, (hk + 1) * HEAD_DIM)
        prev = (ref_prev[bi, :, ks] if blk == 0
                else ref_cur[bi, pl.ds((blk - 1) * ATT_BLOCK, ATT_BLOCK), ks])
        return jnp.concatenate([prev, ref_cur[bi, pl.ds(blk * ATT_BLOCK, ATT_BLOCK), ks]], axis=0)

    items = [(bi, blk, hk) for bi in range(q_ref.shape[0]) for blk in range(q_ref.shape[1] // ATT_BLOCK)
             for hk in range(N_KV_HEADS)]
    logits = {}
    for bi, blk, hk in items:
        q = q_ref[bi, pl.ds(blk * ATT_BLOCK, ATT_BLOCK), :]
        qg = jnp.concatenate([q[:, (hk * GQA_GROUP + g) * HEAD_DIM:(hk * GQA_GROUP + g + 1) * HEAD_DIM]
                              for g in range(GQA_GROUP)], axis=0)
        lg = lax.dot_general(qg, window(kc_ref, kp_ref, bi, blk, hk), (((1,), (1,)), ((), ())),
                             preferred_element_type=F32)
        lg = jnp.where(use_cur, lg[:, ATT_BLOCK:], lg[:, :ATT_BLOCK]) + bias_ref[hk]
        logits[bi, blk, hk] = jnp.where(no_key, NEG_INF, lg) if blk == 0 else lg
    yield

    e2, e_sink = {}, {}
    for it in items:
        m = jnp.maximum(jnp.max(logits[it], axis=-1, keepdims=True), sink[it[2]])
        eb = jnp.exp2(logits.pop(it) - m).astype(BF16)
        e_sink[it] = jnp.exp2(sink[it[2]] - m)
        zero = jnp.zeros_like(eb)
        e2[it] = jnp.concatenate([jnp.where(use_cur, zero, eb), jnp.where(use_cur, eb, zero)], axis=1)
        yield

    low = lax.broadcasted_iota(jnp.int32, (ATT_BLOCK, 2 * HEAD_DIM), 1) < HEAD_DIM
    for it in items:
        bi, blk, hk = it
        v2 = window(vc_ref, vp_ref, bi, blk, hk)
        vw = jnp.concatenate([v2, v2, jnp.ones((2 * ATT_BLOCK, 2 * HEAD_DIM), BF16)], axis=1)
        res = jnp.dot(e2.pop(it), vw, preferred_element_type=F32)
        out = res[:, :2 * HEAD_DIM] / (res[:, 2 * HEAD_DIM:] + e_sink.pop(it))
        for g in range(0, GQA_GROUP, 2):
            h = hk * GQA_GROUP + g
            pair = jnp.where(low, out[g * ATT_BLOCK:(g + 1) * ATT_BLOCK], out[(g + 1) * ATT_BLOCK:(g + 2) * ATT_BLOCK])
            o_ref[bi, pl.ds(blk * ATT_BLOCK, ATT_BLOCK), h * HEAD_DIM:(h + 2) * HEAD_DIM] = pair.astype(BF16)
        yield


def _ffn_kernel(x_ref, yr_ref, ya_ref, mod_ref, g2_ref, wo_ref, wg_ref, wu_ref, wd_ref, o_ref):
    n_sub = max(x_ref.shape[1] // FFN_SUB, 1)
    tm = x_ref.shape[1] // n_sub
    subs = [_ffn_rows(pl.ds(s * tm, tm), x_ref, yr_ref, ya_ref, mod_ref, g2_ref, wo_ref, wg_ref, wu_ref,
                      wd_ref, o_ref) for s in range(n_sub)]
    done = [False] * n_sub
    t = 0
    while not all(done):
        for s in range(min(t + 1, n_sub)):
            if not done[s]:
                done[s] = next(subs[s], True) is True
        t += 1


def _ffn_rows(rs, x_ref, yr_ref, ya_ref, mod_ref, g2_ref, wo_ref, wg_ref, wu_ref, wd_ref, o_ref):
    x = x_ref[0, rs, :]
    wo = wo_ref[...]
    mix = (jnp.dot(yr_ref[0, rs, :], wo[:RWKV_W], preferred_element_type=F32)
           + jnp.dot(ya_ref[0, rs, :], wo[RWKV_W:], preferred_element_type=F32))
    yield
    h_res = x + mod_ref[0, 2:3, :] * mix
    ms = jnp.mean(h_res * h_res, axis=-1, keepdims=True)
    h2 = (h_res * lax.rsqrt(ms + NORM_EPS) * (g2_ref[...] * (1.0 + mod_ref[0, 4:5, :]))
          + mod_ref[0, 3:4, :]).astype(BF16)
    yield
    d_ff = wg_ref.shape[1]
    ffn = jnp.zeros_like(x)
    for lo in range(0, d_ff, FFN_BLOCK):
        cs = slice(lo, min(lo + FFN_BLOCK, d_ff))
        gt = jnp.dot(h2, wg_ref[:, cs], preferred_element_type=F32)
        up = jnp.dot(h2, wu_ref[:, cs], preferred_element_type=F32)
        yield
        act = (gt * _sigmoid(gt) * up).astype(BF16)
        ffn = ffn + jnp.dot(act, wd_ref[cs, :], preferred_element_type=F32)
        yield
    o_ref[0, rs, :] = h_res + mod_ref[0, 5:6, :] * ffn


def _out_ffn(x, y_rwkv, y_att, mod3, g2, w_out, w_gate, w_up, w_down):
    B, S, D = x.shape
    tm = min(FFN_TILE, S)
    resident = lambda shape: pl.BlockSpec(shape, lambda b, j: (0,) * len(shape),
                                          pipeline_mode=pl.Buffered(1))
    row_spec = lambda w: pl.BlockSpec((1, tm, w), lambda b, j: (b, j, 0))
    return pl.pallas_call(
        _ffn_kernel,
        grid=(B, S // tm),
        in_specs=[row_spec(D), row_spec(RWKV_W), row_spec(ATT_Q_W),
                  pl.BlockSpec((1, N_MOD, D), lambda b, j: (b, 0, 0)),
                  resident(g2.shape), resident(w_out.shape), resident(w_gate.shape),
                  resident(w_up.shape), resident(w_down.shape)],
        out_specs=row_spec(D),
        out_shape=jax.ShapeDtypeStruct((B, S, D), F32),
        compiler_params=pltpu.CompilerParams(dimension_semantics=("arbitrary", "arbitrary"),
                                             vmem_limit_bytes=VMEM_LIMIT),
        name="out_proj_ffn",
    )(x, y_rwkv, y_att, mod3, g2, w_out, w_gate, w_up, w_down)


def kernel(x, c, w_ada, b_ada, norm1_g, w_in, rwkv_mu, w0, w_lora_up, a0, a_lora_up, g_lora_up, k_k, k_a, r_k, lnx_g, lnx_b, q_norm_g, k_norm_g, sinks, rel_bias, w_out, norm2_g, w_gate, w_up, w_down):
    B, S, D = x.shape
    row = lambda t: t.reshape(1, -1).astype(F32)
    lane = jnp.arange(MXU_TILE) // HEAD_DIM
    seg = (lane[:, None] == lane[None, :]).astype(BF16)
    t_idx = jnp.arange(min(IN_SUB, S))
    tri = ((t_idx[:, None] >= t_idx[None, :])
           & (t_idx[:, None] // CHUNK == t_idx[None, :] // CHUNK)).astype(BF16)

    mod3 = _modulation(c.astype(F32), w_ada, b_ada).reshape(B, N_MOD, D)
    bias = _bias_table(rel_bias)

    (at, bt, kt, rt, bh, kh, v, wl, bonus, g, q, ak, av,
     w_out16, w_gate16, w_up16, w_down16) = _in_projection(
        x, mod3, row(norm1_g), w_in.astype(BF16), row(rwkv_mu), row(w0), w_lora_up.astype(BF16),
        row(a0), a_lora_up.astype(BF16), g_lora_up.astype(BF16), row(k_k), row(k_a), row(r_k),
        row(jnp.tile(q_norm_g, N_Q_HEADS)), row(jnp.tile(k_norm_g, N_KV_HEADS)), seg, tri,
        (w_out.astype(F32), w_gate.astype(F32), w_up.astype(F32), w_down.astype(F32)))

    y_rwkv, y_att = _mixers(at, bt, kt, rt, bh, kh, v, wl, bonus, g, row(lnx_g), row(lnx_b), seg,
                            q, ak, av, bias, sinks)

    out = _out_ffn(x, y_rwkv, y_att, mod3, row(norm2_g), w_out16, w_gate16, w_up16, w_down16)
    return out.astype(x.dtype)
```

```python
import math

import jax
import jax.numpy as jnp
from jax import lax
from jax.experimental import pallas as pl
from jax.experimental.pallas import tpu as pltpu

F32 = jnp.float32
BF16 = jnp.bfloat16

HEAD_DIM = 64
RWKV_HEADS = 8
RWKV_W = RWKV_HEADS * HEAD_DIM
DECAY_LORA = 64
AAA_LORA = 64
GATE_LORA = 128
RWKV_COLS = 3 * RWKV_W + DECAY_LORA + AAA_LORA + GATE_LORA
RWKV_GN_EPS = 64e-5
N_Q_HEADS = 8
N_KV_HEADS = 2
GQA_GROUP = N_Q_HEADS // N_KV_HEADS
ATT_Q_W = N_Q_HEADS * HEAD_DIM
ATT_KV_W = N_KV_HEADS * HEAD_DIM
ATT_BLOCK = 128
WINDOW = 128
ATT_SCALE = 1.0 / math.sqrt(HEAD_DIM)
LOG2E = math.log2(math.e)
NUM_BUCKETS = 32
MAX_DISTANCE = 128
N_MOD = 6
NORM_EPS = 1e-6
NEG_INF = -1e30

CHUNK = 64
IN_TILE = 1024
IN_SUB = 256
MIX_TILE = 512
MIX_BATCH = 1
FFN_TILE = 512
FFN_SUB = 256
MXU_TILE = 256
F32_SUBLANES = 8
BF16_SUBLANES = 16
FFN_BLOCK = 4 * MXU_TILE
VMEM_LIMIT = 56 * 1024 * 1024


def _dot(a, b):
    return jnp.dot(a.astype(BF16), b.astype(BF16), preferred_element_type=F32)


def _dot_nt(a, b):
    return lax.dot_general(a.astype(BF16), b.astype(BF16), (((1,), (1,)), ((), ())),
                           preferred_element_type=F32)


def _dot_tn(a, b):
    return lax.dot_general(a.astype(BF16), b.astype(BF16), (((0,), (0,)), ((), ())),
                           preferred_element_type=F32)


def _split(x):
    hi = x.astype(BF16)
    lo = (x - hi.astype(F32)).astype(BF16)
    return hi, lo


def _seg_sum(x, seg):
    m, w = x.shape
    xb = x.astype(BF16)
    blk = seg.shape[0]
    if w <= blk:
        return jnp.dot(xb, seg[:w, :w], preferred_element_type=F32)
    n = w // blk
    stacked = jnp.concatenate([xb[:, i * blk:(i + 1) * blk] for i in range(n)], axis=0)
    res = jnp.dot(stacked, seg, preferred_element_type=F32)
    return jnp.concatenate([res[i * m:(i + 1) * m] for i in range(n)], axis=1)


def _sigmoid(z):
    return 0.5 * jnp.tanh(0.5 * z) + 0.5


def _mod_kernel(c_ref, w_ref, b_ref, o_ref):
    c = c_ref[...]
    s = c * _sigmoid(c)
    s_hi, s_lo = _split(s)
    w = w_ref[...]
    w_hi, w_lo = _split(w)
    d = lambda x, y: jnp.dot(x, y, preferred_element_type=F32)
    n = c.shape[0]
    hh = d(jnp.concatenate([s_hi, s_lo], axis=0), w_hi)
    o_ref[...] = hh[:n] + d(s_hi, w_lo) + hh[n:] + b_ref[...]


def _modulation(c, w_ada, b_ada):
    B, D = c.shape
    n = w_ada.shape[1]
    blk = D
    return pl.pallas_call(
        _mod_kernel,
        grid=(n // blk,),
        in_specs=[pl.BlockSpec((B, D), lambda j: (0, 0)),
                  pl.BlockSpec((D, blk), lambda j: (0, j)),
                  pl.BlockSpec((1, blk), lambda j: (0, j))],
        out_specs=pl.BlockSpec((B, blk), lambda j: (0, j)),
        out_shape=jax.ShapeDtypeStruct((B, n), F32),
        name="adaln_mod",
    )(c, w_ada, b_ada.reshape(1, n))


def _t5_thresholds():
    max_exact = NUM_BUCKETS // 2
    out = []
    for k in range(1, NUM_BUCKETS - max_exact):
        edge = max_exact * (MAX_DISTANCE / max_exact) ** (k / (NUM_BUCKETS - max_exact))
        assert abs(edge - round(edge)) > 1e-6, "bucket edge on an integer distance: floor() would be ambiguous"
        out.append(math.ceil(edge))
    return tuple(out)


_T5_THRESHOLDS = _t5_thresholds()


def _bias_kernel(rb_ref, o_ref):
    qi = lax.broadcasted_iota(jnp.int32, (ATT_BLOCK, ATT_BLOCK), 0)
    kj = lax.broadcasted_iota(jnp.int32, (ATT_BLOCK, ATT_BLOCK), 1)
    n = jnp.where(kj <= qi, qi - kj, qi + ATT_BLOCK - kj)
    max_exact = NUM_BUCKETS // 2
    large = jnp.full(n.shape, max_exact, jnp.int32)
    for t in _T5_THRESHOLDS:
        large = large + jnp.where(n >= t, 1, 0)
    bucket = jnp.where(n < max_exact, n, large)
    for h in range(N_Q_HEADS):
        acc = jnp.zeros((ATT_BLOCK, ATT_BLOCK), F32)
        for b in range(NUM_BUCKETS):
            acc = jnp.where(bucket == b, rb_ref[b, h], acc)
        o_ref[h] = acc * LOG2E


def _bias_table(rel_bias):
    assert WINDOW == ATT_BLOCK
    return pl.pallas_call(
        _bias_kernel,
        in_specs=[pl.BlockSpec(memory_space=pltpu.SMEM)],
        out_specs=pl.BlockSpec(memory_space=pltpu.VMEM),
        out_shape=jax.ShapeDtypeStruct((N_Q_HEADS, ATT_BLOCK, ATT_BLOCK), F32),
        name="rel_bias_table",
    )(rel_bias)


def _inproj_kernel(x_ref, mod_ref, g1_ref, win_ref, mu_ref, w0_ref, wlu_ref, a0_ref, alu_ref,
                   glu_ref, kk_ref, ka_ref, rk_ref, qg_ref, kg_ref, seg_ref, tri_ref, *rest):
    n_cast = (len(rest) - 14) // 2
    cast_in, rest = rest[:n_cast], rest[n_cast:]
    (at_out, bt_out, kt_out, rt_out, bh_out, kh_out, v_out, wl_out, bonus_out, g_out,
     q_out, ak_out, av_out) = rest[:13]
    cast_out, carry_ref = rest[13:13 + n_cast], rest[-1]

    @pl.when(pl.program_id(1) == 0)
    def _():
        carry_ref[...] = jnp.zeros_like(carry_ref)

    for src, dst in zip(cast_in, cast_out):
        dst[...] = src[...].astype(BF16)

    tm = tri_ref.shape[0]
    refs = (mu_ref, w0_ref, wlu_ref, a0_ref, alu_ref, glu_ref, kk_ref, ka_ref, rk_ref, qg_ref, kg_ref,
            seg_ref, tri_ref)
    outs = (at_out, bt_out, kt_out, rt_out, bh_out, kh_out, v_out, wl_out, bonus_out, g_out, q_out,
            ak_out, av_out)
    tail = iter(())
    gain = g1_ref[...] * (1.0 + mod_ref[0, 1:2, :])
    for s in range(x_ref.shape[1] // tm):
        rs = pl.ds(s * tm, tm)
        x = x_ref[0, rs, :]
        ms = jnp.mean(x * x, axis=-1, keepdims=True)
        h = (x * lax.rsqrt(ms + NORM_EPS) * gain + mod_ref[0, 0:1, :]).astype(BF16)
        pieces = []
        for lo in range(0, win_ref.shape[1], MXU_TILE):
            pieces.append(jnp.dot(h, win_ref[:, lo:lo + MXU_TILE], preferred_element_type=F32))
            next(tail, None)
        for _ in tail:
            pass
        tail = _inproj_tail(jnp.concatenate(pieces, axis=1), rs, s * (tm // CHUNK), refs, outs, carry_ref)
    for _ in tail:
        pass


def _inproj_tail(p, rs, c0, refs, outs, carry_ref):
    (mu_ref, w0_ref, wlu_ref, a0_ref, alu_ref, glu_ref, kk_ref, ka_ref, rk_ref, qg_ref, kg_ref,
     seg_ref, tri_ref) = refs
    (at_out, bt_out, kt_out, rt_out, bh_out, kh_out, v_out, wl_out, bonus_out, g_out, q_out,
     ak_out, av_out) = outs
    tm = p.shape[0]

    mm = lambda a, b: jnp.dot(a, b, preferred_element_type=F32)

    pr = p[:, :RWKV_COLS]
    rolled = pltpu.roll(pr, 1, 0)
    first = lax.broadcasted_iota(jnp.int32, (F32_SUBLANES, RWKV_COLS), 0) == 0
    prev = jnp.concatenate([jnp.where(first, carry_ref[...], rolled[:F32_SUBLANES]), rolled[F32_SUBLANES:]],
                           axis=0)
    carry_ref[...] = pr[tm - 1:tm, :]
    pm = pr + (prev - pr) * mu_ref[...]
    r = pm[:, 0:RWKV_W]
    k = pm[:, RWKV_W:2 * RWKV_W]
    v = pm[:, 2 * RWKV_W:3 * RWKV_W]
    o = 3 * RWKV_W
    tanh_w = jnp.tanh(pm[:, o:o + DECAY_LORA]).astype(BF16)
    xa = pm[:, o + DECAY_LORA:o + DECAY_LORA + AAA_LORA].astype(BF16)
    sig_g = _sigmoid(pm[:, o + DECAY_LORA + AAA_LORA:RWKV_COLS]).astype(BF16)
    yield
    q = p[:, RWKV_COLS:RWKV_COLS + ATT_Q_W]
    ak = p[:, RWKV_COLS + ATT_Q_W:RWKV_COLS + ATT_Q_W + ATT_KV_W]
    av = p[:, RWKV_COLS + ATT_Q_W + ATT_KV_W:]
    kk = k * kk_ref[...]
    kk_sq, q_sq, ak_sq = kk * kk, q * q, ak * ak
    v_out[0, rs, :] = v.astype(BF16)
    av_out[0, rs, :] = av.astype(BF16)
    yield

    seg = seg_ref[...]
    z_mm = mm(tanh_w, wlu_ref[...])
    a_mm = mm(xa, alu_ref[...])
    g_mm = mm(sig_g, glu_ref[...])
    kk_ss = _seg_sum(kk_sq, seg)
    q_ss = _seg_sum(q_sq, seg)
    ak_ss = _seg_sum(ak_sq, seg)
    yield

    lw = (-math.exp(-0.5) * math.log2(math.e)) * _sigmoid(w0_ref[...] + z_mm)
    lw_hi, lw_lo = _split(lw)
    a = _sigmoid(a0_ref[...] + a_mm)
    g_out[0, rs, :] = g_mm
    kk = kk * lax.rsqrt(jnp.maximum(kk_ss, 1e-24))
    bv = kk * a
    k = k * (1.0 + (a - 1.0) * ka_ref[...])
    rk_prod = r * k * rk_ref[...]
    yield
    q_out[0, rs, :] = (q * lax.rsqrt(q_ss * (1.0 / HEAD_DIM) + NORM_EPS) * (qg_ref[...] * (ATT_SCALE * LOG2E))).astype(BF16)
    ak_out[0, rs, :] = (ak * lax.rsqrt(ak_ss * (1.0 / HEAD_DIM) + NORM_EPS) * kg_ref[...]).astype(BF16)
    yield

    tri = tri_ref[...]
    cum = mm(tri, lw_hi) + mm(tri, lw_lo)
    bonus_ss = _seg_sum(rk_prod, seg)
    yield

    n_chunks = tm // CHUNK
    ends = [cum[(c + 1) * CHUNK - 1:(c + 1) * CHUNK, :] for c in range(n_chunks)]
    cl = jnp.concatenate([jnp.broadcast_to(e, (CHUNK, RWKV_W)) for e in ends], axis=0)
    for c in range(n_chunks):
        wl_out[0, c0 + c] = jnp.exp2(ends[c])
    bonus_out[0, rs, :] = bonus_ss * v
    yield
    e_neg = jnp.exp2(-cum)
    bt_out[0, rs, :] = (bv * e_neg).astype(BF16)
    kt_out[0, rs, :] = (k * e_neg).astype(BF16)
    yield
    e_hat = jnp.exp2(cl - cum)
    bh_out[0, rs, :] = (bv * e_hat).astype(BF16)
    kh_out[0, rs, :] = (k * e_hat).astype(BF16)
    yield
    at_out[0, rs, :] = (-kk * jnp.exp2(cum - lw)).astype(BF16)
    rt_out[0, rs, :] = (r * jnp.exp2(cum)).astype(BF16)


def _in_projection(x, mod3, g1, w_in, mu, w0, wlu, a0, alu, glu, k_k, k_a, r_k, qg, kg, seg, tri, to_cast):
    B, S, D = x.shape
    tm = min(IN_TILE, S)
    nj = S // tm
    n_steps = B * nj
    const = lambda shape: pl.BlockSpec(shape, lambda b, j: (0,) * len(shape))
    row_spec = lambda w: pl.BlockSpec((1, tm, w), lambda b, j: (b, j, 0))
    rows = lambda w, dt: jax.ShapeDtypeStruct((B, S, w), dt)

    def cast_spec(w):
        n = n_steps
        while n > 1 and (n_steps % n or w.shape[0] % (BF16_SUBLANES * n)):
            n -= 1
        per = n_steps // n
        return pl.BlockSpec((w.shape[0] // n, w.shape[1]), lambda b, j: ((b * nj + j) // per, 0))

    cast_specs = [cast_spec(w) for w in to_cast]
    return pl.pallas_call(
        _inproj_kernel,
        grid=(B, nj),
        in_specs=[row_spec(D),
                  pl.BlockSpec((1, N_MOD, D), lambda b, j: (b, 0, 0)),
                  const(g1.shape), const(w_in.shape), const(mu.shape), const(w0.shape),
                  const(wlu.shape), const(a0.shape), const(alu.shape), const(glu.shape),
                  const(k_k.shape), const(k_a.shape), const(r_k.shape), const(qg.shape),
                  const(kg.shape), const(seg.shape), const(tri.shape)] + cast_specs,
        out_specs=([row_spec(RWKV_W)] * 7
                   + [pl.BlockSpec((1, tm // CHUNK, 1, RWKV_W), lambda b, j: (b, j, 0, 0))]
                   + [row_spec(RWKV_W)] * 2
                   + [row_spec(ATT_Q_W), row_spec(ATT_KV_W), row_spec(ATT_KV_W)] + cast_specs),
        out_shape=([rows(RWKV_W, BF16)] * 7
                   + [jax.ShapeDtypeStruct((B, S // CHUNK, 1, RWKV_W), F32)]
                   + [rows(RWKV_W, F32)] * 2
                   + [rows(ATT_Q_W, BF16), rows(ATT_KV_W, BF16), rows(ATT_KV_W, BF16)]
                   + [jax.ShapeDtypeStruct(w.shape, BF16) for w in to_cast]),
        scratch_shapes=[pltpu.VMEM((1, RWKV_COLS), F32)],
        compiler_params=pltpu.CompilerParams(dimension_semantics=("arbitrary", "arbitrary"),
                                             vmem_limit_bytes=VMEM_LIMIT),
        name="in_projection",
    )(x, mod3, g1, w_in, mu, w0, wlu, a0, alu, glu, k_k, k_a, r_k, qg, kg, seg, tri, *to_cast)


def _mixer_kernel(at_ref, bt_ref, kt_ref, rt_ref, bh_ref, kh_ref, v_ref, wl_ref, bonus_ref, g_ref,
                  lg_ref, lb_ref, seg_ref, q_ref, kc_ref, kp_ref, vc_ref, vp_ref, bias_ref, sink_ref,
                  o_ref, oatt_ref, state_ref):
    @pl.when(pl.program_id(1) == 0)
    def _():
        state_ref[...] = jnp.zeros_like(state_ref)

    L, D, P = CHUNK, HEAD_DIM, 2 * HEAD_DIM
    n_batch = at_ref.shape[0]
    n_chunks = at_ref.shape[1] // L
    n_pairs = RWKV_W // P
    row = lax.broadcasted_iota(jnp.int32, (L, P), 0)
    lane = lax.broadcasted_iota(jnp.int32, (L, P), 1)
    low = lane < D
    s_idx = jnp.where(low, lane, lane - D)
    strict = row > s_idx
    incl = row >= s_idx
    zeros_lp = jnp.zeros((L, P), F32)
    own = (lambda t: jnp.where(low, t, jnp.zeros_like(t)), lambda t: jnp.where(low, jnp.zeros_like(t), t))
    other = (own[1], own[0])
    diag = (lane == row, lane == row + D)
    seg = seg_ref[...]

    chunks = [(bi, c) for c in range(n_chunks) for bi in range(n_batch)]
    items = [(ck, j, par) for ck in chunks for j in range(n_pairs) for par in range(2)]
    chunk_in, top, bot, v_o, x = {}, {}, {}, {}, {}

    def front():
        for ck in chunks:
            bi, c = ck
            rows = pl.ds(c * L, L)
            ci = dict(a_t=at_ref[bi, rows, :], b_t=bt_ref[bi, rows, :], k_t=kt_ref[bi, rows, :],
                      r_t=rt_ref[bi, rows, :], b_h=bh_ref[bi, rows, :], k_h=kh_ref[bi, rows, :],
                      v=v_ref[bi, rows, :], w_l=wl_ref[bi, c])
            chunk_in[ck] = ci
            for j in range(n_pairs):
                ps = slice(j * P, (j + 1) * P)
                atp, rtp = ci["a_t"][:, ps], ci["r_t"][:, ps]
                btp, ktp = ci["b_t"][:, ps], ci["k_t"][:, ps]
                s1 = _dot_nt(jnp.concatenate([atp, rtp], axis=0),
                             jnp.concatenate([own[0](btp), own[0](ktp), own[1](btp), own[1](ktp)], axis=0))
                v_swap = pltpu.roll(ci["v"][:, ps].astype(F32), D, 1)
                for par in range(2):
                    it = (ck, j, par)
                    top[it] = jnp.where(strict, s1[:L, P * par:P * (par + 1)], 0.0)
                    bot[it] = jnp.where(incl, s1[L:, P * par:P * (par + 1)], 0.0)
                    v_o[it] = other[par](v_swap)
                    x[it] = own[par](atp).astype(F32)
        yield
        for it in items:
            x[it] = x[it] + _dot(top[it], jnp.concatenate([zeros_lp, v_o[it]], axis=0))
        yield
        a_pow = {it: top[it][:, :D].astype(BF16) for it in items}
        n = 1
        while n < L:
            last = 2 * n >= L
            for it in items:
                xb = x[it].astype(BF16)
                if last:
                    x[it] = x[it] + jnp.dot(a_pow[it], xb, preferred_element_type=F32)
                else:
                    res = jnp.dot(a_pow[it], jnp.concatenate([xb, a_pow[it]], axis=1),
                                  preferred_element_type=F32)
                    x[it] = x[it] + res[:, :P]
                    a_pow[it] = res[:, P:].astype(BF16)
            n *= 2
            yield

    def back():
        rhs2 = {it: jnp.concatenate([x[it], v_o[it]], axis=0).astype(BF16) for it in items}
        rb = {it: jnp.dot(bot[it].astype(BF16), rhs2[it], preferred_element_type=F32) for it in items}
        mn = {}
        for ck in chunks:
            ci = chunk_in[ck]
            for j in range(n_pairs):
                ps = slice(j * P, (j + 1) * P)
                res = _dot_tn(jnp.concatenate([ci["b_h"][:, ps], ci["k_h"][:, ps]], axis=0),
                              jnp.concatenate([rhs2[(ck, j, 0)], rhs2[(ck, j, 1)]], axis=1))
                for par in range(2):
                    mn[(ck, j, par)] = res[D * par:D * (par + 1), P * par:P * (par + 1)]
        yield
        y = {}
        for ck in chunks:
            ci = chunk_in[ck]
            for j in range(n_pairs):
                ps = slice(j * P, (j + 1) * P)
                for par in range(2):
                    it = (ck, j, par)
                    h = (ck[0], 2 * j + par)
                    g_p = rb[it] + ci["r_t"][:, ps]
                    m_p = mn[it] + jnp.where(diag[par], ci["w_l"][:, ps], 0.0)
                    z = state_ref[h]
                    rhs = jnp.concatenate([z, zeros_lp] if par == 0 else [zeros_lp, z], axis=0)
                    res = _dot(jnp.concatenate([g_p, m_p], axis=0), rhs)
                    y[it] = res[:L] + rb[it]
                    state_ref[h] = other[par](res[L:] + mn[it])
            if ck[0] == n_batch - 1:
                yield
        for ck in chunks:
            bi, c = ck
            rows = pl.ds(c * L, L)
            yc = jnp.concatenate(
                [pltpu.roll(jnp.where(low, y[ck, j, 1], y[ck, j, 0]), D, 1) for j in range(n_pairs)], axis=1)
            mu = _seg_sum(yc, seg) * (1.0 / HEAD_DIM)
            yc = yc - mu
            var = _seg_sum(yc * yc, seg) * (1.0 / HEAD_DIM)
            yn = yc * lax.rsqrt(var + RWKV_GN_EPS) * lg_ref[...] + lb_ref[...]
            o_ref[bi, rows, :] = ((yn + bonus_ref[bi, rows, :]) * g_ref[bi, rows, :]).astype(BF16)
            if bi == n_batch - 1:
                yield

    swa = _swa_stages(q_ref, kc_ref, kp_ref, vc_ref, vp_ref, bias_ref, sink_ref, oatt_ref)
    next(swa)
    for _ in front():
        next(swa, None)
    for _ in back():
        next(swa, None)
    for _ in swa:
        pass


def _mixers(at, bt, kt, rt, bh, kh, v, wl, bonus, g, lnx_g, lnx_b, seg, q, ak, av, bias, sinks):
    B, S, W = at.shape
    tc = min(MIX_TILE, S)
    nbat = MIX_BATCH if B % MIX_BATCH == 0 else 1
    nb = tc // ATT_BLOCK
    bias = bias.reshape(N_KV_HEADS, GQA_GROUP * ATT_BLOCK, ATT_BLOCK)
    const = lambda shape: pl.BlockSpec(shape, lambda b, j: (0,) * len(shape))
    rows = lambda w: pl.BlockSpec((nbat, tc, w), lambda b, j: (b, j, 0))
    prev_block = lambda w: pl.BlockSpec((nbat, ATT_BLOCK, w), lambda b, j: (b, jnp.maximum(j * nb - 1, 0), 0))
    return pl.pallas_call(
        _mixer_kernel,
        grid=(B // nbat, S // tc),
        in_specs=([rows(W)] * 7 + [pl.BlockSpec((nbat, tc // CHUNK, 1, W), lambda b, j: (b, j, 0, 0))]
                  + [rows(W)] * 2 + [const(lnx_g.shape), const(lnx_b.shape), const(seg.shape)]
                  + [rows(ATT_Q_W), rows(ATT_KV_W), prev_block(ATT_KV_W), rows(ATT_KV_W), prev_block(ATT_KV_W),
                     const(bias.shape), pl.BlockSpec(memory_space=pltpu.SMEM)]),
        out_specs=[rows(W), rows(ATT_Q_W)],
        out_shape=[jax.ShapeDtypeStruct((B, S, W), BF16), jax.ShapeDtypeStruct((B, S, ATT_Q_W), BF16)],
        scratch_shapes=[pltpu.VMEM((nbat, RWKV_HEADS, CHUNK, 2 * HEAD_DIM), F32)],
        compiler_params=pltpu.CompilerParams(dimension_semantics=("arbitrary", "arbitrary"),
                                             vmem_limit_bytes=VMEM_LIMIT),
        name="rwkv7_swa_mixers",
    )(at, bt, kt, rt, bh, kh, v, wl, bonus, g, lnx_g, lnx_b, seg, q, ak, ak, av, av, bias, sinks)


def _swa_stages(q_ref, kc_ref, kp_ref, vc_ref, vp_ref, bias_ref, sink_ref, o_ref):
    first = pl.program_id(1) == 0
    rows = GQA_GROUP * ATT_BLOCK
    col = lax.broadcasted_iota(jnp.int32, (rows, ATT_BLOCK), 1)
    qpos = lax.broadcasted_iota(jnp.int32, (rows, ATT_BLOCK), 0) % ATT_BLOCK
    use_cur = col <= qpos
    no_key = jnp.logical_and(first, jnp.logical_not(use_cur))
    row = lax.broadcasted_iota(jnp.int32, (rows, 1), 0)
    sink = []
    for hk in range(N_KV_HEADS):
        s = jnp.full((rows, 1), sink_ref[hk * GQA_GROUP], F32)
        for g in range(1, GQA_GROUP):
            s = jnp.where(row >= g * ATT_BLOCK, sink_ref[hk * GQA_GROUP + g], s)
        sink.append(s * LOG2E)

    def window(ref_cur, ref_prev, bi, blk, hk):
        ks = slice(hk * HEAD_DIM, (hk + 1) * HEAD_DIM)
        prev = (ref_prev[bi, :, ks] if blk == 0
                else ref_cur[bi, pl.ds((blk - 1) * ATT_BLOCK, ATT_BLOCK), ks])
        return jnp.concatenate([prev, ref_cur[bi, pl.ds(blk * ATT_BLOCK, ATT_BLOCK), ks]], axis=0)

    items = [(bi, blk, hk) for bi in range(q_ref.shape[0]) for blk in range(q_ref.shape[1] // ATT_BLOCK)
             for hk in range(N_KV_HEADS)]
    logits = {}
    for bi, blk, hk in items:
        q = q_ref[bi, pl.ds(blk * ATT_BLOCK, ATT_BLOCK), :]
        qg = jnp.concatenate([q[:, (hk * GQA_GROUP + g) * HEAD_DIM:(hk * GQA_GROUP + g + 1) * HEAD_DIM]
                              for g in range(GQA_GROUP)], axis=0)
        lg = lax.dot_general(qg, window(kc_ref, kp_ref, bi, blk, hk), (((1,), (1,)), ((), ())),
                             preferred_element_type=F32)
        lg = jnp.where(use_cur, lg[:, ATT_BLOCK:], lg[:, :ATT_BLOCK]) + bias_ref[hk]
        logits[bi, blk, hk] = jnp.where(no_key, NEG_INF, lg) if blk == 0 else lg
    yield

    e2, e_sink = {}, {}
    for it in items:
        m = jnp.maximum(jnp.max(logits[it], axis=-1, keepdims=True), sink[it[2]])
        eb = jnp.exp2(logits.pop(it) - m).astype(BF16)
        e_sink[it] = jnp.exp2(sink[it[2]] - m)
        zero = jnp.zeros_like(eb)
        e2[it] = jnp.concatenate([jnp.where(use_cur, zero, eb), jnp.where(use_cur, eb, zero)], axis=1)
        yield

    low = lax.broadcasted_iota(jnp.int32, (ATT_BLOCK, 2 * HEAD_DIM), 1) < HEAD_DIM
    for it in items:
        bi, blk, hk = it
        v2 = window(vc_ref, vp_ref, bi, blk, hk)
        vw = jnp.concatenate([v2, v2, jnp.ones((2 * ATT_BLOCK, 2 * HEAD_DIM), BF16)], axis=1)
        res = jnp.dot(e2.pop(it), vw, preferred_element_type=F32)
        out = res[:, :2 * HEAD_DIM] / (res[:, 2 * HEAD_DIM:] + e_sink.pop(it))
        for g in range(0, GQA_GROUP, 2):
            h = hk * GQA_GROUP + g
            pair = jnp.where(low, out[g * ATT_BLOCK:(g + 1) * ATT_BLOCK], out[(g + 1) * ATT_BLOCK:(g + 2) * ATT_BLOCK])
            o_ref[bi, pl.ds(blk * ATT_BLOCK, ATT_BLOCK), h * HEAD_DIM:(h + 2) * HEAD_DIM] = pair.astype(BF16)
        yield


def _ffn_kernel(x_ref, yr_ref, ya_ref, mod_ref, g2_ref, wo_ref, wg_ref, wu_ref, wd_ref, o_ref):
    n_sub = max(x_ref.shape[1] // FFN_SUB, 1)
    tm = x_ref.shape[1] // n_sub
    subs = [_ffn_rows(pl.ds(s * tm, tm), x_ref, yr_ref, ya_ref, mod_ref, g2_ref, wo_ref, wg_ref, wu_ref,
                      wd_ref, o_ref) for s in range(n_sub)]
    done = [False] * n_sub
    t = 0
    while not all(done):
        for s in range(min(t + 1, n_sub)):
            if not done[s]:
                done[s] = next(subs[s], True) is True
        t += 1


def _ffn_rows(rs, x_ref, yr_ref, ya_ref, mod_ref, g2_ref, wo_ref, wg_ref, wu_ref, wd_ref, o_ref):
    x = x_ref[0, rs, :]
    wo = wo_ref[...]
    mix = (jnp.dot(yr_ref[0, rs, :], wo[:RWKV_W], preferred_element_type=F32)
           + jnp.dot(ya_ref[0, rs, :], wo[RWKV_W:], preferred_element_type=F32))
    yield
    h_res = x + mod_ref[0, 2:3, :] * mix
    ms = jnp.mean(h_res * h_res, axis=-1, keepdims=True)
    h2 = (h_res * lax.rsqrt(ms + NORM_EPS) * (g2_ref[...] * (1.0 + mod_ref[0, 4:5, :]))
          + mod_ref[0, 3:4, :]).astype(BF16)
    yield
    d_ff = wg_ref.shape[1]
    ffn = jnp.zeros_like(x)
    for lo in range(0, d_ff, FFN_BLOCK):
        cs = slice(lo, min(lo + FFN_BLOCK, d_ff))
        gt = jnp.dot(h2, wg_ref[:, cs], preferred_element_type=F32)
        up = jnp.dot(h2, wu_ref[:, cs], preferred_element_type=F32)
        yield
        act = (gt * _sigmoid(gt) * up).astype(BF16)
        ffn = ffn + jnp.dot(act, wd_ref[cs, :], preferred_element_type=F32)
        yield
    o_ref[0, rs, :] = h_res + mod_ref[0, 5:6, :] * ffn


def _out_ffn(x, y_rwkv, y_att, mod3, g2, w_out, w_gate, w_up, w_down):
    B, S, D = x.shape
    tm = min(FFN_TILE, S)
    resident = lambda shape: pl.BlockSpec(shape, lambda b, j: (0,) * len(shape),
                                          pipeline_mode=pl.Buffered(1))
    row_spec = lambda w: pl.BlockSpec((1, tm, w), lambda b, j: (b, j, 0))
    return pl.pallas_call(
        _ffn_kernel,
        grid=(B, S // tm),
        in_specs=[row_spec(D), row_spec(RWKV_W), row_spec(ATT_Q_W),
                  pl.BlockSpec((1, N_MOD, D), lambda b, j: (b, 0, 0)),
                  resident(g2.shape), resident(w_out.shape), resident(w_gate.shape),
                  resident(w_up.shape), resident(w_down.shape)],
        out_specs=row_spec(D),
        out_shape=jax.ShapeDtypeStruct((B, S, D), F32),
        compiler_params=pltpu.CompilerParams(dimension_semantics=("arbitrary", "arbitrary"),
                                             vmem_limit_bytes=VMEM_LIMIT),
        name="out_proj_ffn",
    )(x, y_rwkv, y_att, mod3, g2, w_out, w_gate, w_up, w_down)


def kernel(x, c, w_ada, b_ada, norm1_g, w_in, rwkv_mu, w0, w_lora_up, a0, a_lora_up, g_lora_up, k_k, k_a, r_k, lnx_g, lnx_b, q_norm_g, k_norm_g, sinks, rel_bias, w_out, norm2_g, w_gate, w_up, w_down):
    B, S, D = x.shape
    row = lambda t: t.reshape(1, -1).astype(F32)
    lane = jnp.arange(MXU_TILE) // HEAD_DIM
    seg = (lane[:, None] == lane[None, :]).astype(BF16)
    t_idx = jnp.arange(min(IN_SUB, S))
    tri = ((t_idx[:, None] >= t_idx[None, :])
           & (t_idx[:, None] // CHUNK == t_idx[None, :] // CHUNK)).astype(BF16)

    mod3 = _modulation(c.astype(F32), w_ada, b_ada).reshape(B, N_MOD, D)
    bias = _bias_table(rel_bias)

    (at, bt, kt, rt, bh, kh, v, wl, bonus, g, q, ak, av,
     w_out16, w_gate16, w_up16, w_down16) = _in_projection(
        x, mod3, row(norm1_g), w_in.astype(BF16), row(rwkv_mu), row(w0), w_lora_up.astype(BF16),
        row(a0), a_lora_up.astype(BF16), g_lora_up.astype(BF16), row(k_k), row(k_a), row(r_k),
        row(jnp.tile(q_norm_g, N_Q_HEADS)), row(jnp.tile(k_norm_g, N_KV_HEADS)), seg, tri,
        (w_out.astype(F32), w_gate.astype(F32), w_up.astype(F32), w_down.astype(F32)))

    y_rwkv, y_att = _mixers(at, bt, kt, rt, bh, kh, v, wl, bonus, g, row(lnx_g), row(lnx_b), seg,
                            q, ak, av, bias, sinks)

    out = _out_ffn(x, y_rwkv, y_att, mod3, row(norm2_g), w_out16, w_gate16, w_up16, w_down16)
    return out.astype(x.dtype)
```

```python
import math

import jax
import jax.numpy as jnp
from jax import lax
from jax.experimental import pallas as pl
from jax.experimental.pallas import tpu as pltpu

F32 = jnp.float32
BF16 = jnp.bfloat16

HEAD_DIM = 64
RWKV_HEADS = 8
RWKV_W = RWKV_HEADS * HEAD_DIM
DECAY_LORA = 64
AAA_LORA = 64
GATE_LORA = 128
RWKV_COLS = 3 * RWKV_W + DECAY_LORA + AAA_LORA + GATE_LORA
RWKV_GN_EPS = 64e-5
N_Q_HEADS = 8
N_KV_HEADS = 2
GQA_GROUP = N_Q_HEADS // N_KV_HEADS
ATT_Q_W = N_Q_HEADS * HEAD_DIM
ATT_KV_W = N_KV_HEADS * HEAD_DIM
ATT_BLOCK = 128
WINDOW = 128
ATT_SCALE = 1.0 / math.sqrt(HEAD_DIM)
LOG2E = math.log2(math.e)
NUM_BUCKETS = 32
MAX_DISTANCE = 128
N_MOD = 6
NORM_EPS = 1e-6
NEG_INF = -1e30

CHUNK = 64
IN_TILE = 1024
IN_SUB = 256
MIX_TILE = 512
MIX_BATCH = 1
FFN_TILE = 1024
FFN_SUB = 256
MXU_TILE = 256
F32_SUBLANES = 8
BF16_SUBLANES = 16
FFN_BLOCK = 4 * MXU_TILE
VMEM_LIMIT = 56 * 1024 * 1024


def _dot(a, b):
    return jnp.dot(a.astype(BF16), b.astype(BF16), preferred_element_type=F32)


def _dot_nt(a, b):
    return lax.dot_general(a.astype(BF16), b.astype(BF16), (((1,), (1,)), ((), ())),
                           preferred_element_type=F32)


def _dot_tn(a, b):
    return lax.dot_general(a.astype(BF16), b.astype(BF16), (((0,), (0,)), ((), ())),
                           preferred_element_type=F32)


def _split(x):
    hi = x.astype(BF16)
    lo = (x - hi.astype(F32)).astype(BF16)
    return hi, lo


def _seg_sum(x, seg):
    m, w = x.shape
    xb = x.astype(BF16)
    blk = seg.shape[0]
    if w <= blk:
        return jnp.dot(xb, seg[:w, :w], preferred_element_type=F32)
    n = w // blk
    stacked = jnp.concatenate([xb[:, i * blk:(i + 1) * blk] for i in range(n)], axis=0)
    res = jnp.dot(stacked, seg, preferred_element_type=F32)
    return jnp.concatenate([res[i * m:(i + 1) * m] for i in range(n)], axis=1)


def _sigmoid(z):
    return 0.5 * jnp.tanh(0.5 * z) + 0.5


def _mod_kernel(c_ref, w_ref, b_ref, o_ref):
    c = c_ref[...]
    s = c * _sigmoid(c)
    s_hi, s_lo = _split(s)
    w = w_ref[...]
    w_hi, w_lo = _split(w)
    d = lambda x, y: jnp.dot(x, y, preferred_element_type=F32)
    n = c.shape[0]
    hh = d(jnp.concatenate([s_hi, s_lo], axis=0), w_hi)
    o_ref[...] = hh[:n] + d(s_hi, w_lo) + hh[n:] + b_ref[...]


def _modulation(c, w_ada, b_ada):
    B, D = c.shape
    n = w_ada.shape[1]
    blk = D
    return pl.pallas_call(
        _mod_kernel,
        grid=(n // blk,),
        in_specs=[pl.BlockSpec((B, D), lambda j: (0, 0)),
                  pl.BlockSpec((D, blk), lambda j: (0, j)),
                  pl.BlockSpec((1, blk), lambda j: (0, j))],
        out_specs=pl.BlockSpec((B, blk), lambda j: (0, j)),
        out_shape=jax.ShapeDtypeStruct((B, n), F32),
        name="adaln_mod",
    )(c, w_ada, b_ada.reshape(1, n))


def _t5_thresholds():
    max_exact = NUM_BUCKETS // 2
    out = []
    for k in range(1, NUM_BUCKETS - max_exact):
        edge = max_exact * (MAX_DISTANCE / max_exact) ** (k / (NUM_BUCKETS - max_exact))
        assert abs(edge - round(edge)) > 1e-6, "bucket edge on an integer distance: floor() would be ambiguous"
        out.append(math.ceil(edge))
    return tuple(out)


_T5_THRESHOLDS = _t5_thresholds()


def _bias_kernel(rb_ref, o_ref):
    qi = lax.broadcasted_iota(jnp.int32, (ATT_BLOCK, ATT_BLOCK), 0)
    kj = lax.broadcasted_iota(jnp.int32, (ATT_BLOCK, ATT_BLOCK), 1)
    n = jnp.where(kj <= qi, qi - kj, qi + ATT_BLOCK - kj)
    max_exact = NUM_BUCKETS // 2
    large = jnp.full(n.shape, max_exact, jnp.int32)
    for t in _T5_THRESHOLDS:
        large = large + jnp.where(n >= t, 1, 0)
    bucket = jnp.where(n < max_exact, n, large)
    for h in range(N_Q_HEADS):
        acc = jnp.zeros((ATT_BLOCK, ATT_BLOCK), F32)
        for b in range(NUM_BUCKETS):
            acc = jnp.where(bucket == b, rb_ref[b, h], acc)
        o_ref[h] = acc * LOG2E


def _bias_table(rel_bias):
    assert WINDOW == ATT_BLOCK
    return pl.pallas_call(
        _bias_kernel,
        in_specs=[pl.BlockSpec(memory_space=pltpu.SMEM)],
        out_specs=pl.BlockSpec(memory_space=pltpu.VMEM),
        out_shape=jax.ShapeDtypeStruct((N_Q_HEADS, ATT_BLOCK, ATT_BLOCK), F32),
        name="rel_bias_table",
    )(rel_bias)


def _inproj_kernel(x_ref, mod_ref, g1_ref, win_ref, mu_ref, w0_ref, wlu_ref, a0_ref, alu_ref,
                   glu_ref, kk_ref, ka_ref, rk_ref, qg_ref, kg_ref, seg_ref, tri_ref, *rest):
    n_cast = (len(rest) - 14) // 2
    cast_in, rest = rest[:n_cast], rest[n_cast:]
    (at_out, bt_out, kt_out, rt_out, bh_out, kh_out, v_out, wl_out, bonus_out, g_out,
     q_out, ak_out, av_out) = rest[:13]
    cast_out, carry_ref = rest[13:13 + n_cast], rest[-1]

    @pl.when(pl.program_id(1) == 0)
    def _():
        carry_ref[...] = jnp.zeros_like(carry_ref)

    for src, dst in zip(cast_in, cast_out):
        dst[...] = src[...].astype(BF16)

    tm = tri_ref.shape[0]
    refs = (mu_ref, w0_ref, wlu_ref, a0_ref, alu_ref, glu_ref, kk_ref, ka_ref, rk_ref, qg_ref, kg_ref,
            seg_ref, tri_ref)
    outs = (at_out, bt_out, kt_out, rt_out, bh_out, kh_out, v_out, wl_out, bonus_out, g_out, q_out,
            ak_out, av_out)
    tail = iter(())
    gain = g1_ref[...] * (1.0 + mod_ref[0, 1:2, :])
    for s in range(x_ref.shape[1] // tm):
        rs = pl.ds(s * tm, tm)
        x = x_ref[0, rs, :]
        ms = jnp.mean(x * x, axis=-1, keepdims=True)
        h = (x * lax.rsqrt(ms + NORM_EPS) * gain + mod_ref[0, 0:1, :]).astype(BF16)
        pieces = []
        for lo in range(0, win_ref.shape[1], MXU_TILE):
            pieces.append(jnp.dot(h, win_ref[:, lo:lo + MXU_TILE], preferred_element_type=F32))
            next(tail, None)
        for _ in tail:
            pass
        tail = _inproj_tail(jnp.concatenate(pieces, axis=1), rs, s * (tm // CHUNK), refs, outs, carry_ref)
    for _ in tail:
        pass


def _inproj_tail(p, rs, c0, refs, outs, carry_ref):
    (mu_ref, w0_ref, wlu_ref, a0_ref, alu_ref, glu_ref, kk_ref, ka_ref, rk_ref, qg_ref, kg_ref,
     seg_ref, tri_ref) = refs
    (at_out, bt_out, kt_out, rt_out, bh_out, kh_out, v_out, wl_out, bonus_out, g_out, q_out,
     ak_out, av_out) = outs
    tm = p.shape[0]

    mm = lambda a, b: jnp.dot(a, b, preferred_element_type=F32)

    pr = p[:, :RWKV_COLS]
    rolled = pltpu.roll(pr, 1, 0)
    first = lax.broadcasted_iota(jnp.int32, (F32_SUBLANES, RWKV_COLS), 0) == 0
    prev = jnp.concatenate([jnp.where(first, carry_ref[...], rolled[:F32_SUBLANES]), rolled[F32_SUBLANES:]],
                           axis=0)
    carry_ref[...] = pr[tm - 1:tm, :]
    pm = pr + (prev - pr) * mu_ref[...]
    r = pm[:, 0:RWKV_W]
    k = pm[:, RWKV_W:2 * RWKV_W]
    v = pm[:, 2 * RWKV_W:3 * RWKV_W]
    o = 3 * RWKV_W
    tanh_w = jnp.tanh(pm[:, o:o + DECAY_LORA]).astype(BF16)
    xa = pm[:, o + DECAY_LORA:o + DECAY_LORA + AAA_LORA].astype(BF16)
    sig_g = _sigmoid(pm[:, o + DECAY_LORA + AAA_LORA:RWKV_COLS]).astype(BF16)
    yield
    q = p[:, RWKV_COLS:RWKV_COLS + ATT_Q_W]
    ak = p[:, RWKV_COLS + ATT_Q_W:RWKV_COLS + ATT_Q_W + ATT_KV_W]
    av = p[:, RWKV_COLS + ATT_Q_W + ATT_KV_W:]
    kk = k * kk_ref[...]
    kk_sq, q_sq, ak_sq = kk * kk, q * q, ak * ak
    v_out[0, rs, :] = v.astype(BF16)
    av_out[0, rs, :] = av.astype(BF16)
    yield

    seg = seg_ref[...]
    z_mm = mm(tanh_w, wlu_ref[...])
    a_mm = mm(xa, alu_ref[...])
    g_mm = mm(sig_g, glu_ref[...])
    kk_ss = _seg_sum(kk_sq, seg)
    q_ss = _seg_sum(q_sq, seg)
    ak_ss = _seg_sum(ak_sq, seg)
    yield

    lw = (-math.exp(-0.5) * math.log2(math.e)) * _sigmoid(w0_ref[...] + z_mm)
    lw_hi, lw_lo = _split(lw)
    a = _sigmoid(a0_ref[...] + a_mm)
    g_out[0, rs, :] = g_mm
    kk = kk * lax.rsqrt(jnp.maximum(kk_ss, 1e-24))
    bv = kk * a
    k = k * (1.0 + (a - 1.0) * ka_ref[...])
    rk_prod = r * k * rk_ref[...]
    yield
    q_out[0, rs, :] = (q * lax.rsqrt(q_ss * (1.0 / HEAD_DIM) + NORM_EPS) * (qg_ref[...] * (ATT_SCALE * LOG2E))).astype(BF16)
    ak_out[0, rs, :] = (ak * lax.rsqrt(ak_ss * (1.0 / HEAD_DIM) + NORM_EPS) * kg_ref[...]).astype(BF16)
    yield

    tri = tri_ref[...]
    cum = mm(tri, lw_hi) + mm(tri, lw_lo)
    bonus_ss = _seg_sum(rk_prod, seg)
    yield

    n_chunks = tm // CHUNK
    ends = [cum[(c + 1) * CHUNK - 1:(c + 1) * CHUNK, :] for c in range(n_chunks)]
    cl = jnp.concatenate([jnp.broadcast_to(e, (CHUNK, RWKV_W)) for e in ends], axis=0)
    for c in range(n_chunks):
        wl_out[0, c0 + c] = jnp.exp2(ends[c])
    bonus_out[0, rs, :] = bonus_ss * v
    yield
    e_neg = jnp.exp2(-cum)
    bt_out[0, rs, :] = (bv * e_neg).astype(BF16)
    kt_out[0, rs, :] = (k * e_neg).astype(BF16)
    yield
    e_hat = jnp.exp2(cl - cum)
    bh_out[0, rs, :] = (bv * e_hat).astype(BF16)
    kh_out[0, rs, :] = (k * e_hat).astype(BF16)
    yield
    at_out[0, rs, :] = (-kk * jnp.exp2(cum - lw)).astype(BF16)
    rt_out[0, rs, :] = (r * jnp.exp2(cum)).astype(BF16)


def _in_projection(x, mod3, g1, w_in, mu, w0, wlu, a0, alu, glu, k_k, k_a, r_k, qg, kg, seg, tri, to_cast):
    B, S, D = x.shape
    tm = min(IN_TILE, S)
    nj = S // tm
    n_steps = B * nj
    const = lambda shape: pl.BlockSpec(shape, lambda b, j: (0,) * len(shape))
    row_spec = lambda w: pl.BlockSpec((1, tm, w), lambda b, j: (b, j, 0))
    rows = lambda w, dt: jax.ShapeDtypeStruct((B, S, w), dt)

    def cast_spec(w):
        n = n_steps
        while n > 1 and (n_steps % n or w.shape[0] % (BF16_SUBLANES * n)):
            n -= 1
        per = n_steps // n
        return pl.BlockSpec((w.shape[0] // n, w.shape[1]), lambda b, j: ((b * nj + j) // per, 0))

    cast_specs = [cast_spec(w) for w in to_cast]
    return pl.pallas_call(
        _inproj_kernel,
        grid=(B, nj),
        in_specs=[row_spec(D),
                  pl.BlockSpec((1, N_MOD, D), lambda b, j: (b, 0, 0)),
                  const(g1.shape), const(w_in.shape), const(mu.shape), const(w0.shape),
                  const(wlu.shape), const(a0.shape), const(alu.shape), const(glu.shape),
                  const(k_k.shape), const(k_a.shape), const(r_k.shape), const(qg.shape),
                  const(kg.shape), const(seg.shape), const(tri.shape)] + cast_specs,
        out_specs=([row_spec(RWKV_W)] * 7
                   + [pl.BlockSpec((1, tm // CHUNK, 1, RWKV_W), lambda b, j: (b, j, 0, 0))]
                   + [row_spec(RWKV_W)] * 2
                   + [row_spec(ATT_Q_W), row_spec(ATT_KV_W), row_spec(ATT_KV_W)] + cast_specs),
        out_shape=([rows(RWKV_W, BF16)] * 7
                   + [jax.ShapeDtypeStruct((B, S // CHUNK, 1, RWKV_W), F32)]
                   + [rows(RWKV_W, F32)] * 2
                   + [rows(ATT_Q_W, BF16), rows(ATT_KV_W, BF16), rows(ATT_KV_W, BF16)]
                   + [jax.ShapeDtypeStruct(w.shape, BF16) for w in to_cast]),
        scratch_shapes=[pltpu.VMEM((1, RWKV_COLS), F32)],
        compiler_params=pltpu.CompilerParams(dimension_semantics=("arbitrary", "arbitrary"),
                                             vmem_limit_bytes=VMEM_LIMIT),
        name="in_projection",
    )(x, mod3, g1, w_in, mu, w0, wlu, a0, alu, glu, k_k, k_a, r_k, qg, kg, seg, tri, *to_cast)


def _mixer_kernel(at_ref, bt_ref, kt_ref, rt_ref, bh_ref, kh_ref, v_ref, wl_ref, bonus_ref, g_ref,
                  lg_ref, lb_ref, seg_ref, q_ref, kc_ref, kp_ref, vc_ref, vp_ref, bias_ref, sink_ref,
                  o_ref, oatt_ref, state_ref):
    @pl.when(pl.program_id(1) == 0)
    def _():
        state_ref[...] = jnp.zeros_like(state_ref)

    L, D, P = CHUNK, HEAD_DIM, 2 * HEAD_DIM
    n_batch = at_ref.shape[0]
    n_chunks = at_ref.shape[1] // L
    n_pairs = RWKV_W // P
    row = lax.broadcasted_iota(jnp.int32, (L, P), 0)
    lane = lax.broadcasted_iota(jnp.int32, (L, P), 1)
    low = lane < D
    s_idx = jnp.where(low, lane, lane - D)
    strict = row > s_idx
    incl = row >= s_idx
    zeros_lp = jnp.zeros((L, P), F32)
    own = (lambda t: jnp.where(low, t, jnp.zeros_like(t)), lambda t: jnp.where(low, jnp.zeros_like(t), t))
    other = (own[1], own[0])
    diag = (lane == row, lane == row + D)
    seg = seg_ref[...]

    chunks = [(bi, c) for c in range(n_chunks) for bi in range(n_batch)]
    items = [(ck, j, par) for ck in chunks for j in range(n_pairs) for par in range(2)]
    chunk_in, top, bot, v_o, x = {}, {}, {}, {}, {}

    def front():
        for ck in chunks:
            bi, c = ck
            rows = pl.ds(c * L, L)
            ci = dict(a_t=at_ref[bi, rows, :], b_t=bt_ref[bi, rows, :], k_t=kt_ref[bi, rows, :],
                      r_t=rt_ref[bi, rows, :], b_h=bh_ref[bi, rows, :], k_h=kh_ref[bi, rows, :],
                      v=v_ref[bi, rows, :], w_l=wl_ref[bi, c])
            chunk_in[ck] = ci
            for j in range(n_pairs):
                ps = slice(j * P, (j + 1) * P)
                atp, rtp = ci["a_t"][:, ps], ci["r_t"][:, ps]
                btp, ktp = ci["b_t"][:, ps], ci["k_t"][:, ps]
                s1 = _dot_nt(jnp.concatenate([atp, rtp], axis=0),
                             jnp.concatenate([own[0](btp), own[0](ktp), own[1](btp), own[1](ktp)], axis=0))
                v_swap = pltpu.roll(ci["v"][:, ps].astype(F32), D, 1)
                for par in range(2):
                    it = (ck, j, par)
                    top[it] = jnp.where(strict, s1[:L, P * par:P * (par + 1)], 0.0)
                    bot[it] = jnp.where(incl, s1[L:, P * par:P * (par + 1)], 0.0)
                    v_o[it] = other[par](v_swap)
                    x[it] = own[par](atp).astype(F32)
        yield
        for it in items:
            x[it] = x[it] + _dot(top[it], jnp.concatenate([zeros_lp, v_o[it]], axis=0))
        yield
        a_pow = {it: top[it][:, :D].astype(BF16) for it in items}
        n = 1
        while n < L:
            last = 2 * n >= L
            for it in items:
                xb = x[it].astype(BF16)
                if last:
                    x[it] = x[it] + jnp.dot(a_pow[it], xb, preferred_element_type=F32)
                else:
                    res = jnp.dot(a_pow[it], jnp.concatenate([xb, a_pow[it]], axis=1),
                                  preferred_element_type=F32)
                    x[it] = x[it] + res[:, :P]
                    a_pow[it] = res[:, P:].astype(BF16)
            n *= 2
            yield

    def back():
        rhs2 = {it: jnp.concatenate([x[it], v_o[it]], axis=0).astype(BF16) for it in items}
        rb = {it: jnp.dot(bot[it].astype(BF16), rhs2[it], preferred_element_type=F32) for it in items}
        mn = {}
        for ck in chunks:
            ci = chunk_in[ck]
            for j in range(n_pairs):
                ps = slice(j * P, (j + 1) * P)
                res = _dot_tn(jnp.concatenate([ci["b_h"][:, ps], ci["k_h"][:, ps]], axis=0),
                              jnp.concatenate([rhs2[(ck, j, 0)], rhs2[(ck, j, 1)]], axis=1))
                for par in range(2):
                    mn[(ck, j, par)] = res[D * par:D * (par + 1), P * par:P * (par + 1)]
        yield
        y = {}
        for ck in chunks:
            ci = chunk_in[ck]
            for j in range(n_pairs):
                ps = slice(j * P, (j + 1) * P)
                for par in range(2):
                    it = (ck, j, par)
                    h = (ck[0], 2 * j + par)
                    g_p = rb[it] + ci["r_t"][:, ps]
                    m_p = mn[it] + jnp.where(diag[par], ci["w_l"][:, ps], 0.0)
                    z = state_ref[h]
                    rhs = jnp.concatenate([z, zeros_lp] if par == 0 else [zeros_lp, z], axis=0)
                    res = _dot(jnp.concatenate([g_p, m_p], axis=0), rhs)
                    y[it] = res[:L] + rb[it]
                    state_ref[h] = other[par](res[L:] + mn[it])
            if ck[0] == n_batch - 1:
                yield
        for ck in chunks:
            bi, c = ck
            rows = pl.ds(c * L, L)
            yc = jnp.concatenate(
                [pltpu.roll(jnp.where(low, y[ck, j, 1], y[ck, j, 0]), D, 1) for j in range(n_pairs)], axis=1)
            mu = _seg_sum(yc, seg) * (1.0 / HEAD_DIM)
            yc = yc - mu
            var = _seg_sum(yc * yc, seg) * (1.0 / HEAD_DIM)
            yn = yc * lax.rsqrt(var + RWKV_GN_EPS) * lg_ref[...] + lb_ref[...]
            o_ref[bi, rows, :] = ((yn + bonus_ref[bi, rows, :]) * g_ref[bi, rows, :]).astype(BF16)
            if bi == n_batch - 1:
                yield

    swa = _swa_stages(q_ref, kc_ref, kp_ref, vc_ref, vp_ref, bias_ref, sink_ref, oatt_ref)
    next(swa)
    for _ in front():
        next(swa, None)
    for _ in back():
        next(swa, None)
    for _ in swa:
        pass


def _mixers(at, bt, kt, rt, bh, kh, v, wl, bonus, g, lnx_g, lnx_b, seg, q, ak, av, bias, sinks):
    B, S, W = at.shape
    tc = min(MIX_TILE, S)
    nbat = MIX_BATCH if B % MIX_BATCH == 0 else 1
    nb = tc // ATT_BLOCK
    bias = bias.reshape(N_KV_HEADS, GQA_GROUP * ATT_BLOCK, ATT_BLOCK)
    const = lambda shape: pl.BlockSpec(shape, lambda b, j: (0,) * len(shape))
    rows = lambda w: pl.BlockSpec((nbat, tc, w), lambda b, j: (b, j, 0))
    prev_block = lambda w: pl.BlockSpec((nbat, ATT_BLOCK, w), lambda b, j: (b, jnp.maximum(j * nb - 1, 0), 0))
    return pl.pallas_call(
        _mixer_kernel,
        grid=(B // nbat, S // tc),
        in_specs=([rows(W)] * 7 + [pl.BlockSpec((nbat, tc // CHUNK, 1, W), lambda b, j: (b, j, 0, 0))]
                  + [rows(W)] * 2 + [const(lnx_g.shape), const(lnx_b.shape), const(seg.shape)]
                  + [rows(ATT_Q_W), rows(ATT_KV_W), prev_block(ATT_KV_W), rows(ATT_KV_W), prev_block(ATT_KV_W),
                     const(bias.shape), pl.BlockSpec(memory_space=pltpu.SMEM)]),
        out_specs=[rows(W), rows(ATT_Q_W)],
        out_shape=[jax.ShapeDtypeStruct((B, S, W), BF16), jax.ShapeDtypeStruct((B, S, ATT_Q_W), BF16)],
        scratch_shapes=[pltpu.VMEM((nbat, RWKV_HEADS, CHUNK, 2 * HEAD_DIM), F32)],
        compiler_params=pltpu.CompilerParams(dimension_semantics=("arbitrary", "arbitrary"),
                                             vmem_limit_bytes=VMEM_LIMIT),
        name="rwkv7_swa_mixers",
    )(at, bt, kt, rt, bh, kh, v, wl, bonus, g, lnx_g, lnx_b, seg, q, ak, ak, av, av, bias, sinks)


def _swa_stages(q_ref, kc_ref, kp_ref, vc_ref, vp_ref, bias_ref, sink_ref, o_ref):
    first = pl.program_id(1) == 0
    rows = GQA_GROUP * ATT_BLOCK
    col = lax.broadcasted_iota(jnp.int32, (rows, ATT_BLOCK), 1)
    qpos = lax.broadcasted_iota(jnp.int32, (rows, ATT_BLOCK), 0) % ATT_BLOCK
    use_cur = col <= qpos
    no_key = jnp.logical_and(first, jnp.logical_not(use_cur))
    row = lax.broadcasted_iota(jnp.int32, (rows, 1), 0)
    sink = []
    for hk in range(N_KV_HEADS):
        s = jnp.full((rows, 1), sink_ref[hk * GQA_GROUP], F32)
        for g in range(1, GQA_GROUP):
            s = jnp.where(row >= g * ATT_BLOCK, sink_ref[hk * GQA_GROUP + g], s)
        sink.append(s * LOG2E)

    def window(ref_cur, ref_prev, bi, blk, hk):
        ks = slice(hk * HEAD_DIM, (hk + 1) * HEAD_DIM)
        prev = (ref_prev[bi, :, ks] if blk == 0
                else ref_cur[bi, pl.ds((blk - 1) * ATT_BLOCK, ATT_BLOCK), ks])
        return jnp.concatenate([prev, ref_cur[bi, pl.ds(blk * ATT_BLOCK, ATT_BLOCK), ks]], axis=0)

    items = [(bi, blk, hk) for bi in range(q_ref.shape[0]) for blk in range(q_ref.shape[1] // ATT_BLOCK)
             for hk in range(N_KV_HEADS)]
    logits = {}
    for bi, blk, hk in items:
        q = q_ref[bi, pl.ds(blk * ATT_BLOCK, ATT_BLOCK), :]
        qg = jnp.concatenate([q[:, (hk * GQA_GROUP + g) * HEAD_DIM:(hk * GQA_GROUP + g + 1) * HEAD_DIM]
                              for g in range(GQA_GROUP)], axis=0)
        lg = lax.dot_general(qg, window(kc_ref, kp_ref, bi, blk, hk), (((1,), (1,)), ((), ())),
                             preferred_element_type=F32)
        lg = jnp.where(use_cur, lg[:, ATT_BLOCK:], lg[:, :ATT_BLOCK]) + bias_ref[hk]
        logits[bi, blk, hk] = jnp.where(no_key, NEG_INF, lg) if blk == 0 else lg
    yield

    e2, e_sink = {}, {}
    for it in items:
        m = jnp.maximum(jnp.max(logits[it], axis=-1, keepdims=True), sink[it[2]])
        eb = jnp.exp2(logits.pop(it) - m).astype(BF16)
        e_sink[it] = jnp.exp2(sink[it[2]] - m)
        zero = jnp.zeros_like(eb)
        e2[it] = jnp.concatenate([jnp.where(use_cur, zero, eb), jnp.where(use_cur, eb, zero)], axis=1)
        yield

    low = lax.broadcasted_iota(jnp.int32, (ATT_BLOCK, 2 * HEAD_DIM), 1) < HEAD_DIM
    for it in items:
        bi, blk, hk = it
        v2 = window(vc_ref, vp_ref, bi, blk, hk)
        vw = jnp.concatenate([v2, v2, jnp.ones((2 * ATT_BLOCK, 2 * HEAD_DIM), BF16)], axis=1)
        res = jnp.dot(e2.pop(it), vw, preferred_element_type=F32)
        out = res[:, :2 * HEAD_DIM] / (res[:, 2 * HEAD_DIM:] + e_sink.pop(it))
        for g in range(0, GQA_GROUP, 2):
            h = hk * GQA_GROUP + g
            pair = jnp.where(low, out[g * ATT_BLOCK:(g + 1) * ATT_BLOCK], out[(g + 1) * ATT_BLOCK:(g + 2) * ATT_BLOCK])
            o_ref[bi, pl.ds(blk * ATT_BLOCK, ATT_BLOCK), h * HEAD_DIM:(h + 2) * HEAD_DIM] = pair.astype(BF16)
        yield


def _ffn_kernel(x_ref, yr_ref, ya_ref, mod_ref, g2_ref, wo_ref, wg_ref, wu_ref, wd_ref, o_ref):
    n_sub = max(x_ref.shape[1] // FFN_SUB, 1)
    tm = x_ref.shape[1] // n_sub
    subs = [_ffn_rows(pl.ds(s * tm, tm), x_ref, yr_ref, ya_ref, mod_ref, g2_ref, wo_ref, wg_ref, wu_ref,
                      wd_ref, o_ref) for s in range(n_sub)]
    done = [False] * n_sub
    t = 0
    while not all(done):
        for s in range(min(t + 1, n_sub)):
            if not done[s]:
                done[s] = next(subs[s], True) is True
        t += 1


def _ffn_rows(rs, x_ref, yr_ref, ya_ref, mod_ref, g2_ref, wo_ref, wg_ref, wu_ref, wd_ref, o_ref):
    x = x_ref[0, rs, :]
    wo = wo_ref[...]
    mix = (jnp.dot(yr_ref[0, rs, :], wo[:RWKV_W], preferred_element_type=F32)
           + jnp.dot(ya_ref[0, rs, :], wo[RWKV_W:], preferred_element_type=F32))
    yield
    h_res = x + mod_ref[0, 2:3, :] * mix
    ms = jnp.mean(h_res * h_res, axis=-1, keepdims=True)
    h2 = (h_res * lax.rsqrt(ms + NORM_EPS) * (g2_ref[...] * (1.0 + mod_ref[0, 4:5, :]))
          + mod_ref[0, 3:4, :]).astype(BF16)
    yield
    d_ff = wg_ref.shape[1]
    ffn = jnp.zeros_like(x)
    for lo in range(0, d_ff, FFN_BLOCK):
        cs = slice(lo, min(lo + FFN_BLOCK, d_ff))
        gt = jnp.dot(h2, wg_ref[:, cs], preferred_element_type=F32)
        up = jnp.dot(h2, wu_ref[:, cs], preferred_element_type=F32)
        yield
        act = (gt * _sigmoid(gt) * up).astype(BF16)
        ffn = ffn + jnp.dot(act, wd_ref[cs, :], preferred_element_type=F32)
        yield
    o_ref[0, rs, :] = h_res + mod_ref[0, 5:6, :] * ffn


def _out_ffn(x, y_rwkv, y_att, mod3, g2, w_out, w_gate, w_up, w_down):
    B, S, D = x.shape
    tm = min(FFN_TILE, S)
    resident = lambda shape: pl.BlockSpec(shape, lambda b, j: (0,) * len(shape),
                                          pipeline_mode=pl.Buffered(1))
    row_spec = lambda w: pl.BlockSpec((1, tm, w), lambda b, j: (b, j, 0))
    return pl.pallas_call(
        _ffn_kernel,
        grid=(B, S // tm),
        in_specs=[row_spec(D), row_spec(RWKV_W), row_spec(ATT_Q_W),
                  pl.BlockSpec((1, N_MOD, D), lambda b, j: (b, 0, 0)),
                  resident(g2.shape), resident(w_out.shape), resident(w_gate.shape),
                  resident(w_up.shape), resident(w_down.shape)],
        out_specs=row_spec(D),
        out_shape=jax.ShapeDtypeStruct((B, S, D), F32),
        compiler_params=pltpu.CompilerParams(dimension_semantics=("arbitrary", "arbitrary"),
                                             vmem_limit_bytes=VMEM_LIMIT),
        name="out_proj_ffn",
    )(x, y_rwkv, y_att, mod3, g2, w_out, w_gate, w_up, w_down)


def kernel(x, c, w_ada, b_ada, norm1_g, w_in, rwkv_mu, w0, w_lora_up, a0, a_lora_up, g_lora_up, k_k, k_a, r_k, lnx_g, lnx_b, q_norm_g, k_norm_g, sinks, rel_bias, w_out, norm2_g, w_gate, w_up, w_down):
    B, S, D = x.shape
    row = lambda t: t.reshape(1, -1).astype(F32)
    lane = jnp.arange(MXU_TILE) // HEAD_DIM
    seg = (lane[:, None] == lane[None, :]).astype(BF16)
    t_idx = jnp.arange(min(IN_SUB, S))
    tri = ((t_idx[:, None] >= t_idx[None, :])
           & (t_idx[:, None] // CHUNK == t_idx[None, :] // CHUNK)).astype(BF16)

    mod3 = _modulation(c.astype(F32), w_ada, b_ada).reshape(B, N_MOD, D)
    bias = _bias_table(rel_bias)

    (at, bt, kt, rt, bh, kh, v, wl, bonus, g, q, ak, av,
     w_out16, w_gate16, w_up16, w_down16) = _in_projection(
        x, mod3, row(norm1_g), w_in.astype(BF16), row(rwkv_mu), row(w0), w_lora_up.astype(BF16),
        row(a0), a_lora_up.astype(BF16), g_lora_up.astype(BF16), row(k_k), row(k_a), row(r_k),
        row(jnp.tile(q_norm_g, N_Q_HEADS)), row(jnp.tile(k_norm_g, N_KV_HEADS)), seg, tri,
        (w_out.astype(F32), w_gate.astype(F32), w_up.astype(F32), w_down.astype(F32)))

    y_rwkv, y_att = _mixers(at, bt, kt, rt, bh, kh, v, wl, bonus, g, row(lnx_g), row(lnx_b), seg,
                            q, ak, av, bias, sinks)

    out = _out_ffn(x, y_rwkv, y_att, mod3, row(norm2_g), w_out16, w_gate16, w_up16, w_down16)
    return out.astype(x.dtype)
```

```python
import math

import jax
import jax.numpy as jnp
from jax import lax
from jax.experimental import pallas as pl
from jax.experimental.pallas import tpu as pltpu

F32 = jnp.float32
BF16 = jnp.bfloat16

HEAD_DIM = 64
RWKV_HEADS = 8
RWKV_W = RWKV_HEADS * HEAD_DIM
DECAY_LORA = 64
AAA_LORA = 64
GATE_LORA = 128
RWKV_COLS = 3 * RWKV_W + DECAY_LORA + AAA_LORA + GATE_LORA
RWKV_GN_EPS = 64e-5
N_Q_HEADS = 8
N_KV_HEADS = 2
GQA_GROUP = N_Q_HEADS // N_KV_HEADS
ATT_Q_W = N_Q_HEADS * HEAD_DIM
ATT_KV_W = N_KV_HEADS * HEAD_DIM
ATT_BLOCK = 128
WINDOW = 128
ATT_SCALE = 1.0 / math.sqrt(HEAD_DIM)
LOG2E = math.log2(math.e)
NUM_BUCKETS = 32
MAX_DISTANCE = 128
N_MOD = 6
NORM_EPS = 1e-6
NEG_INF = -1e30

CHUNK = 64
IN_TILE = 1024
IN_SUB = 256
MIX_TILE = 512
MIX_BATCH = 1
FFN_TILE = 1024
FFN_SUB = 256
MXU_TILE = 256
F32_SUBLANES = 8
BF16_SUBLANES = 16
FFN_BLOCK = 4 * MXU_TILE
VMEM_LIMIT = 56 * 1024 * 1024


def _dot(a, b):
    return jnp.dot(a.astype(BF16), b.astype(BF16), preferred_element_type=F32)


def _dot_nt(a, b):
    return lax.dot_general(a.astype(BF16), b.astype(BF16), (((1,), (1,)), ((), ())),
                           preferred_element_type=F32)


def _dot_tn(a, b):
    return lax.dot_general(a.astype(BF16), b.astype(BF16), (((0,), (0,)), ((), ())),
                           preferred_element_type=F32)


def _split(x):
    hi = x.astype(BF16)
    lo = (x - hi.astype(F32)).astype(BF16)
    return hi, lo


def _seg_sum(x, seg):
    m, w = x.shape
    xb = x.astype(BF16)
    blk = seg.shape[0]
    if w <= blk:
        return jnp.dot(xb, seg[:w, :w], preferred_element_type=F32)
    n = w // blk
    stacked = jnp.concatenate([xb[:, i * blk:(i + 1) * blk] for i in range(n)], axis=0)
    res = jnp.dot(stacked, seg, preferred_element_type=F32)
    return jnp.concatenate([res[i * m:(i + 1) * m] for i in range(n)], axis=1)


def _sigmoid(z):
    return 0.5 * jnp.tanh(0.5 * z) + 0.5


def _mod_kernel(c_ref, w_ref, b_ref, o_ref):
    c = c_ref[...]
    s = c * _sigmoid(c)
    s_hi, s_lo = _split(s)
    w = w_ref[...]
    w_hi, w_lo = _split(w)
    d = lambda x, y: jnp.dot(x, y, preferred_element_type=F32)
    n = c.shape[0]
    hh = d(jnp.concatenate([s_hi, s_lo], axis=0), w_hi)
    o_ref[...] = hh[:n] + d(s_hi, w_lo) + hh[n:] + b_ref[...]


def _modulation(c, w_ada, b_ada):
    B, D = c.shape
    n = w_ada.shape[1]
    blk = D
    return pl.pallas_call(
        _mod_kernel,
        grid=(n // blk,),
        in_specs=[pl.BlockSpec((B, D), lambda j: (0, 0)),
                  pl.BlockSpec((D, blk), lambda j: (0, j)),
                  pl.BlockSpec((1, blk), lambda j: (0, j))],
        out_specs=pl.BlockSpec((B, blk), lambda j: (0, j)),
        out_shape=jax.ShapeDtypeStruct((B, n), F32),
        name="adaln_mod",
    )(c, w_ada, b_ada.reshape(1, n))


def _t5_thresholds():
    max_exact = NUM_BUCKETS // 2
    out = []
    for k in range(1, NUM_BUCKETS - max_exact):
        edge = max_exact * (MAX_DISTANCE / max_exact) ** (k / (NUM_BUCKETS - max_exact))
        assert abs(edge - round(edge)) > 1e-6, "bucket edge on an integer distance: floor() would be ambiguous"
        out.append(math.ceil(edge))
    return tuple(out)


_T5_THRESHOLDS = _t5_thresholds()


def _bias_kernel(rb_ref, o_ref):
    qi = lax.broadcasted_iota(jnp.int32, (ATT_BLOCK, ATT_BLOCK), 0)
    kj = lax.broadcasted_iota(jnp.int32, (ATT_BLOCK, ATT_BLOCK), 1)
    n = jnp.where(kj <= qi, qi - kj, qi + ATT_BLOCK - kj)
    max_exact = NUM_BUCKETS // 2
    large = jnp.full(n.shape, max_exact, jnp.int32)
    for t in _T5_THRESHOLDS:
        large = large + jnp.where(n >= t, 1, 0)
    bucket = jnp.where(n < max_exact, n, large)
    for h in range(N_Q_HEADS):
        acc = jnp.zeros((ATT_BLOCK, ATT_BLOCK), F32)
        for b in range(NUM_BUCKETS):
            acc = jnp.where(bucket == b, rb_ref[b, h], acc)
        o_ref[h] = acc * LOG2E


def _bias_table(rel_bias):
    assert WINDOW == ATT_BLOCK
    return pl.pallas_call(
        _bias_kernel,
        in_specs=[pl.BlockSpec(memory_space=pltpu.SMEM)],
        out_specs=pl.BlockSpec(memory_space=pltpu.VMEM),
        out_shape=jax.ShapeDtypeStruct((N_Q_HEADS, ATT_BLOCK, ATT_BLOCK), F32),
        name="rel_bias_table",
    )(rel_bias)


def _inproj_kernel(x_ref, mod_ref, g1_ref, win_ref, mu_ref, w0_ref, wlu_ref, a0_ref, alu_ref,
                   glu_ref, kk_ref, ka_ref, rk_ref, qg_ref, kg_ref, seg_ref, tri_ref, *rest):
    n_cast = (len(rest) - 14) // 2
    cast_in, rest = rest[:n_cast], rest[n_cast:]
    (at_out, bt_out, kt_out, rt_out, bh_out, kh_out, v_out, wl_out, bonus_out, g_out,
     q_out, ak_out, av_out) = rest[:13]
    cast_out, carry_ref = rest[13:13 + n_cast], rest[-1]

    @pl.when(pl.program_id(1) == 0)
    def _():
        carry_ref[...] = jnp.zeros_like(carry_ref)

    for src, dst in zip(cast_in, cast_out):
        dst[...] = src[...].astype(BF16)

    tm = tri_ref.shape[0]
    refs = (mu_ref, w0_ref, wlu_ref, a0_ref, alu_ref, glu_ref, kk_ref, ka_ref, rk_ref, qg_ref, kg_ref,
            seg_ref, tri_ref)
    outs = (at_out, bt_out, kt_out, rt_out, bh_out, kh_out, v_out, wl_out, bonus_out, g_out, q_out,
            ak_out, av_out)
    tail = iter(())
    gain = g1_ref[...] * (1.0 + mod_ref[0, 1:2, :])
    for s in range(x_ref.shape[1] // tm):
        rs = pl.ds(s * tm, tm)
        x = x_ref[0, rs, :]
        ms = jnp.mean(x * x, axis=-1, keepdims=True)
        h = (x * lax.rsqrt(ms + NORM_EPS) * gain + mod_ref[0, 0:1, :]).astype(BF16)
        pieces = []
        for lo in range(0, win_ref.shape[1], MXU_TILE):
            pieces.append(jnp.dot(h, win_ref[:, lo:lo + MXU_TILE], preferred_element_type=F32))
            next(tail, None)
        for _ in tail:
            pass
        tail = _inproj_tail(jnp.concatenate(pieces, axis=1), rs, s * (tm // CHUNK), refs, outs, carry_ref)
    for _ in tail:
        pass


def _inproj_tail(p, rs, c0, refs, outs, carry_ref):
    (mu_ref, w0_ref, wlu_ref, a0_ref, alu_ref, glu_ref, kk_ref, ka_ref, rk_ref, qg_ref, kg_ref,
     seg_ref, tri_ref) = refs
    (at_out, bt_out, kt_out, rt_out, bh_out, kh_out, v_out, wl_out, bonus_out, g_out, q_out,
     ak_out, av_out) = outs
    tm = p.shape[0]

    mm = lambda a, b: jnp.dot(a, b, preferred_element_type=F32)

    pr = p[:, :RWKV_COLS]
    rolled = pltpu.roll(pr, 1, 0)
    first = lax.broadcasted_iota(jnp.int32, (F32_SUBLANES, RWKV_COLS), 0) == 0
    prev = jnp.concatenate([jnp.where(first, carry_ref[...], rolled[:F32_SUBLANES]), rolled[F32_SUBLANES:]],
                           axis=0)
    carry_ref[...] = pr[tm - 1:tm, :]
    pm = pr + (prev - pr) * mu_ref[...]
    r = pm[:, 0:RWKV_W]
    k = pm[:, RWKV_W:2 * RWKV_W]
    v = pm[:, 2 * RWKV_W:3 * RWKV_W]
    o = 3 * RWKV_W
    tanh_w = jnp.tanh(pm[:, o:o + DECAY_LORA]).astype(BF16)
    xa = pm[:, o + DECAY_LORA:o + DECAY_LORA + AAA_LORA].astype(BF16)
    sig_g = _sigmoid(pm[:, o + DECAY_LORA + AAA_LORA:RWKV_COLS]).astype(BF16)
    yield
    q = p[:, RWKV_COLS:RWKV_COLS + ATT_Q_W]
    ak = p[:, RWKV_COLS + ATT_Q_W:RWKV_COLS + ATT_Q_W + ATT_KV_W]
    av = p[:, RWKV_COLS + ATT_Q_W + ATT_KV_W:]
    kk = k * kk_ref[...]
    kk_sq, q_sq, ak_sq = kk * kk, q * q, ak * ak
    v_out[0, rs, :] = v.astype(BF16)
    av_out[0, rs, :] = av.astype(BF16)
    yield

    seg = seg_ref[...]
    z_mm = mm(tanh_w, wlu_ref[...].astype(BF16))
    a_mm = mm(xa, alu_ref[...].astype(BF16))
    g_mm = mm(sig_g, glu_ref[...].astype(BF16))
    kk_ss = _seg_sum(kk_sq, seg)
    q_ss = _seg_sum(q_sq, seg)
    ak_ss = _seg_sum(ak_sq, seg)
    yield

    lw = (-math.exp(-0.5) * math.log2(math.e)) * _sigmoid(w0_ref[...] + z_mm)
    lw_hi, lw_lo = _split(lw)
    a = _sigmoid(a0_ref[...] + a_mm)
    g_out[0, rs, :] = g_mm
    kk = kk * lax.rsqrt(jnp.maximum(kk_ss, 1e-24))
    bv = kk * a
    k = k * (1.0 + (a - 1.0) * ka_ref[...])
    rk_prod = r * k * rk_ref[...]
    yield
    q_out[0, rs, :] = (q * lax.rsqrt(q_ss * (1.0 / HEAD_DIM) + NORM_EPS) * (qg_ref[...] * (ATT_SCALE * LOG2E))).astype(BF16)
    ak_out[0, rs, :] = (ak * lax.rsqrt(ak_ss * (1.0 / HEAD_DIM) + NORM_EPS) * kg_ref[...]).astype(BF16)
    yield

    tri = tri_ref[...]
    cum = mm(tri, lw_hi) + mm(tri, lw_lo)
    bonus_ss = _seg_sum(rk_prod, seg)
    yield

    n_chunks = tm // CHUNK
    ends = [cum[(c + 1) * CHUNK - 1:(c + 1) * CHUNK, :] for c in range(n_chunks)]
    cl = jnp.concatenate([jnp.broadcast_to(e, (CHUNK, RWKV_W)) for e in ends], axis=0)
    for c in range(n_chunks):
        wl_out[0, c0 + c] = jnp.exp2(ends[c])
    bonus_out[0, rs, :] = bonus_ss * v
    yield
    e_neg = jnp.exp2(-cum)
    bt_out[0, rs, :] = (bv * e_neg).astype(BF16)
    kt_out[0, rs, :] = (k * e_neg).astype(BF16)
    yield
    e_hat = jnp.exp2(cl - cum)
    bh_out[0, rs, :] = (bv * e_hat).astype(BF16)
    kh_out[0, rs, :] = (k * e_hat).astype(BF16)
    yield
    at_out[0, rs, :] = (-kk * jnp.exp2(cum - lw)).astype(BF16)
    rt_out[0, rs, :] = (r * jnp.exp2(cum)).astype(BF16)


def _in_projection(x, mod3, g1, w_in, mu, w0, wlu, a0, alu, glu, k_k, k_a, r_k, qg, kg, seg, tri, to_cast):
    B, S, D = x.shape
    tm = min(IN_TILE, S)
    nj = S // tm
    n_steps = B * nj
    const = lambda shape: pl.BlockSpec(shape, lambda b, j: (0,) * len(shape))
    row_spec = lambda w: pl.BlockSpec((1, tm, w), lambda b, j: (b, j, 0))
    rows = lambda w, dt: jax.ShapeDtypeStruct((B, S, w), dt)

    def cast_spec(w):
        n = n_steps
        while n > 1 and (n_steps % n or w.shape[0] % (BF16_SUBLANES * n)):
            n -= 1
        per = n_steps // n
        return pl.BlockSpec((w.shape[0] // n, w.shape[1]), lambda b, j: ((b * nj + j) // per, 0))

    cast_specs = [cast_spec(w) for w in to_cast]
    return pl.pallas_call(
        _inproj_kernel,
        grid=(B, nj),
        in_specs=[row_spec(D),
                  pl.BlockSpec((1, N_MOD, D), lambda b, j: (b, 0, 0)),
                  const(g1.shape), const(w_in.shape), const(mu.shape), const(w0.shape),
                  const(wlu.shape), const(a0.shape), const(alu.shape), const(glu.shape),
                  const(k_k.shape), const(k_a.shape), const(r_k.shape), const(qg.shape),
                  const(kg.shape), const(seg.shape), const(tri.shape)] + cast_specs,
        out_specs=([row_spec(RWKV_W)] * 7
                   + [pl.BlockSpec((1, tm // CHUNK, 1, RWKV_W), lambda b, j: (b, j, 0, 0))]
                   + [row_spec(RWKV_W)] * 2
                   + [row_spec(ATT_Q_W), row_spec(ATT_KV_W), row_spec(ATT_KV_W)] + cast_specs),
        out_shape=([rows(RWKV_W, BF16)] * 7
                   + [jax.ShapeDtypeStruct((B, S // CHUNK, 1, RWKV_W), F32)]
                   + [rows(RWKV_W, F32)] * 2
                   + [rows(ATT_Q_W, BF16), rows(ATT_KV_W, BF16), rows(ATT_KV_W, BF16)]
                   + [jax.ShapeDtypeStruct(w.shape, BF16) for w in to_cast]),
        scratch_shapes=[pltpu.VMEM((1, RWKV_COLS), F32)],
        compiler_params=pltpu.CompilerParams(dimension_semantics=("arbitrary", "arbitrary"),
                                             vmem_limit_bytes=VMEM_LIMIT),
        name="in_projection",
    )(x, mod3, g1, w_in, mu, w0, wlu, a0, alu, glu, k_k, k_a, r_k, qg, kg, seg, tri, *to_cast)


def _mixer_kernel(at_ref, bt_ref, kt_ref, rt_ref, bh_ref, kh_ref, v_ref, wl_ref, bonus_ref, g_ref,
                  lg_ref, lb_ref, seg_ref, q_ref, kc_ref, kp_ref, vc_ref, vp_ref, bias_ref, sink_ref,
                  o_ref, oatt_ref, state_ref):
    @pl.when(pl.program_id(1) == 0)
    def _():
        state_ref[...] = jnp.zeros_like(state_ref)

    L, D, P = CHUNK, HEAD_DIM, 2 * HEAD_DIM
    n_batch = at_ref.shape[0]
    n_chunks = at_ref.shape[1] // L
    n_pairs = RWKV_W // P
    row = lax.broadcasted_iota(jnp.int32, (L, P), 0)
    lane = lax.broadcasted_iota(jnp.int32, (L, P), 1)
    low = lane < D
    s_idx = jnp.where(low, lane, lane - D)
    strict = row > s_idx
    incl = row >= s_idx
    zeros_lp = jnp.zeros((L, P), F32)
    own = (lambda t: jnp.where(low, t, jnp.zeros_like(t)), lambda t: jnp.where(low, jnp.zeros_like(t), t))
    other = (own[1], own[0])
    diag = (lane == row, lane == row + D)
    seg = seg_ref[...]

    chunks = [(bi, c) for c in range(n_chunks) for bi in range(n_batch)]
    items = [(ck, j, par) for ck in chunks for j in range(n_pairs) for par in range(2)]
    chunk_in, top, bot, v_o, x = {}, {}, {}, {}, {}

    def front():
        for ck in chunks:
            bi, c = ck
            rows = pl.ds(c * L, L)
            ci = dict(a_t=at_ref[bi, rows, :], b_t=bt_ref[bi, rows, :], k_t=kt_ref[bi, rows, :],
                      r_t=rt_ref[bi, rows, :], b_h=bh_ref[bi, rows, :], k_h=kh_ref[bi, rows, :],
                      v=v_ref[bi, rows, :], w_l=wl_ref[bi, c])
            chunk_in[ck] = ci
            for j in range(n_pairs):
                ps = slice(j * P, (j + 1) * P)
                atp, rtp = ci["a_t"][:, ps], ci["r_t"][:, ps]
                btp, ktp = ci["b_t"][:, ps], ci["k_t"][:, ps]
                s1 = _dot_nt(jnp.concatenate([atp, rtp], axis=0),
                             jnp.concatenate([own[0](btp), own[0](ktp), own[1](btp), own[1](ktp)], axis=0))
                v_swap = pltpu.roll(ci["v"][:, ps].astype(F32), D, 1)
                for par in range(2):
                    it = (ck, j, par)
                    top[it] = jnp.where(strict, s1[:L, P * par:P * (par + 1)], 0.0)
                    bot[it] = jnp.where(incl, s1[L:, P * par:P * (par + 1)], 0.0)
                    v_o[it] = other[par](v_swap)
                    x[it] = own[par](atp).astype(F32)
        yield
        for it in items:
            x[it] = x[it] + _dot(top[it], jnp.concatenate([zeros_lp, v_o[it]], axis=0))
        yield
        a_pow = {it: top[it][:, :D].astype(BF16) for it in items}
        n = 1
        while n < L:
            last = 2 * n >= L
            for it in items:
                xb = x[it].astype(BF16)
                if last:
                    x[it] = x[it] + jnp.dot(a_pow[it], xb, preferred_element_type=F32)
                else:
                    res = jnp.dot(a_pow[it], jnp.concatenate([xb, a_pow[it]], axis=1),
                                  preferred_element_type=F32)
                    x[it] = x[it] + res[:, :P]
                    a_pow[it] = res[:, P:].astype(BF16)
            n *= 2
            yield

    def back():
        rhs2 = {it: jnp.concatenate([x[it], v_o[it]], axis=0).astype(BF16) for it in items}
        rb = {it: jnp.dot(bot[it].astype(BF16), rhs2[it], preferred_element_type=F32) for it in items}
        mn = {}
        for ck in chunks:
            ci = chunk_in[ck]
            for j in range(n_pairs):
                ps = slice(j * P, (j + 1) * P)
                res = _dot_tn(jnp.concatenate([ci["b_h"][:, ps], ci["k_h"][:, ps]], axis=0),
                              jnp.concatenate([rhs2[(ck, j, 0)], rhs2[(ck, j, 1)]], axis=1))
                for par in range(2):
                    mn[(ck, j, par)] = res[D * par:D * (par + 1), P * par:P * (par + 1)]
        yield
        y = {}
        for ck in chunks:
            ci = chunk_in[ck]
            for j in range(n_pairs):
                ps = slice(j * P, (j + 1) * P)
                for par in range(2):
                    it = (ck, j, par)
                    h = (ck[0], 2 * j + par)
                    g_p = rb[it] + ci["r_t"][:, ps]
                    m_p = mn[it] + jnp.where(diag[par], ci["w_l"][:, ps], 0.0)
                    z = state_ref[h]
                    rhs = jnp.concatenate([z, zeros_lp] if par == 0 else [zeros_lp, z], axis=0)
                    res = _dot(jnp.concatenate([g_p, m_p], axis=0), rhs)
                    y[it] = res[:L] + rb[it]
                    state_ref[h] = other[par](res[L:] + mn[it])
            if ck[0] == n_batch - 1:
                yield
        for ck in chunks:
            bi, c = ck
            rows = pl.ds(c * L, L)
            yc = jnp.concatenate(
                [pltpu.roll(jnp.where(low, y[ck, j, 1], y[ck, j, 0]), D, 1) for j in range(n_pairs)], axis=1)
            mu = _seg_sum(yc, seg) * (1.0 / HEAD_DIM)
            yc = yc - mu
            var = _seg_sum(yc * yc, seg) * (1.0 / HEAD_DIM)
            yn = yc * lax.rsqrt(var + RWKV_GN_EPS) * lg_ref[...] + lb_ref[...]
            o_ref[bi, rows, :] = ((yn + bonus_ref[bi, rows, :]) * g_ref[bi, rows, :]).astype(BF16)
            if bi == n_batch - 1:
                yield

    swa = _swa_stages(q_ref, kc_ref, kp_ref, vc_ref, vp_ref, bias_ref, sink_ref, oatt_ref)
    next(swa)
    for _ in front():
        next(swa, None)
    for _ in back():
        next(swa, None)
    for _ in swa:
        pass


def _mixers(at, bt, kt, rt, bh, kh, v, wl, bonus, g, lnx_g, lnx_b, seg, q, ak, av, bias, sinks):
    B, S, W = at.shape
    tc = min(MIX_TILE, S)
    nbat = MIX_BATCH if B % MIX_BATCH == 0 else 1
    nb = tc // ATT_BLOCK
    bias = bias.reshape(N_KV_HEADS, GQA_GROUP * ATT_BLOCK, ATT_BLOCK)
    const = lambda shape: pl.BlockSpec(shape, lambda b, j: (0,) * len(shape))
    rows = lambda w: pl.BlockSpec((nbat, tc, w), lambda b, j: (b, j, 0))
    prev_block = lambda w: pl.BlockSpec((nbat, ATT_BLOCK, w), lambda b, j: (b, jnp.maximum(j * nb - 1, 0), 0))
    return pl.pallas_call(
        _mixer_kernel,
        grid=(B // nbat, S // tc),
        in_specs=([rows(W)] * 7 + [pl.BlockSpec((nbat, tc // CHUNK, 1, W), lambda b, j: (b, j, 0, 0))]
                  + [rows(W)] * 2 + [const(lnx_g.shape), const(lnx_b.shape), const(seg.shape)]
                  + [rows(ATT_Q_W), rows(ATT_KV_W), prev_block(ATT_KV_W), rows(ATT_KV_W), prev_block(ATT_KV_W),
                     const(bias.shape), pl.BlockSpec(memory_space=pltpu.SMEM)]),
        out_specs=[rows(W), rows(ATT_Q_W)],
        out_shape=[jax.ShapeDtypeStruct((B, S, W), BF16), jax.ShapeDtypeStruct((B, S, ATT_Q_W), BF16)],
        scratch_shapes=[pltpu.VMEM((nbat, RWKV_HEADS, CHUNK, 2 * HEAD_DIM), F32)],
        compiler_params=pltpu.CompilerParams(dimension_semantics=("arbitrary", "arbitrary"),
                                             vmem_limit_bytes=VMEM_LIMIT),
        name="rwkv7_swa_mixers",
    )(at, bt, kt, rt, bh, kh, v, wl, bonus, g, lnx_g, lnx_b, seg, q, ak, ak, av, av, bias, sinks)


def _swa_stages(q_ref, kc_ref, kp_ref, vc_ref, vp_ref, bias_ref, sink_ref, o_ref):
    first = pl.program_id(1) == 0
    rows = GQA_GROUP * ATT_BLOCK
    col = lax.broadcasted_iota(jnp.int32, (rows, ATT_BLOCK), 1)
    qpos = lax.broadcasted_iota(jnp.int32, (rows, ATT_BLOCK), 0) % ATT_BLOCK
    use_cur = col <= qpos
    no_key = jnp.logical_and(first, jnp.logical_not(use_cur))
    row = lax.broadcasted_iota(jnp.int32, (rows, 1), 0)
    sink = []
    for hk in range(N_KV_HEADS):
        s = jnp.full((rows, 1), sink_ref[hk * GQA_GROUP], F32)
        for g in range(1, GQA_GROUP):
            s = jnp.where(row >= g * ATT_BLOCK, sink_ref[hk * GQA_GROUP + g], s)
        sink.append(s * LOG2E)

    def window(ref_cur, ref_prev, bi, blk, hk):
        ks = slice(hk * HEAD_DIM, (hk + 1) * HEAD_DIM)
        prev = (ref_prev[bi, :, ks] if blk == 0
                else ref_cur[bi, pl.ds((blk - 1) * ATT_BLOCK, ATT_BLOCK), ks])
        return jnp.concatenate([prev, ref_cur[bi, pl.ds(blk * ATT_BLOCK, ATT_BLOCK), ks]], axis=0)

    items = [(bi, blk, hk) for bi in range(q_ref.shape[0]) for blk in range(q_ref.shape[1] // ATT_BLOCK)
             for hk in range(N_KV_HEADS)]
    logits = {}
    for bi, blk, hk in items:
        q = q_ref[bi, pl.ds(blk * ATT_BLOCK, ATT_BLOCK), :]
        qg = jnp.concatenate([q[:, (hk * GQA_GROUP + g) * HEAD_DIM:(hk * GQA_GROUP + g + 1) * HEAD_DIM]
                              for g in range(GQA_GROUP)], axis=0)
        lg = lax.dot_general(qg, window(kc_ref, kp_ref, bi, blk, hk), (((1,), (1,)), ((), ())),
                             preferred_element_type=F32)
        lg = jnp.where(use_cur, lg[:, ATT_BLOCK:], lg[:, :ATT_BLOCK]) + bias_ref[hk]
        logits[bi, blk, hk] = jnp.where(no_key, NEG_INF, lg) if blk == 0 else lg
    yield

    e2, e_sink = {}, {}
    for it in items:
        m = jnp.maximum(jnp.max(logits[it], axis=-1, keepdims=True), sink[it[2]])
        eb = jnp.exp2(logits.pop(it) - m).astype(BF16)
        e_sink[it] = jnp.exp2(sink[it[2]] - m)
        zero = jnp.zeros_like(eb)
        e2[it] = jnp.concatenate([jnp.where(use_cur, zero, eb), jnp.where(use_cur, eb, zero)], axis=1)
        yield

    low = lax.broadcasted_iota(jnp.int32, (ATT_BLOCK, 2 * HEAD_DIM), 1) < HEAD_DIM
    for it in items:
        bi, blk, hk = it
        v2 = window(vc_ref, vp_ref, bi, blk, hk)
        vw = jnp.concatenate([v2, v2, jnp.ones((2 * ATT_BLOCK, 2 * HEAD_DIM), BF16)], axis=1)
        res = jnp.dot(e2.pop(it), vw, preferred_element_type=F32)
        out = res[:, :2 * HEAD_DIM] / (res[:, 2 * HEAD_DIM:] + e_sink.pop(it))
        for g in range(0, GQA_GROUP, 2):
            h = hk * GQA_GROUP + g
            pair = jnp.where(low, out[g * ATT_BLOCK:(g + 1) * ATT_BLOCK], out[(g + 1) * ATT_BLOCK:(g + 2) * ATT_BLOCK])
            o_ref[bi, pl.ds(blk * ATT_BLOCK, ATT_BLOCK), h * HEAD_DIM:(h + 2) * HEAD_DIM] = pair.astype(BF16)
        yield


def _ffn_kernel(x_ref, yr_ref, ya_ref, mod_ref, g2_ref, wo_ref, wg_ref, wu_ref, wd_ref, o_ref):
    n_sub = max(x_ref.shape[1] // FFN_SUB, 1)
    tm = x_ref.shape[1] // n_sub
    subs = [_ffn_rows(pl.ds(s * tm, tm), x_ref, yr_ref, ya_ref, mod_ref, g2_ref, wo_ref, wg_ref, wu_ref,
                      wd_ref, o_ref) for s in range(n_sub)]
    done = [False] * n_sub
    t = 0
    while not all(done):
        for s in range(min(t + 1, n_sub)):
            if not done[s]:
                done[s] = next(subs[s], True) is True
        t += 1


def _ffn_rows(rs, x_ref, yr_ref, ya_ref, mod_ref, g2_ref, wo_ref, wg_ref, wu_ref, wd_ref, o_ref):
    x = x_ref[0, rs, :]
    wo = wo_ref[...]
    mix = (jnp.dot(yr_ref[0, rs, :], wo[:RWKV_W], preferred_element_type=F32)
           + jnp.dot(ya_ref[0, rs, :], wo[RWKV_W:], preferred_element_type=F32))
    yield
    h_res = x + mod_ref[0, 2:3, :] * mix
    ms = jnp.mean(h_res * h_res, axis=-1, keepdims=True)
    h2 = (h_res * lax.rsqrt(ms + NORM_EPS) * (g2_ref[...] * (1.0 + mod_ref[0, 4:5, :]))
          + mod_ref[0, 3:4, :]).astype(BF16)
    yield
    d_ff = wg_ref.shape[1]
    ffn = jnp.zeros_like(x)
    for lo in range(0, d_ff, FFN_BLOCK):
        cs = slice(lo, min(lo + FFN_BLOCK, d_ff))
        gt = jnp.dot(h2, wg_ref[:, cs], preferred_element_type=F32)
        up = jnp.dot(h2, wu_ref[:, cs], preferred_element_type=F32)
        yield
        act = (gt * _sigmoid(gt) * up).astype(BF16)
        ffn = ffn + jnp.dot(act, wd_ref[cs, :], preferred_element_type=F32)
        yield
    o_ref[0, rs, :] = h_res + mod_ref[0, 5:6, :] * ffn


def _out_ffn(x, y_rwkv, y_att, mod3, g2, w_out, w_gate, w_up, w_down):
    B, S, D = x.shape
    tm = min(FFN_TILE, S)
    resident = lambda shape: pl.BlockSpec(shape, lambda b, j: (0,) * len(shape),
                                          pipeline_mode=pl.Buffered(1))
    row_spec = lambda w: pl.BlockSpec((1, tm, w), lambda b, j: (b, j, 0))
    return pl.pallas_call(
        _ffn_kernel,
        grid=(B, S // tm),
        in_specs=[row_spec(D), row_spec(RWKV_W), row_spec(ATT_Q_W),
                  pl.BlockSpec((1, N_MOD, D), lambda b, j: (b, 0, 0)),
                  resident(g2.shape), resident(w_out.shape), resident(w_gate.shape),
                  resident(w_up.shape), resident(w_down.shape)],
        out_specs=row_spec(D),
        out_shape=jax.ShapeDtypeStruct((B, S, D), F32),
        compiler_params=pltpu.CompilerParams(dimension_semantics=("arbitrary", "arbitrary"),
                                             vmem_limit_bytes=VMEM_LIMIT),
        name="out_proj_ffn",
    )(x, y_rwkv, y_att, mod3, g2, w_out, w_gate, w_up, w_down)


def kernel(x, c, w_ada, b_ada, norm1_g, w_in, rwkv_mu, w0, w_lora_up, a0, a_lora_up, g_lora_up, k_k, k_a, r_k, lnx_g, lnx_b, q_norm_g, k_norm_g, sinks, rel_bias, w_out, norm2_g, w_gate, w_up, w_down):
    B, S, D = x.shape
    row = lambda t: t.reshape(1, -1).astype(F32)
    lane = jnp.arange(MXU_TILE) // HEAD_DIM
    seg = (lane[:, None] == lane[None, :]).astype(BF16)
    t_idx = jnp.arange(min(IN_SUB, S))
    tri = ((t_idx[:, None] >= t_idx[None, :])
           & (t_idx[:, None] // CHUNK == t_idx[None, :] // CHUNK)).astype(BF16)

    mod3 = _modulation(c.astype(F32), w_ada, b_ada).reshape(B, N_MOD, D)
    bias = _bias_table(rel_bias)

    (at, bt, kt, rt, bh, kh, v, wl, bonus, g, q, ak, av,
     w_out16, w_gate16, w_up16, w_down16) = _in_projection(
        x, mod3, row(norm1_g), w_in.astype(BF16), row(rwkv_mu), row(w0), w_lora_up.astype(F32),
        row(a0), a_lora_up.astype(F32), g_lora_up.astype(F32), row(k_k), row(k_a), row(r_k),
        row(jnp.tile(q_norm_g, N_Q_HEADS)), row(jnp.tile(k_norm_g, N_KV_HEADS)), seg, tri,
        (w_out.astype(F32), w_gate.astype(F32), w_up.astype(F32), w_down.astype(F32)))

    y_rwkv, y_att = _mixers(at, bt, kt, rt, bh, kh, v, wl, bonus, g, row(lnx_g), row(lnx_b), seg,
                            q, ak, av, bias, sinks)

    out = _out_ffn(x, y_rwkv, y_att, mod3, row(norm2_g), w_out16, w_gate16, w_up16, w_down16)
    return out.astype(x.dtype)
```

```python
import math

import jax
import jax.numpy as jnp
from jax import lax
from jax.experimental import pallas as pl
from jax.experimental.pallas import tpu as pltpu

F32 = jnp.float32
BF16 = jnp.bfloat16

HEAD_DIM = 64
RWKV_HEADS = 8
RWKV_W = RWKV_HEADS * HEAD_DIM
DECAY_LORA = 64
AAA_LORA = 64
GATE_LORA = 128
RWKV_COLS = 3 * RWKV_W + DECAY_LORA + AAA_LORA + GATE_LORA
RWKV_GN_EPS = 64e-5
N_Q_HEADS = 8
N_KV_HEADS = 2
GQA_GROUP = N_Q_HEADS // N_KV_HEADS
ATT_Q_W = N_Q_HEADS * HEAD_DIM
ATT_KV_W = N_KV_HEADS * HEAD_DIM
ATT_BLOCK = 128
WINDOW = 128
ATT_SCALE = 1.0 / math.sqrt(HEAD_DIM)
LOG2E = math.log2(math.e)
NUM_BUCKETS = 32
MAX_DISTANCE = 128
N_MOD = 6
NORM_EPS = 1e-6
NEG_INF = -1e30

CHUNK = 64
IN_TILE = 1024
IN_SUB = 256
MIX_TILE = 512
MIX_BATCH = 1
FFN_TILE = 512
FFN_SUB = 256
MXU_TILE = 256
F32_SUBLANES = 8
BF16_SUBLANES = 16
FFN_BLOCK = 4 * MXU_TILE
VMEM_LIMIT = 56 * 1024 * 1024


def _dot(a, b):
    return jnp.dot(a.astype(BF16), b.astype(BF16), preferred_element_type=F32)


def _dot_nt(a, b):
    return lax.dot_general(a.astype(BF16), b.astype(BF16), (((1,), (1,)), ((), ())),
                           preferred_element_type=F32)


def _dot_tn(a, b):
    return lax.dot_general(a.astype(BF16), b.astype(BF16), (((0,), (0,)), ((), ())),
                           preferred_element_type=F32)


def _split(x):
    hi = x.astype(BF16)
    lo = (x - hi.astype(F32)).astype(BF16)
    return hi, lo


def _seg_sum(x, seg):
    m, w = x.shape
    xb = x.astype(BF16)
    blk = seg.shape[0]
    if w <= blk:
        return jnp.dot(xb, seg[:w, :w], preferred_element_type=F32)
    n = w // blk
    stacked = jnp.concatenate([xb[:, i * blk:(i + 1) * blk] for i in range(n)], axis=0)
    res = jnp.dot(stacked, seg, preferred_element_type=F32)
    return jnp.concatenate([res[i * m:(i + 1) * m] for i in range(n)], axis=1)


def _sigmoid(z):
    return 0.5 * jnp.tanh(0.5 * z) + 0.5


def _mod_kernel(c_ref, w_ref, b_ref, o_ref):
    c = c_ref[...]
    s = c * _sigmoid(c)
    s_hi, s_lo = _split(s)
    w = w_ref[...]
    w_hi, w_lo = _split(w)
    d = lambda x, y: jnp.dot(x, y, preferred_element_type=F32)
    n = c.shape[0]
    hh = d(jnp.concatenate([s_hi, s_lo], axis=0), w_hi)
    o_ref[...] = hh[:n] + d(s_hi, w_lo) + hh[n:] + b_ref[...]


def _modulation(c, w_ada, b_ada):
    B, D = c.shape
    n = w_ada.shape[1]
    blk = D
    return pl.pallas_call(
        _mod_kernel,
        grid=(n // blk,),
        in_specs=[pl.BlockSpec((B, D), lambda j: (0, 0)),
                  pl.BlockSpec((D, blk), lambda j: (0, j)),
                  pl.BlockSpec((1, blk), lambda j: (0, j))],
        out_specs=pl.BlockSpec((B, blk), lambda j: (0, j)),
        out_shape=jax.ShapeDtypeStruct((B, n), F32),
        name="adaln_mod",
    )(c, w_ada, b_ada.reshape(1, n))


def _t5_thresholds():
    max_exact = NUM_BUCKETS // 2
    out = []
    for k in range(1, NUM_BUCKETS - max_exact):
        edge = max_exact * (MAX_DISTANCE / max_exact) ** (k / (NUM_BUCKETS - max_exact))
        assert abs(edge - round(edge)) > 1e-6, "bucket edge on an integer distance: floor() would be ambiguous"
        out.append(math.ceil(edge))
    return tuple(out)


_T5_THRESHOLDS = _t5_thresholds()


def _bias_kernel(rb_ref, o_ref):
    qi = lax.broadcasted_iota(jnp.int32, (ATT_BLOCK, ATT_BLOCK), 0)
    kj = lax.broadcasted_iota(jnp.int32, (ATT_BLOCK, ATT_BLOCK), 1)
    n = jnp.where(kj <= qi, qi - kj, qi + ATT_BLOCK - kj)
    max_exact = NUM_BUCKETS // 2
    large = jnp.full(n.shape, max_exact, jnp.int32)
    for t in _T5_THRESHOLDS:
        large = large + jnp.where(n >= t, 1, 0)
    bucket = jnp.where(n < max_exact, n, large)
    for h in range(N_Q_HEADS):
        acc = jnp.zeros((ATT_BLOCK, ATT_BLOCK), F32)
        for b in range(NUM_BUCKETS):
            acc = jnp.where(bucket == b, rb_ref[b, h], acc)
        o_ref[h] = acc * LOG2E


def _bias_table(rel_bias):
    assert WINDOW == ATT_BLOCK
    return pl.pallas_call(
        _bias_kernel,
        in_specs=[pl.BlockSpec(memory_space=pltpu.SMEM)],
        out_specs=pl.BlockSpec(memory_space=pltpu.VMEM),
        out_shape=jax.ShapeDtypeStruct((N_Q_HEADS, ATT_BLOCK, ATT_BLOCK), F32),
        name="rel_bias_table",
    )(rel_bias)


def _inproj_kernel(x_ref, mod_ref, g1_ref, win_ref, mu_ref, w0_ref, wlu_ref, a0_ref, alu_ref,
                   glu_ref, kk_ref, ka_ref, rk_ref, qg_ref, kg_ref, seg_ref, tri_ref, *rest):
    n_cast = (len(rest) - 14) // 2
    cast_in, rest = rest[:n_cast], rest[n_cast:]
    (at_out, bt_out, kt_out, rt_out, bh_out, kh_out, v_out, wl_out, bonus_out, g_out,
     q_out, ak_out, av_out) = rest[:13]
    cast_out, carry_ref = rest[13:13 + n_cast], rest[-1]

    @pl.when(pl.program_id(1) == 0)
    def _():
        carry_ref[...] = jnp.zeros_like(carry_ref)

    for src, dst in zip(cast_in, cast_out):
        dst[...] = src[...].astype(BF16)

    tm = tri_ref.shape[0]
    refs = (mu_ref, w0_ref, wlu_ref, a0_ref, alu_ref, glu_ref, kk_ref, ka_ref, rk_ref, qg_ref, kg_ref,
            seg_ref, tri_ref)
    outs = (at_out, bt_out, kt_out, rt_out, bh_out, kh_out, v_out, wl_out, bonus_out, g_out, q_out,
            ak_out, av_out)
    tail = iter(())
    gain = g1_ref[...] * (1.0 + mod_ref[0, 1:2, :])
    for s in range(x_ref.shape[1] // tm):
        rs = pl.ds(s * tm, tm)
        x = x_ref[0, rs, :]
        ms = jnp.mean(x * x, axis=-1, keepdims=True)
        h = (x * lax.rsqrt(ms + NORM_EPS) * gain + mod_ref[0, 0:1, :]).astype(BF16)
        pieces = []
        for lo in range(0, win_ref.shape[1], MXU_TILE):
            pieces.append(jnp.dot(h, win_ref[:, lo:lo + MXU_TILE], preferred_element_type=F32))
            next(tail, None)
        for _ in tail:
            pass
        tail = _inproj_tail(jnp.concatenate(pieces, axis=1), rs, s * (tm // CHUNK), refs, outs, carry_ref)
    for _ in tail:
        pass


def _inproj_tail(p, rs, c0, refs, outs, carry_ref):
    (mu_ref, w0_ref, wlu_ref, a0_ref, alu_ref, glu_ref, kk_ref, ka_ref, rk_ref, qg_ref, kg_ref,
     seg_ref, tri_ref) = refs
    (at_out, bt_out, kt_out, rt_out, bh_out, kh_out, v_out, wl_out, bonus_out, g_out, q_out,
     ak_out, av_out) = outs
    tm = p.shape[0]

    mm = lambda a, b: jnp.dot(a, b, preferred_element_type=F32)

    pr = p[:, :RWKV_COLS]
    rolled = pltpu.roll(pr, 1, 0)
    first = lax.broadcasted_iota(jnp.int32, (F32_SUBLANES, RWKV_COLS), 0) == 0
    prev = jnp.concatenate([jnp.where(first, carry_ref[...], rolled[:F32_SUBLANES]), rolled[F32_SUBLANES:]],
                           axis=0)
    carry_ref[...] = pr[tm - 1:tm, :]
    pm = pr + (prev - pr) * mu_ref[...]
    r = pm[:, 0:RWKV_W]
    k = pm[:, RWKV_W:2 * RWKV_W]
    v = pm[:, 2 * RWKV_W:3 * RWKV_W]
    o = 3 * RWKV_W
    tanh_w = jnp.tanh(pm[:, o:o + DECAY_LORA]).astype(BF16)
    xa = pm[:, o + DECAY_LORA:o + DECAY_LORA + AAA_LORA].astype(BF16)
    sig_g = _sigmoid(pm[:, o + DECAY_LORA + AAA_LORA:RWKV_COLS]).astype(BF16)
    yield
    q = p[:, RWKV_COLS:RWKV_COLS + ATT_Q_W]
    ak = p[:, RWKV_COLS + ATT_Q_W:RWKV_COLS + ATT_Q_W + ATT_KV_W]
    av = p[:, RWKV_COLS + ATT_Q_W + ATT_KV_W:]
    kk = k * kk_ref[...]
    kk_sq, q_sq, ak_sq = kk * kk, q * q, ak * ak
    v_out[0, rs, :] = v.astype(BF16)
    av_out[0, rs, :] = av.astype(BF16)
    yield

    seg = seg_ref[...]
    z_mm = mm(tanh_w, wlu_ref[...].astype(BF16))
    a_mm = mm(xa, alu_ref[...].astype(BF16))
    g_mm = mm(sig_g, glu_ref[...].astype(BF16))
    kk_ss = _seg_sum(kk_sq, seg)
    q_ss = _seg_sum(q_sq, seg)
    ak_ss = _seg_sum(ak_sq, seg)
    yield

    lw = (-math.exp(-0.5) * math.log2(math.e)) * _sigmoid(w0_ref[...] + z_mm)
    lw_hi, lw_lo = _split(lw)
    a = _sigmoid(a0_ref[...] + a_mm)
    g_out[0, rs, :] = g_mm
    kk = kk * lax.rsqrt(jnp.maximum(kk_ss, 1e-24))
    bv = kk * a
    k = k * (1.0 + (a - 1.0) * ka_ref[...])
    rk_prod = r * k * rk_ref[...]
    yield
    q_out[0, rs, :] = (q * lax.rsqrt(q_ss * (1.0 / HEAD_DIM) + NORM_EPS) * (qg_ref[...] * (ATT_SCALE * LOG2E))).astype(BF16)
    ak_out[0, rs, :] = (ak * lax.rsqrt(ak_ss * (1.0 / HEAD_DIM) + NORM_EPS) * kg_ref[...]).astype(BF16)
    yield

    tri = tri_ref[...]
    cum = mm(tri, lw_hi) + mm(tri, lw_lo)
    bonus_ss = _seg_sum(rk_prod, seg)
    yield

    n_chunks = tm // CHUNK
    ends = [cum[(c + 1) * CHUNK - 1:(c + 1) * CHUNK, :] for c in range(n_chunks)]
    cl = jnp.concatenate([jnp.broadcast_to(e, (CHUNK, RWKV_W)) for e in ends], axis=0)
    for c in range(n_chunks):
        wl_out[0, c0 + c] = jnp.exp2(ends[c])
    bonus_out[0, rs, :] = bonus_ss * v
    yield
    e_neg = jnp.exp2(-cum)
    bt_out[0, rs, :] = (bv * e_neg).astype(BF16)
    kt_out[0, rs, :] = (k * e_neg).astype(BF16)
    yield
    e_hat = jnp.exp2(cl - cum)
    bh_out[0, rs, :] = (bv * e_hat).astype(BF16)
    kh_out[0, rs, :] = (k * e_hat).astype(BF16)
    yield
    at_out[0, rs, :] = (-kk * jnp.exp2(cum - lw)).astype(BF16)
    rt_out[0, rs, :] = (r * jnp.exp2(cum)).astype(BF16)


def _in_projection(x, mod3, g1, w_in, mu, w0, wlu, a0, alu, glu, k_k, k_a, r_k, qg, kg, seg, tri, to_cast):
    B, S, D = x.shape
    tm = min(IN_TILE, S)
    nj = S // tm
    n_steps = B * nj
    const = lambda shape: pl.BlockSpec(shape, lambda b, j: (0,) * len(shape))
    row_spec = lambda w: pl.BlockSpec((1, tm, w), lambda b, j: (b, j, 0))
    rows = lambda w, dt: jax.ShapeDtypeStruct((B, S, w), dt)

    def cast_spec(w):
        n = n_steps
        while n > 1 and (n_steps % n or w.shape[0] % (BF16_SUBLANES * n)):
            n -= 1
        per = n_steps // n
        return pl.BlockSpec((w.shape[0] // n, w.shape[1]), lambda b, j: ((b * nj + j) // per, 0))

    cast_specs = [cast_spec(w) for w in to_cast]
    return pl.pallas_call(
        _inproj_kernel,
        grid=(B, nj),
        in_specs=[row_spec(D),
                  pl.BlockSpec((1, N_MOD, D), lambda b, j: (b, 0, 0)),
                  const(g1.shape), const(w_in.shape), const(mu.shape), const(w0.shape),
                  const(wlu.shape), const(a0.shape), const(alu.shape), const(glu.shape),
                  const(k_k.shape), const(k_a.shape), const(r_k.shape), const(qg.shape),
                  const(kg.shape), const(seg.shape), const(tri.shape)] + cast_specs,
        out_specs=([row_spec(RWKV_W)] * 7
                   + [pl.BlockSpec((1, tm // CHUNK, 1, RWKV_W), lambda b, j: (b, j, 0, 0))]
                   + [row_spec(RWKV_W)] * 2
                   + [row_spec(ATT_Q_W), row_spec(ATT_KV_W), row_spec(ATT_KV_W)] + cast_specs),
        out_shape=([rows(RWKV_W, BF16)] * 7
                   + [jax.ShapeDtypeStruct((B, S // CHUNK, 1, RWKV_W), F32)]
                   + [rows(RWKV_W, F32)] * 2
                   + [rows(ATT_Q_W, BF16), rows(ATT_KV_W, BF16), rows(ATT_KV_W, BF16)]
                   + [jax.ShapeDtypeStruct(w.shape, BF16) for w in to_cast]),
        scratch_shapes=[pltpu.VMEM((1, RWKV_COLS), F32)],
        compiler_params=pltpu.CompilerParams(dimension_semantics=("arbitrary", "arbitrary"),
                                             vmem_limit_bytes=VMEM_LIMIT),
        name="in_projection",
    )(x, mod3, g1, w_in, mu, w0, wlu, a0, alu, glu, k_k, k_a, r_k, qg, kg, seg, tri, *to_cast)


def _mixer_kernel(at_ref, bt_ref, kt_ref, rt_ref, bh_ref, kh_ref, v_ref, wl_ref, bonus_ref, g_ref,
                  lg_ref, lb_ref, seg_ref, q_ref, kc_ref, kp_ref, vc_ref, vp_ref, bias_ref, sink_ref,
                  o_ref, oatt_ref, state_ref):
    @pl.when(pl.program_id(1) == 0)
    def _():
        state_ref[...] = jnp.zeros_like(state_ref)

    L, D, P = CHUNK, HEAD_DIM, 2 * HEAD_DIM
    n_batch = at_ref.shape[0]
    n_chunks = at_ref.shape[1] // L
    n_pairs = RWKV_W // P
    row = lax.broadcasted_iota(jnp.int32, (L, P), 0)
    lane = lax.broadcasted_iota(jnp.int32, (L, P), 1)
    low = lane < D
    s_idx = jnp.where(low, lane, lane - D)
    strict = row > s_idx
    incl = row >= s_idx
    zeros_lp = jnp.zeros((L, P), F32)
    own = (lambda t: jnp.where(low, t, jnp.zeros_like(t)), lambda t: jnp.where(low, jnp.zeros_like(t), t))
    other = (own[1], own[0])
    diag = (lane == row, lane == row + D)
    seg = seg_ref[...]

    chunks = [(bi, c) for c in range(n_chunks) for bi in range(n_batch)]
    items = [(ck, j, par) for ck in chunks for j in range(n_pairs) for par in range(2)]
    chunk_in, top, bot, v_o, x = {}, {}, {}, {}, {}

    def front():
        for ck in chunks:
            bi, c = ck
            rows = pl.ds(c * L, L)
            ci = dict(a_t=at_ref[bi, rows, :], b_t=bt_ref[bi, rows, :], k_t=kt_ref[bi, rows, :],
                      r_t=rt_ref[bi, rows, :], b_h=bh_ref[bi, rows, :], k_h=kh_ref[bi, rows, :],
                      v=v_ref[bi, rows, :], w_l=wl_ref[bi, c])
            chunk_in[ck] = ci
            for j in range(n_pairs):
                ps = slice(j * P, (j + 1) * P)
                atp, rtp = ci["a_t"][:, ps], ci["r_t"][:, ps]
                btp, ktp = ci["b_t"][:, ps], ci["k_t"][:, ps]
                s1 = _dot_nt(jnp.concatenate([atp, rtp], axis=0),
                             jnp.concatenate([own[0](btp), own[0](ktp), own[1](btp), own[1](ktp)], axis=0))
                v_swap = pltpu.roll(ci["v"][:, ps].astype(F32), D, 1)
                for par in range(2):
                    it = (ck, j, par)
                    top[it] = jnp.where(strict, s1[:L, P * par:P * (par + 1)], 0.0)
                    bot[it] = jnp.where(incl, s1[L:, P * par:P * (par + 1)], 0.0)
                    v_o[it] = other[par](v_swap)
                    x[it] = own[par](atp).astype(F32)
        yield
        for it in items:
            x[it] = x[it] + _dot(top[it], jnp.concatenate([zeros_lp, v_o[it]], axis=0))
        yield
        a_pow = {it: top[it][:, :D].astype(BF16) for it in items}
        n = 1
        while n < L:
            last = 2 * n >= L
            for it in items:
                xb = x[it].astype(BF16)
                if last:
                    x[it] = x[it] + jnp.dot(a_pow[it], xb, preferred_element_type=F32)
                else:
                    res = jnp.dot(a_pow[it], jnp.concatenate([xb, a_pow[it]], axis=1),
                                  preferred_element_type=F32)
                    x[it] = x[it] + res[:, :P]
                    a_pow[it] = res[:, P:].astype(BF16)
            n *= 2
            yield

    def back():
        rhs2 = {it: jnp.concatenate([x[it], v_o[it]], axis=0).astype(BF16) for it in items}
        rb = {it: jnp.dot(bot[it].astype(BF16), rhs2[it], preferred_element_type=F32) for it in items}
        mn = {}
        for ck in chunks:
            ci = chunk_in[ck]
            for j in range(n_pairs):
                ps = slice(j * P, (j + 1) * P)
                res = _dot_tn(jnp.concatenate([ci["b_h"][:, ps], ci["k_h"][:, ps]], axis=0),
                              jnp.concatenate([rhs2[(ck, j, 0)], rhs2[(ck, j, 1)]], axis=1))
                for par in range(2):
                    mn[(ck, j, par)] = res[D * par:D * (par + 1), P * par:P * (par + 1)]
        yield
        y = {}
        for ck in chunks:
            ci = chunk_in[ck]
            for j in range(n_pairs):
                ps = slice(j * P, (j + 1) * P)
                for par in range(2):
                    it = (ck, j, par)
                    h = (ck[0], 2 * j + par)
                    g_p = rb[it] + ci["r_t"][:, ps]
                    m_p = mn[it] + jnp.where(diag[par], ci["w_l"][:, ps], 0.0)
                    z = state_ref[h]
                    rhs = jnp.concatenate([z, zeros_lp] if par == 0 else [zeros_lp, z], axis=0)
                    res = _dot(jnp.concatenate([g_p, m_p], axis=0), rhs)
                    y[it] = res[:L] + rb[it]
                    state_ref[h] = other[par](res[L:] + mn[it])
            if ck[0] == n_batch - 1:
                yield
        for ck in chunks:
            bi, c = ck
            rows = pl.ds(c * L, L)
            o_ref[bi, rows, :] = jnp.concatenate(
                [pltpu.roll(jnp.where(low, y[ck, j, 1], y[ck, j, 0]), D, 1) for j in range(n_pairs)], axis=1)
            if bi == n_batch - 1:
                yield

    swa = _swa_stages(q_ref, kc_ref, kp_ref, vc_ref, vp_ref, bias_ref, sink_ref, oatt_ref)
    next(swa)
    for _ in front():
        next(swa, None)
    for _ in back():
        next(swa, None)
    for _ in swa:
        pass


def _mixers(at, bt, kt, rt, bh, kh, v, wl, bonus, g, lnx_g, lnx_b, seg, q, ak, av, bias, sinks):
    B, S, W = at.shape
    tc = min(MIX_TILE, S)
    nbat = MIX_BATCH if B % MIX_BATCH == 0 else 1
    nb = tc // ATT_BLOCK
    bias = bias.reshape(N_KV_HEADS, GQA_GROUP * ATT_BLOCK, ATT_BLOCK)
    const = lambda shape: pl.BlockSpec(shape, lambda b, j: (0,) * len(shape))
    rows = lambda w: pl.BlockSpec((nbat, tc, w), lambda b, j: (b, j, 0))
    prev_block = lambda w: pl.BlockSpec((nbat, ATT_BLOCK, w), lambda b, j: (b, jnp.maximum(j * nb - 1, 0), 0))
    return pl.pallas_call(
        _mixer_kernel,
        grid=(B // nbat, S // tc),
        in_specs=([rows(W)] * 7 + [pl.BlockSpec((nbat, tc // CHUNK, 1, W), lambda b, j: (b, j, 0, 0))]
                  + [rows(W)] * 2 + [const(lnx_g.shape), const(lnx_b.shape), const(seg.shape)]
                  + [rows(ATT_Q_W), rows(ATT_KV_W), prev_block(ATT_KV_W), rows(ATT_KV_W), prev_block(ATT_KV_W),
                     const(bias.shape), pl.BlockSpec(memory_space=pltpu.SMEM)]),
        out_specs=[rows(W), rows(ATT_Q_W)],
        out_shape=[jax.ShapeDtypeStruct((B, S, W), F32), jax.ShapeDtypeStruct((B, S, ATT_Q_W), BF16)],
        scratch_shapes=[pltpu.VMEM((nbat, RWKV_HEADS, CHUNK, 2 * HEAD_DIM), F32)],
        compiler_params=pltpu.CompilerParams(dimension_semantics=("arbitrary", "arbitrary"),
                                             vmem_limit_bytes=VMEM_LIMIT),
        name="rwkv7_swa_mixers",
    )(at, bt, kt, rt, bh, kh, v, wl, bonus, g, lnx_g, lnx_b, seg, q, ak, ak, av, av, bias, sinks)


def _swa_stages(q_ref, kc_ref, kp_ref, vc_ref, vp_ref, bias_ref, sink_ref, o_ref):
    first = pl.program_id(1) == 0
    rows = GQA_GROUP * ATT_BLOCK
    col = lax.broadcasted_iota(jnp.int32, (rows, ATT_BLOCK), 1)
    qpos = lax.broadcasted_iota(jnp.int32, (rows, ATT_BLOCK), 0) % ATT_BLOCK
    use_cur = col <= qpos
    no_key = jnp.logical_and(first, jnp.logical_not(use_cur))
    row = lax.broadcasted_iota(jnp.int32, (rows, 1), 0)
    sink = []
    for hk in range(N_KV_HEADS):
        s = jnp.full((rows, 1), sink_ref[hk * GQA_GROUP], F32)
        for g in range(1, GQA_GROUP):
            s = jnp.where(row >= g * ATT_BLOCK, sink_ref[hk * GQA_GROUP + g], s)
        sink.append(s * LOG2E)

    def window(ref_cur, ref_prev, bi, blk, hk):
        ks = slice(hk * HEAD_DIM, (hk + 1) * HEAD_DIM)
        prev = (ref_prev[bi, :, ks] if blk == 0
                else ref_cur[bi, pl.ds((blk - 1) * ATT_BLOCK, ATT_BLOCK), ks])
        return jnp.concatenate([prev, ref_cur[bi, pl.ds(blk * ATT_BLOCK, ATT_BLOCK), ks]], axis=0)

    items = [(bi, blk, hk) for bi in range(q_ref.shape[0]) for blk in range(q_ref.shape[1] // ATT_BLOCK)
             for hk in range(N_KV_HEADS)]
    logits = {}
    for bi, blk, hk in items:
        q = q_ref[bi, pl.ds(blk * ATT_BLOCK, ATT_BLOCK), :]
        qg = jnp.concatenate([q[:, (hk * GQA_GROUP + g) * HEAD_DIM:(hk * GQA_GROUP + g + 1) * HEAD_DIM]
                              for g in range(GQA_GROUP)], axis=0)
        lg = lax.dot_general(qg, window(kc_ref, kp_ref, bi, blk, hk), (((1,), (1,)), ((), ())),
                             preferred_element_type=F32)
        lg = jnp.where(use_cur, lg[:, ATT_BLOCK:], lg[:, :ATT_BLOCK]) + bias_ref[hk]
        logits[bi, blk, hk] = jnp.where(no_key, NEG_INF, lg) if blk == 0 else lg
    yield

    e2, e_sink = {}, {}
    for it in items:
        m = jnp.maximum(jnp.max(logits[it], axis=-1, keepdims=True), sink[it[2]])
        eb = jnp.exp2(logits.pop(it) - m).astype(BF16)
        e_sink[it] = jnp.exp2(sink[it[2]] - m)
        zero = jnp.zeros_like(eb)
        e2[it] = jnp.concatenate([jnp.where(use_cur, zero, eb), jnp.where(use_cur, eb, zero)], axis=1)
        yield

    low = lax.broadcasted_iota(jnp.int32, (ATT_BLOCK, 2 * HEAD_DIM), 1) < HEAD_DIM
    for it in items:
        bi, blk, hk = it
        v2 = window(vc_ref, vp_ref, bi, blk, hk)
        vw = jnp.concatenate([v2, v2, jnp.ones((2 * ATT_BLOCK, 2 * HEAD_DIM), BF16)], axis=1)
        res = jnp.dot(e2.pop(it), vw, preferred_element_type=F32)
        out = res[:, :2 * HEAD_DIM] / (res[:, 2 * HEAD_DIM:] + e_sink.pop(it))
        for g in range(0, GQA_GROUP, 2):
            h = hk * GQA_GROUP + g
            pair = jnp.where(low, out[g * ATT_BLOCK:(g + 1) * ATT_BLOCK], out[(g + 1) * ATT_BLOCK:(g + 2) * ATT_BLOCK])
            o_ref[bi, pl.ds(blk * ATT_BLOCK, ATT_BLOCK), h * HEAD_DIM:(h + 2) * HEAD_DIM] = pair.astype(BF16)
        yield


def _ffn_kernel(x_ref, yr_ref, ya_ref, mod_ref, g2_ref, wo_ref, wg_ref, wu_ref, wd_ref,
                bonus_ref, g_ref, lg_ref, lb_ref, seg_ref, o_ref):
    n_sub = max(x_ref.shape[1] // FFN_SUB, 1)
    tm = x_ref.shape[1] // n_sub
    subs = [_ffn_rows(pl.ds(s * tm, tm), x_ref, yr_ref, ya_ref, mod_ref, g2_ref, wo_ref, wg_ref, wu_ref,
                      wd_ref, bonus_ref, g_ref, lg_ref, lb_ref, seg_ref, o_ref) for s in range(n_sub)]
    done = [False] * n_sub
    t = 0
    while not all(done):
        for s in range(min(t + 1, n_sub)):
            if not done[s]:
                done[s] = next(subs[s], True) is True
        t += 1


def _ffn_rows(rs, x_ref, yr_ref, ya_ref, mod_ref, g2_ref, wo_ref, wg_ref, wu_ref, wd_ref,
              bonus_ref, g_ref, lg_ref, lb_ref, seg_ref, o_ref):
    x = x_ref[0, rs, :]
    wo = wo_ref[...]
    seg = seg_ref[...]
    yc = yr_ref[0, rs, :]
    yc = yc - _seg_sum(yc, seg) * (1.0 / HEAD_DIM)
    var = _seg_sum(yc * yc, seg) * (1.0 / HEAD_DIM)
    yn = yc * lax.rsqrt(var + RWKV_GN_EPS) * lg_ref[...] + lb_ref[...]
    y_rwkv = ((yn + bonus_ref[0, rs, :]) * g_ref[0, rs, :]).astype(BF16)
    mix = (jnp.dot(y_rwkv, wo[:RWKV_W], preferred_element_type=F32)
           + jnp.dot(ya_ref[0, rs, :], wo[RWKV_W:], preferred_element_type=F32))
    yield
    h_res = x + mod_ref[0, 2:3, :] * mix
    ms = jnp.mean(h_res * h_res, axis=-1, keepdims=True)
    h2 = (h_res * lax.rsqrt(ms + NORM_EPS) * (g2_ref[...] * (1.0 + mod_ref[0, 4:5, :]))
          + mod_ref[0, 3:4, :]).astype(BF16)
    yield
    d_ff = wg_ref.shape[1]
    ffn = jnp.zeros_like(x)
    for lo in range(0, d_ff, FFN_BLOCK):
        cs = slice(lo, min(lo + FFN_BLOCK, d_ff))
        gt = jnp.dot(h2, wg_ref[:, cs], preferred_element_type=F32)
        up = jnp.dot(h2, wu_ref[:, cs], preferred_element_type=F32)
        yield
        act = (gt * _sigmoid(gt) * up).astype(BF16)
        ffn = ffn + jnp.dot(act, wd_ref[cs, :], preferred_element_type=F32)
        yield
    o_ref[0, rs, :] = h_res + mod_ref[0, 5:6, :] * ffn


def _out_ffn(x, y_rwkv, y_att, mod3, g2, w_out, w_gate, w_up, w_down, bonus, g, lnx_g, lnx_b, seg):
    B, S, D = x.shape
    tm = min(FFN_TILE, S)
    resident = lambda shape: pl.BlockSpec(shape, lambda b, j: (0,) * len(shape),
                                          pipeline_mode=pl.Buffered(1))
    row_spec = lambda w: pl.BlockSpec((1, tm, w), lambda b, j: (b, j, 0))
    return pl.pallas_call(
        _ffn_kernel,
        grid=(B, S // tm),
        in_specs=[row_spec(D), row_spec(RWKV_W), row_spec(ATT_Q_W),
                  pl.BlockSpec((1, N_MOD, D), lambda b, j: (b, 0, 0)),
                  resident(g2.shape), resident(w_out.shape), resident(w_gate.shape),
                  resident(w_up.shape), resident(w_down.shape),
                  row_spec(RWKV_W), row_spec(RWKV_W), resident(lnx_g.shape), resident(lnx_b.shape),
                  resident(seg.shape)],
        out_specs=row_spec(D),
        out_shape=jax.ShapeDtypeStruct((B, S, D), F32),
        compiler_params=pltpu.CompilerParams(dimension_semantics=("arbitrary", "arbitrary"),
                                             vmem_limit_bytes=VMEM_LIMIT),
        name="out_proj_ffn",
    )(x, y_rwkv, y_att, mod3, g2, w_out, w_gate, w_up, w_down, bonus, g, lnx_g, lnx_b, seg)


def kernel(x, c, w_ada, b_ada, norm1_g, w_in, rwkv_mu, w0, w_lora_up, a0, a_lora_up, g_lora_up, k_k, k_a, r_k, lnx_g, lnx_b, q_norm_g, k_norm_g, sinks, rel_bias, w_out, norm2_g, w_gate, w_up, w_down):
    B, S, D = x.shape
    row = lambda t: t.reshape(1, -1).astype(F32)
    lane = jnp.arange(MXU_TILE) // HEAD_DIM
    seg = (lane[:, None] == lane[None, :]).astype(BF16)
    t_idx = jnp.arange(min(IN_SUB, S))
    tri = ((t_idx[:, None] >= t_idx[None, :])
           & (t_idx[:, None] // CHUNK == t_idx[None, :] // CHUNK)).astype(BF16)

    mod3 = _modulation(c.astype(F32), w_ada, b_ada).reshape(B, N_MOD, D)
    bias = _bias_table(rel_bias)

    (at, bt, kt, rt, bh, kh, v, wl, bonus, g, q, ak, av,
     w_out16, w_gate16, w_up16, w_down16) = _in_projection(
        x, mod3, row(norm1_g), w_in.astype(BF16), row(rwkv_mu), row(w0), w_lora_up.astype(F32),
        row(a0), a_lora_up.astype(F32), g_lora_up.astype(F32), row(k_k), row(k_a), row(r_k),
        row(jnp.tile(q_norm_g, N_Q_HEADS)), row(jnp.tile(k_norm_g, N_KV_HEADS)), seg, tri,
        (w_out.astype(F32), w_gate.astype(F32), w_up.astype(F32), w_down.astype(F32)))

    y_rwkv, y_att = _mixers(at, bt, kt, rt, bh, kh, v, wl, bonus, g, row(lnx_g), row(lnx_b), seg,
                            q, ak, av, bias, sinks)

    out = _out_ffn(x, y_rwkv, y_att, mod3, row(norm2_g), w_out16, w_gate16, w_up16, w_down16,
                   bonus, g, row(lnx_g), row(lnx_b), seg)
    return out.astype(x.dtype)
```
